```python
import math
import jax, jax.numpy as jnp
from jax import lax
import numpy as np

D_MODEL = 1024
BATCH = 1
SEQ = 16384
DEPTH = 4

HEAD_DIM = 64
N_HEADS_A = D_MODEL // (2 * HEAD_DIM)
N_HEADS_B = D_MODEL // (2 * HEAD_DIM)
N_HEADS_C = D_MODEL // (2 * HEAD_DIM)
A_WIDTH = N_HEADS_A * HEAD_DIM
B_WIDTH = N_HEADS_B * HEAD_DIM
HYB_IN = 3 * A_WIDTH + 3 * B_WIDTH + N_HEADS_B
DIFF_QK = 2 * N_HEADS_C * HEAD_DIM
DIFF_V = N_HEADS_C * 2 * HEAD_DIM
D_FF = 2816
CONV_WIDTH = 3
ROPE_THETA = 10000.0
DILATED_PATTERNS = ((128, 1), (512, 4), (2048, 16))
BLOCK_Q = 128
NORM_EPS = 1e-6
SUBLN_EPS = 1e-5
N_EVEN = (DEPTH + 1) // 2
N_ODD = DEPTH // 2

kernel_name = "hybrid_dilated_fox_diffattn_convglu"


def rms_norm(x, g, eps=NORM_EPS):
    xf = x.astype(jnp.float32)
    y = xf * lax.rsqrt(jnp.mean(xf * xf, axis=-1, keepdims=True) + eps)
    return (y * g.astype(jnp.float32)).astype(x.dtype)


def rope_tables(seq):
    inv = 1.0 / (ROPE_THETA ** (jnp.arange(0, HEAD_DIM, 2, dtype=jnp.float32) / HEAD_DIM))
    ang = jnp.arange(seq, dtype=jnp.float32)[:, None] * inv[None, :]
    ang = jnp.concatenate([ang, ang], axis=-1)
    return jnp.cos(ang)[None, :, None, :], jnp.sin(ang)[None, :, None, :]


def apply_rope(x, cos, sin):
    x1, x2 = jnp.split(x, 2, axis=-1)
    rot = jnp.concatenate([-x2, x1], axis=-1)
    return (x.astype(jnp.float32) * cos + rot.astype(jnp.float32) * sin).astype(x.dtype)


def local_window_attention(q, k, v, span):
    n, L, h, dh = q.shape
    nb = -(-L // BLOCK_Q)
    lp = nb * BLOCK_Q
    pad = lp - L
    padt = lambda t: jnp.pad(t, ((0, 0), (0, pad), (0, 0), (0, 0)))
    q, k, v = padt(q), padt(k), padt(v)

    def blocks_with_prev(t):
        te = jnp.pad(t, ((0, 0), (BLOCK_Q, 0), (0, 0), (0, 0)))
        prev = te[:, :lp].reshape(n, nb, BLOCK_Q, h, t.shape[-1])
        cur = te[:, BLOCK_Q:].reshape(n, nb, BLOCK_Q, h, t.shape[-1])
        return jnp.concatenate([prev, cur], axis=2)

    kb, vb = blocks_with_prev(k), blocks_with_prev(v)
    qb = q.reshape(n, nb, BLOCK_Q, h, dh)
    s = jnp.einsum('nbqhd,nbkhd->nbhqk', qb, kb, preferred_element_type=jnp.float32)
    i = jnp.arange(BLOCK_Q)[:, None]
    j = jnp.arange(2 * BLOCK_Q)[None, :]
    delta = i - j + BLOCK_Q
    band = (delta >= 0) & (delta <= span)
    blk = jnp.arange(nb)[:, None, None]
    valid = band[None] & (blk * BLOCK_Q - BLOCK_Q + j[None] >= 0)
    s = jnp.where(valid[None, :, None], s, -jnp.inf)
    m = jnp.max(s, axis=-1, keepdims=True)
    p = jnp.exp(s - m)
    denom = jnp.sum(p, axis=-1, keepdims=True)
    o = jnp.einsum('nbhqk,nbkhd->nbqhd', (p / denom).astype(v.dtype), vb)
    lse = (m + jnp.log(denom))[..., 0]
    lse = jnp.transpose(lse, (0, 1, 3, 2)).reshape(n, lp, h)[:, :L]
    o = o.reshape(n, lp, h, dh)[:, :L]
    return o, lse


def dilated_attention(q, k, v):
    b, s, h, dh = q.shape
    outs, lses = [], []
    for window, dil in DILATED_PATTERNS:
        L = s // dil

        def to_sub(t):
            t = t.reshape(b, L, dil, *t.shape[2:])
            return jnp.swapaxes(t, 1, 2).reshape(b * dil, L, *t.shape[3:])

        def from_sub(t):
            t = t.reshape(b, dil, L, *t.shape[2:])
            return jnp.swapaxes(t, 1, 2).reshape(b, s, *t.shape[3:])

        o, lse = local_window_attention(to_sub(q), to_sub(k), to_sub(v), window // dil)
        outs.append(from_sub(o))
        lses.append(from_sub(lse))
    alpha = jax.nn.softmax(jnp.stack(lses, axis=0), axis=0)
    out = jnp.sum(alpha[..., None] * jnp.stack(outs, axis=0).astype(jnp.float32), axis=0)
    return out.astype(q.dtype)


def to_blocks(t):
    b, s = t.shape[:2]
    return jnp.moveaxis(t.reshape(b, s // BLOCK_Q, BLOCK_Q, *t.shape[2:]), 1, 0)


def from_blocks(t):
    nb, b, bq = t.shape[:3]
    return jnp.moveaxis(t, 0, 1).reshape(b, nb * bq, *t.shape[3:])


def causal_block_probs(qb, k, q_start, bias=None):
    s = jnp.einsum('bqhd,bkhd->bhqk', qb, k, preferred_element_type=jnp.float32)
    if bias is not None:
        s = s + bias
    qpos = q_start + jnp.arange(qb.shape[1])
    kpos = jnp.arange(k.shape[1])
    s = jnp.where(kpos[None, :] <= qpos[:, None], s, -jnp.inf)
    return jax.nn.softmax(s, axis=-1)


def forgetting_attention(q, k, v, logf):
    c = jnp.cumsum(logf, axis=1)
    ck = jnp.transpose(c, (0, 2, 1))
    nb = q.shape[1] // BLOCK_Q

    def step(args):
        start, qb, cb = args
        bias = jnp.transpose(cb, (0, 2, 1))[..., :, None] - ck[:, :, None, :]
        p = causal_block_probs(qb, k, start, bias)
        return jnp.einsum('bhqk,bkhd->bqhd', p.astype(v.dtype), v)

    out = lax.map(step, (jnp.arange(nb) * BLOCK_Q, to_blocks(q), to_blocks(c)))
    return from_blocks(out)


def differential_attention(q1, q2, k1, k2, v, lam):
    nb = q1.shape[1] // BLOCK_Q

    def step(args):
        start, q1b, q2b = args
        a = causal_block_probs(q1b, k1, start) - lam * causal_block_probs(q2b, k2, start)
        return jnp.einsum('bhqk,bkhd->bqhd', a.astype(v.dtype), v)

    out = lax.map(step, (jnp.arange(nb) * BLOCK_Q, to_blocks(q1), to_blocks(q2)))
    return from_blocks(out)


def hybrid_mixer(n, w_in, b_f, w_out, cos, sin):
    b, s, _ = n.shape
    proj = n @ w_in
    cuts = [A_WIDTH, 2 * A_WIDTH, 3 * A_WIDTH,
            3 * A_WIDTH + B_WIDTH, 3 * A_WIDTH + 2 * B_WIDTH, 3 * A_WIDTH + 3 * B_WIDTH]
    qa, ka, va, qb, kb, vb, fb = jnp.split(proj, cuts, axis=-1)
    heads = lambda t: t.reshape(b, s, -1, HEAD_DIM)
    scale = HEAD_DIM ** -0.5
    oa = dilated_attention(apply_rope(heads(qa), cos, sin) * scale,
                           apply_rope(heads(ka), cos, sin), heads(va))
    logf = jax.nn.log_sigmoid((fb + b_f).astype(jnp.float32))
    ob = forgetting_attention(heads(qb) * scale, heads(kb), heads(vb), logf)
    o = jnp.concatenate([oa.reshape(b, s, A_WIDTH), ob.reshape(b, s, B_WIDTH)], axis=-1)
    return o @ w_out


def diff_mixer(n, w_qkv, lam_params, subln_g, w_out, cos, sin, layer_idx):
    b, s, _ = n.shape
    q, k, v = jnp.split(n @ w_qkv, [DIFF_QK, 2 * DIFF_QK], axis=-1)
    scale = HEAD_DIM ** -0.5
    q = (apply_rope(q.reshape(b, s, 2 * N_HEADS_C, HEAD_DIM), cos, sin) * scale).reshape(b, s, N_HEADS_C, 2, HEAD_DIM)
    k = apply_rope(k.reshape(b, s, 2 * N_HEADS_C, HEAD_DIM), cos, sin).reshape(b, s, N_HEADS_C, 2, HEAD_DIM)
    v = v.reshape(b, s, N_HEADS_C, 2 * HEAD_DIM)
    lam_init = 0.8 - 0.6 * math.exp(-0.3 * layer_idx)
    lp = lam_params.astype(jnp.float32)
    lam = jnp.exp(jnp.sum(lp[0] * lp[1])) - jnp.exp(jnp.sum(lp[2] * lp[3])) + lam_init
    o = differential_attention(q[:, :, :, 0], q[:, :, :, 1], k[:, :, :, 0], k[:, :, :, 1], v, lam)
    o = rms_norm(o, subln_g, SUBLN_EPS) * (1.0 - lam_init)
    return o.reshape(b, s, DIFF_V) @ w_out


def conv_glu_ffn(n, w_up, conv_w, conv_b, w_down):
    s = n.shape[1]
    gate, up = jnp.split(n @ w_up, 2, axis=-1)
    gp = jnp.pad(gate, ((0, 0), (CONV_WIDTH - 1, 0), (0, 0)))
    conv = conv_b
    for j in range(CONV_WIDTH):
        conv = conv + gp[:, j:j + s] * conv_w[j]
    return (jax.nn.silu(conv) * up) @ w_down


def setup_inputs(seed: int = 0) -> dict:
    key = jax.random.key(seed)
    ks = jax.random.split(key, 20)
    nrm = lambda k, shape, fan_in: jax.random.normal(k, shape, jnp.float32) * fan_in ** -0.5
    return {
        "x": jax.random.normal(ks[0], (BATCH, SEQ, D_MODEL), jnp.float32),
        "attn_norm": 1.0 + 0.02 * jax.random.normal(ks[1], (DEPTH, D_MODEL), jnp.float32),
        "ffn_norm": 1.0 + 0.02 * jax.random.normal(ks[2], (DEPTH, D_MODEL), jnp.float32),
        "final_norm": 1.0 + 0.02 * jax.random.normal(ks[3], (D_MODEL,), jnp.float32),
        "hyb_w_in": nrm(ks[4], (N_EVEN, D_MODEL, HYB_IN), D_MODEL),
        "hyb_b_f": 2.0 + 0.5 * jax.random.normal(ks[5], (N_EVEN, N_HEADS_B), jnp.float32),
        "hyb_w_out": nrm(ks[6], (N_EVEN, A_WIDTH + B_WIDTH, D_MODEL), A_WIDTH + B_WIDTH),
        "diff_w_qkv": nrm(ks[7], (N_ODD, D_MODEL, 2 * DIFF_QK + DIFF_V), D_MODEL),
        "diff_lambda": 0.1 * jax.random.normal(ks[8], (N_ODD, 4, HEAD_DIM), jnp.float32),
        "diff_subln": 1.0 + 0.02 * jax.random.normal(ks[9], (N_ODD, 2 * HEAD_DIM), jnp.float32),
        "diff_w_out": nrm(ks[10], (N_ODD, DIFF_V, D_MODEL), DIFF_V),
        "ffn_w_up": nrm(ks[11], (DEPTH, D_MODEL, 2 * D_FF), D_MODEL),
        "ffn_conv_w": nrm(ks[12], (DEPTH, CONV_WIDTH, D_FF), CONV_WIDTH),
        "ffn_conv_b": 0.01 * jax.random.normal(ks[13], (DEPTH, D_FF), jnp.float32),
        "ffn_w_down": nrm(ks[14], (DEPTH, D_FF, D_MODEL), D_FF),
    }


def reference(x, attn_norm, ffn_norm, final_norm, hyb_w_in, hyb_b_f, hyb_w_out,
              diff_w_qkv, diff_lambda, diff_subln, diff_w_out,
              ffn_w_up, ffn_conv_w, ffn_conv_b, ffn_w_down):
    cos, sin = rope_tables(x.shape[1])
    h = x
    for l in range(DEPTH):
        n = rms_norm(h, attn_norm[l])
        if l % 2 == 0:
            e = l // 2
            h = h + hybrid_mixer(n, hyb_w_in[e], hyb_b_f[e], hyb_w_out[e], cos, sin)
        else:
            o = l // 2
            h = h + diff_mixer(n, diff_w_qkv[o], diff_lambda[o], diff_subln[o], diff_w_out[o], cos, sin, l)
        h = h + conv_glu_ffn(rms_norm(h, ffn_norm[l]), ffn_w_up[l], ffn_conv_w[l], ffn_conv_b[l], ffn_w_down[l])
    return rms_norm(h, final_norm)
```

```python
import functools
import math

import jax
import jax.numpy as jnp
from jax import lax
from jax.experimental import pallas as pl
from jax.experimental.pallas import tpu as pltpu

F32 = jnp.float32
BF16 = jnp.bfloat16

HEAD_DIM = 64
LANES = 128
BF16_SUBLANES = 16
ROPE_THETA = 10000.0
DILATED_PATTERNS = ((128, 1), (512, 4), (2048, 16))
BAND = 128
NORM_EPS = 1e-6
SUBLN_EPS = 1e-5
CONV_WIDTH = 3
NEG = -1e30
VMEM_LIMIT = 48 * 1024 * 1024

NT_DIMS = (((1,), (1,)), ((), ()))


def _params(sem):
    return pltpu.CompilerParams(dimension_semantics=sem, vmem_limit_bytes=VMEM_LIMIT)


def _rms(x, g, eps):
    return x * lax.rsqrt(jnp.mean(x * x, axis=-1, keepdims=True) + eps) * g


def _lane_lo(rows):
    return lax.broadcasted_iota(jnp.int32, (rows, LANES), 1) < HEAD_DIM


def _rope_tile(x, cos, sin_signed):
    rows = x.shape[0]
    first_half = (lax.broadcasted_iota(jnp.int32, (rows, LANES), 1) & 32) == 0
    out = []
    for c in range(x.shape[1] // LANES):
        xc = x[:, c * LANES:(c + 1) * LANES]
        ahead = pltpu.roll(xc, LANES - 32, 1)
        behind = pltpu.roll(xc, 32, 1)
        rot = jnp.where(first_half, ahead, behind)
        out.append(xc * cos + rot * sin_signed)
    return jnp.concatenate(out, axis=1)


def _norm_proj_kernel(h_ref, g_ref, w_ref, cos_ref, sin_ref, o_ref, n_scr, *,
                      rope_tiles, scale_tiles, scale):
    j = pl.program_id(1)

    @pl.when(j == 0)
    def _():
        n_scr[...] = _rms(h_ref[...], g_ref[...], NORM_EPS).astype(BF16)

    acc = jnp.dot(n_scr[...], w_ref[...], preferred_element_type=F32)
    is_rope = functools.reduce(jnp.logical_or, [j == t for t in rope_tiles])
    is_scaled = functools.reduce(jnp.logical_or, [j == t for t in scale_tiles])
    sc = jnp.where(is_scaled, scale, 1.0).astype(F32)

    @pl.when(is_rope)
    def _():
        o_ref[...] = (_rope_tile(acc, cos_ref[...], sin_ref[...]) * sc).astype(BF16)

    @pl.when(jnp.logical_not(is_rope))
    def _():
        o_ref[...] = (acc * sc).astype(BF16)


def _norm_proj(h, g, w, cos, sin_signed, *, rope_tiles, scale_tiles, tm, tn):
    s, d = h.shape
    n = w.shape[1]
    kern = functools.partial(_norm_proj_kernel, rope_tiles=rope_tiles,
                             scale_tiles=scale_tiles, scale=HEAD_DIM ** -0.5)
    return pl.pallas_call(
        kern,
        grid=(s // tm, n // tn),
        in_specs=[
            pl.BlockSpec((tm, d), lambda i, j: (i, 0)),
            pl.BlockSpec((1, d), lambda i, j: (0, 0)),
            pl.BlockSpec((d, tn), lambda i, j: (0, j)),
            pl.BlockSpec((tm, LANES), lambda i, j: (i, 0)),
            pl.BlockSpec((tm, LANES), lambda i, j: (i, 0)),
        ],
        out_specs=pl.BlockSpec((tm, tn), lambda i, j: (i, j)),
        out_shape=jax.ShapeDtypeStruct((s, n), BF16),
        scratch_shapes=[pltpu.VMEM((tm, d), BF16)],
        compiler_params=_params(("parallel", "arbitrary")),
        name="norm_proj",
    )(h, g, w, cos, sin_signed)


def _fox_gate_kernel(h_ref, g_ref, wf_ref, bf_ref, c_ref, carry, *, tc):
    i = pl.program_id(0)

    @pl.when(i == 0)
    def _():
        carry[...] = jnp.zeros_like(carry)

    n = _rms(h_ref[...], g_ref[...], NORM_EPS).astype(BF16)
    z = lax.dot_general(wf_ref[...], n, NT_DIMS, preferred_element_type=F32) + bf_ref[...]
    logf = jnp.minimum(z, 0.0) - jnp.log(1.0 + jnp.exp(-jnp.abs(z)))
    src = lax.broadcasted_iota(jnp.int32, (tc, tc), 0)
    dst = lax.broadcasted_iota(jnp.int32, (tc, tc), 1)
    prefix = jnp.where(src <= dst, 1.0, 0.0).astype(F32)
    cs = jnp.dot(logf, prefix, precision=lax.Precision.HIGHEST,
                 preferred_element_type=F32) + carry[:, 0:1]
    c_ref[...] = cs
    carry[...] = jnp.broadcast_to(cs[:, tc - 1:tc], carry.shape)


def _fox_gate(h, g, wf_t, b_f, *, tc):
    s, d = h.shape
    nh = wf_t.shape[0]
    return pl.pallas_call(
        functools.partial(_fox_gate_kernel, tc=tc),
        grid=(s // tc,),
        in_specs=[
            pl.BlockSpec((tc, d), lambda i: (i, 0)),
            pl.BlockSpec((1, d), lambda i: (0, 0)),
            pl.BlockSpec((nh, d), lambda i: (0, 0)),
            pl.BlockSpec((nh, 1), lambda i: (0, 0)),
        ],
        out_specs=pl.BlockSpec((nh, tc), lambda i: (0, i)),
        out_shape=jax.ShapeDtypeStruct((nh, s), F32),
        scratch_shapes=[pltpu.VMEM((nh, LANES), F32)],
        compiler_params=_params(("arbitrary",)),
        name="fox_gate",
    )(h, g, wf_t, b_f)


def _dilated_kernel(q_ref, kc_ref, kh_ref, vc_ref, vh_ref, o_ref, l_ref, kbuf, vbuf, *, tq):
    i = pl.program_id(1)
    kbuf[0:BAND] = kh_ref[...]
    kbuf[BAND:] = kc_ref[...]
    vbuf[0:BAND] = vh_ref[...]
    vbuf[BAND:] = vc_ref[...]
    row = lax.broadcasted_iota(jnp.int32, (BAND, 2 * BAND), 0)
    col = lax.broadcasted_iota(jnp.int32, (BAND, 2 * BAND), 1)
    delta = row - col + BAND
    band = (delta >= 0) & (delta <= BAND)
    lo = _lane_lo(BAND)
    for a in range(tq // BAND):
        first_key = i * tq + (a - 1) * BAND
        valid = band & (col + first_key >= 0)
        rows = slice(a * BAND, (a + 1) * BAND)
        for hp in range(q_ref.shape[1] // LANES):
            lanes = slice(hp * LANES, (hp + 1) * LANES)
            q = q_ref[rows, lanes]
            kk = kbuf[a * BAND:(a + 2) * BAND, lanes]
            vv = vbuf[a * BAND:(a + 2) * BAND, lanes]
            outs, lses = [], []
            for qh in (jnp.where(lo, q, jnp.zeros_like(q)), jnp.where(lo, jnp.zeros_like(q), q)):
                s = lax.dot_general(qh, kk, NT_DIMS, preferred_element_type=F32)
                s = jnp.where(valid, s, NEG)
                m = jnp.max(s, axis=1, keepdims=True)
                p = jnp.exp(s - m)
                den = jnp.sum(p, axis=1, keepdims=True)
                pv = jnp.dot(p.astype(BF16), vv, preferred_element_type=F32)
                outs.append(pv / den)
                lses.append(jnp.broadcast_to(m + jnp.log(den), (BAND, LANES)))
            o_ref[rows, lanes] = jnp.where(lo, outs[0], outs[1])
            l_ref[rows, lanes] = jnp.where(lo, lses[0], lses[1])


def _dilated_pattern(proj, dil, *, width, tq):
    s, n = proj.shape
    L = s // dil
    view = proj.reshape(L, dil * n)
    per_res = n // width
    halo_per_tile = tq // BAND
    cur = lambda part: pl.BlockSpec((tq, width), lambda r, i: (i, r * per_res + part))
    halo = lambda part: pl.BlockSpec(
        (BAND, width), lambda r, i: (jnp.maximum(i * halo_per_tile - 1, 0), r * per_res + part))
    out_spec = pl.BlockSpec((tq, width), lambda r, i: (i, r))
    o, lse = pl.pallas_call(
        functools.partial(_dilated_kernel, tq=tq),
        grid=(dil, L // tq),
        in_specs=[cur(0), cur(1), halo(1), cur(2), halo(2)],
        out_specs=[out_spec, out_spec],
        out_shape=[jax.ShapeDtypeStruct((L, dil * width), F32)] * 2,
        scratch_shapes=[pltpu.VMEM((tq + BAND, width), BF16)] * 2,
        compiler_params=_params(("parallel", "arbitrary")),
        name=f"dilated_d{dil}",
    )(view, view, view, view, view)
    return o.reshape(s, width), lse.reshape(s, width)


def _flash_pair_kernel(qi_ref, kj_ref, q_ref, k_ref, v_ref, *rest, mode, tq, tk, lam_init):
    if mode == "fox":
        c_ref, o_ref, qz, m_scr, l_scr, acc = rest
    else:
        lam_ref, g_ref, o_ref, qz, m_scr, l_scr, acc = rest
    step = pl.program_id(1)
    qi = qi_ref[step]
    kj = kj_ref[step]
    last_kj = (qi * tq + tq - 1) // tk
    lo = _lane_lo(tq)

    @pl.when(kj == 0)
    def _():
        q = q_ref[...]
        qz[0] = jnp.where(lo, q, jnp.zeros_like(q))
        qz[1] = jnp.where(lo, jnp.zeros_like(q), q)
        m_scr[...] = jnp.full_like(m_scr, NEG)
        l_scr[...] = jnp.zeros_like(l_scr)
        acc[...] = jnp.zeros_like(acc)

    def body(masked):
        k = k_ref[...]
        v = v_ref[...]
        if masked:
            row = qi * tq + lax.broadcasted_iota(jnp.int32, (tq, tk), 0)
            col = kj * tk + lax.broadcasted_iota(jnp.int32, (tq, tk), 1)
            keep = col <= row
        for a in range(2):
            s = lax.dot_general(qz[a], k, NT_DIMS, preferred_element_type=F32)
            if mode == "fox":
                s = s - c_ref[a:a + 1, :]
            if masked:
                s = jnp.where(keep, s, NEG)
            m_prev = m_scr[a]
            m_next = jnp.maximum(m_prev, jnp.max(s, axis=1, keepdims=True))
            alpha = jnp.exp(m_prev - m_next)
            p = jnp.exp(s - pltpu.repeat(m_next, tk // LANES, axis=1))
            l_scr[a] = alpha * l_scr[a] + jnp.sum(p, axis=1, keepdims=True)
            acc[a] = alpha * acc[a] + jnp.dot(p.astype(BF16), v, preferred_element_type=F32)
            m_scr[a] = m_next

    needs_mask = kj * tk + tk - 1 > qi * tq
    pl.when(needs_mask)(lambda: body(True))
    pl.when(jnp.logical_not(needs_mask))(lambda: body(False))

    @pl.when(kj == last_kj)
    def _():
        o0 = acc[0] / l_scr[0]
        o1 = acc[1] / l_scr[1]
        if mode == "fox":
            o_ref[...] = jnp.where(lo, o0, o1).astype(BF16)
        else:
            lp = lam_ref[...]
            t1 = jnp.sum(lp[0:1] * lp[1:2], axis=1, keepdims=True)
            t2 = jnp.sum(lp[2:3] * lp[3:4], axis=1, keepdims=True)
            lam = jnp.exp(t1) - jnp.exp(t2) + lam_init
            o = o0 - lam * o1
            o_ref[...] = (_rms(o, g_ref[...], SUBLN_EPS) * (1.0 - lam_init)).astype(BF16)


def _causal_steps(s, tq, tk):
    qi, kj = [], []
    for i in range(s // tq):
        for j in range((i * tq + tq - 1) // tk + 1):
            qi.append(i)
            kj.append(j)
    return jnp.asarray(qi, jnp.int32), jnp.asarray(kj, jnp.int32)


def _flash_pair(proj, extras, *, mode, n_groups, q_col, k_col, v_col, tq, tk, lam_init=0.0):
    s = proj.shape[0]
    qi, kj = _causal_steps(s, tq, tk)
    in_specs = [
        pl.BlockSpec((tq, LANES), lambda g, t, qi, kj: (qi[t], q_col + g)),
        pl.BlockSpec((tk, LANES), lambda g, t, qi, kj: (kj[t], k_col + g)),
        pl.BlockSpec((tk, LANES), lambda g, t, qi, kj: (kj[t], v_col + g)),
    ]
    if mode == "fox":
        (c,) = extras
        in_specs.append(pl.BlockSpec((None, 2, tk), lambda g, t, qi, kj: (g, 0, kj[t])))
    else:
        lam_params, subln_g = extras
        in_specs.append(pl.BlockSpec(lam_params.shape, lambda g, t, qi, kj: (0, 0)))
        in_specs.append(pl.BlockSpec(subln_g.shape, lambda g, t, qi, kj: (0, 0)))
    kern = functools.partial(_flash_pair_kernel, mode=mode, tq=tq, tk=tk, lam_init=lam_init)
    return pl.pallas_call(
        kern,
        grid_spec=pltpu.PrefetchScalarGridSpec(
            num_scalar_prefetch=2,
            grid=(n_groups, qi.shape[0]),
            in_specs=in_specs,
            out_specs=pl.BlockSpec((tq, LANES), lambda g, t, qi, kj: (qi[t], g)),
            scratch_shapes=[
                pltpu.VMEM((2, tq, LANES), BF16),
                pltpu.VMEM((2, tq, LANES), F32),
                pltpu.VMEM((2, tq, LANES), F32),
                pltpu.VMEM((2, tq, LANES), F32),
            ],
        ),
        out_shape=jax.ShapeDtypeStruct((s, n_groups * LANES), BF16),
        compiler_params=_params(("parallel", "arbitrary")),
        name=f"flash_{mode}",
    )(qi, kj, proj, proj, proj, *extras)


def _hyb_out_kernel(o1, o2, o3, l1, l2, l3, ob_ref, w_ref, h_ref, out_ref):
    a1, a2, a3 = l1[...], l2[...], l3[...]
    m = jnp.maximum(jnp.maximum(a1, a2), a3)
    e1, e2, e3 = jnp.exp(a1 - m), jnp.exp(a2 - m), jnp.exp(a3 - m)
    oa = (e1 * o1[...] + e2 * o2[...] + e3 * o3[...]) / (e1 + e2 + e3)
    wa = oa.shape[1]
    acc = jnp.dot(oa.astype(BF16), w_ref[0:wa, :], preferred_element_type=F32)
    acc = acc + jnp.dot(ob_ref[...], w_ref[wa:, :], preferred_element_type=F32)
    out_ref[...] = h_ref[...] + acc


def _hyb_out(pattern_outs, ob, w, h, *, tm):
    s, d = h.shape
    wa = pattern_outs[0][0].shape[1]
    part = pl.BlockSpec((tm, wa), lambda i: (i, 0))
    os_ = [o for o, _ in pattern_outs]
    ls_ = [l for _, l in pattern_outs]
    return pl.pallas_call(
        _hyb_out_kernel,
        grid=(s // tm,),
        in_specs=[part] * 6 + [
            pl.BlockSpec((tm, ob.shape[1]), lambda i: (i, 0)),
            pl.BlockSpec(w.shape, lambda i: (0, 0)),
            pl.BlockSpec((tm, d), lambda i: (i, 0)),
        ],
        out_specs=pl.BlockSpec((tm, d), lambda i: (i, 0)),
        out_shape=jax.ShapeDtypeStruct((s, d), F32),
        compiler_params=_params(("parallel",)),
        name="hyb_out",
    )(*os_, *ls_, ob, w, h)


def _proj_res_kernel(a_ref, w_ref, h_ref, out_ref):
    out_ref[...] = h_ref[...] + jnp.dot(a_ref[...], w_ref[...], preferred_element_type=F32)


def _proj_res(a, w, h, *, tm):
    s, d = h.shape
    return pl.pallas_call(
        _proj_res_kernel,
        grid=(s // tm,),
        in_specs=[
            pl.BlockSpec((tm, a.shape[1]), lambda i: (i, 0)),
            pl.BlockSpec(w.shape, lambda i: (0, 0)),
            pl.BlockSpec((tm, d), lambda i: (i, 0)),
        ],
        out_specs=pl.BlockSpec((tm, d), lambda i: (i, 0)),
        out_shape=jax.ShapeDtypeStruct((s, d), F32),
        compiler_params=_params(("parallel",)),
        name="proj_res",
    )(a, w, h)


HALO = BF16_SUBLANES


def _ffn_kernel(h_ref, halo_ref, g_ref, wg_ref, wu_ref, cw_ref, cb_ref, wd_ref, *rest,
                tm, final):
    if final:
        fg_ref, out_ref, n_scr, gate_scr, acc = rest
    else:
        out_ref, n_scr, gate_scr, acc = rest
    i = pl.program_id(0)
    j = pl.program_id(1)

    @pl.when(j == 0)
    def _():
        g = g_ref[...]
        prev = jnp.where(i > 0, halo_ref[...], 0.0)
        n_scr[0:HALO] = _rms(prev, g, NORM_EPS).astype(BF16)
        n_scr[HALO:] = _rms(h_ref[...], g, NORM_EPS).astype(BF16)
        acc[...] = jnp.zeros_like(acc)

    gate_scr[...] = jnp.dot(n_scr[...], wg_ref[...], preferred_element_type=F32)
    up = jnp.dot(n_scr[HALO:], wu_ref[...], preferred_element_type=F32)
    conv = cb_ref[...]
    for t in range(CONV_WIDTH):
        start = HALO - (CONV_WIDTH - 1) + t
        conv = conv + gate_scr[start:start + tm] * cw_ref[t:t + 1, :]
    act = conv * (1.0 / (1.0 + jnp.exp(-conv))) * up
    acc[...] += jnp.dot(act.astype(BF16), wd_ref[...], preferred_element_type=F32)

    @pl.when(j == pl.num_programs(1) - 1)
    def _():
        y = h_ref[...] + acc[...]
        if final:
            y = _rms(y, fg_ref[...], NORM_EPS)
        out_ref[...] = y


def _ffn(h, g, w_up, conv_w, conv_b, w_down, final_g, *, tm, tf):
    s, d = h.shape
    d_ff = w_down.shape[0]
    nf = d_ff // tf
    final = final_g is not None
    in_specs = [
        pl.BlockSpec((tm, d), lambda i, j: (i, 0)),
        pl.BlockSpec((HALO, d), lambda i, j: (jnp.maximum(i * (tm // HALO) - 1, 0), 0)),
        pl.BlockSpec((1, d), lambda i, j: (0, 0)),
        pl.BlockSpec((d, tf), lambda i, j: (0, j)),
        pl.BlockSpec((d, tf), lambda i, j: (0, nf + j)),
        pl.BlockSpec((CONV_WIDTH, tf), lambda i, j: (0, j)),
        pl.BlockSpec((1, tf), lambda i, j: (0, j)),
        pl.BlockSpec((tf, d), lambda i, j: (j, 0)),
    ]
    args = [h, h, g, w_up, w_up, conv_w, conv_b, w_down]
    if final:
        in_specs.append(pl.BlockSpec((1, d), lambda i, j: (0, 0)))
        args.append(final_g)
    return pl.pallas_call(
        functools.partial(_ffn_kernel, tm=tm, final=final),
        grid=(s // tm, nf),
        in_specs=in_specs,
        out_specs=pl.BlockSpec((tm, d), lambda i, j: (i, 0)),
        out_shape=jax.ShapeDtypeStruct((s, d), F32),
        scratch_shapes=[
            pltpu.VMEM((tm + HALO, d), BF16),
            pltpu.VMEM((tm + HALO, tf), F32),
            pltpu.VMEM((tm, d), F32),
        ],
        compiler_params=_params(("parallel", "arbitrary")),
        name="ffn_final" if final else "ffn",
    )(*args)


def _rope_tables(s):
    inv = 1.0 / (ROPE_THETA ** (jnp.arange(0, HEAD_DIM, 2, dtype=F32) / HEAD_DIM))
    ang = jnp.arange(s, dtype=F32)[:, None] * inv[None, :]
    cos = jnp.tile(jnp.cos(ang), (1, LANES // 32))
    sign = jnp.where((jnp.arange(LANES) & 32) == 0, -1.0, 1.0).astype(F32)
    sin_signed = jnp.tile(jnp.sin(ang), (1, LANES // 32)) * sign[None, :]
    return cos, sin_signed


def kernel(x, attn_norm, ffn_norm, final_norm, hyb_w_in, hyb_b_f, hyb_w_out, diff_w_qkv,
           diff_lambda, diff_subln, diff_w_out, ffn_w_up, ffn_conv_w, ffn_conv_b, ffn_w_down):
    b, s, d = x.shape
    assert b == 1
    depth = attn_norm.shape[0]
    width = hyb_w_out.shape[1] // 2
    n_pairs = width // LANES
    n_diff_heads = diff_w_out.shape[1] // LANES
    tm = min(1024, s)
    tq_flash, tk_flash = min(1024, s), min(512, s)
    cos, sin_signed = _rope_tables(s)
    h = x[0]

    for l in range(depth):
        g_attn = attn_norm[l][None, :]
        if l % 2 == 0:
            e = l // 2
            w_in = hyb_w_in[e]
            proj = _norm_proj(h, g_attn, w_in[:, :6 * width].astype(BF16), cos, sin_signed,
                              rope_tiles=(0, 1), scale_tiles=(0, 3), tm=tm, tn=width)
            wf_t = w_in[:, 6 * width:].T.astype(BF16)
            c = _fox_gate(h, g_attn, wf_t, hyb_b_f[e][:, None], tc=min(512, s))
            pats = [_dilated_pattern(proj, dil, width=width, tq=min(512, s // dil))
                    for _, dil in DILATED_PATTERNS]
            ob = _flash_pair(proj, (c.reshape(n_pairs, 2, s),), mode="fox", n_groups=n_pairs,
                             q_col=3 * n_pairs, k_col=4 * n_pairs, v_col=5 * n_pairs,
                             tq=tq_flash, tk=tk_flash)
            h = _hyb_out(pats, ob, hyb_w_out[e].astype(BF16), h, tm=min(512, s))
        else:
            o = l // 2
            n_q = 2 * n_diff_heads * HEAD_DIM // 512
            proj = _norm_proj(h, g_attn, diff_w_qkv[o].astype(BF16), cos, sin_signed,
                              rope_tiles=tuple(range(2 * n_q)), scale_tiles=tuple(range(n_q)),
                              tm=tm, tn=512)
            lam_init = 0.8 - 0.6 * math.exp(-0.3 * l)
            att = _flash_pair(proj, (diff_lambda[o], diff_subln[o][None, :]), mode="diff",
                              n_groups=n_diff_heads, q_col=0, k_col=n_diff_heads,
                              v_col=2 * n_diff_heads, tq=tq_flash, tk=tk_flash,
                              lam_init=lam_init)
            h = _proj_res(att, diff_w_out[o].astype(BF16), h, tm=min(512, s))
        h = _ffn(h, ffn_norm[l][None, :], ffn_w_up[l].astype(BF16), ffn_conv_w[l],
                 ffn_conv_b[l][None, :], ffn_w_down[l].astype(BF16),
                 final_norm[None, :] if l == depth - 1 else None, tm=tm, tf=256)
    return h[None]
```

```python
import functools
import math

import jax
import jax.numpy as jnp
from jax import lax
from jax.experimental import pallas as pl
from jax.experimental.pallas import tpu as pltpu

F32 = jnp.float32
BF16 = jnp.bfloat16

HEAD_DIM = 64
LANES = 128
BF16_SUBLANES = 16
ROPE_THETA = 10000.0
DILATED_PATTERNS = ((128, 1), (512, 4), (2048, 16))
BAND = 128
NORM_EPS = 1e-6
SUBLN_EPS = 1e-5
CONV_WIDTH = 3
NEG = -1e30
VMEM_LIMIT = 48 * 1024 * 1024

NT_DIMS = (((1,), (1,)), ((), ()))


def _params(sem):
    return pltpu.CompilerParams(dimension_semantics=sem, vmem_limit_bytes=VMEM_LIMIT)


def _rms(x, g, eps):
    return x * lax.rsqrt(jnp.mean(x * x, axis=-1, keepdims=True) + eps) * g


def _tile_in(j, tiles):
    pred = j < 0
    for t in tiles:
        pred = jnp.logical_or(pred, j == t)
    return pred


def _lane_lo(rows):
    return lax.broadcasted_iota(jnp.int32, (rows, LANES), 1) < HEAD_DIM


def _rope_tile(x, cos, sin_signed):
    rows = x.shape[0]
    first_half = (lax.broadcasted_iota(jnp.int32, (rows, LANES), 1) & 32) == 0
    out = []
    for c in range(x.shape[1] // LANES):
        xc = x[:, c * LANES:(c + 1) * LANES]
        ahead = pltpu.roll(xc, LANES - 32, 1)
        behind = pltpu.roll(xc, 32, 1)
        rot = jnp.where(first_half, ahead, behind)
        out.append(xc * cos + rot * sin_signed)
    return jnp.concatenate(out, axis=1)


def _norm_proj_kernel(h_ref, g_ref, w_ref, cos_ref, sin_ref, o_ref, n_scr, *,
                      rope_tiles, scale_tiles, scale):
    j = pl.program_id(1)

    @pl.when(j == 0)
    def _():
        n_scr[...] = _rms(h_ref[...], g_ref[...], NORM_EPS).astype(BF16)

    acc = jnp.dot(n_scr[...], w_ref[...], preferred_element_type=F32)
    sc = jnp.where(_tile_in(j, scale_tiles), scale, 1.0).astype(F32)
    is_rope = _tile_in(j, rope_tiles)

    @pl.when(is_rope)
    def _():
        o_ref[...] = (_rope_tile(acc, cos_ref[...], sin_ref[...]) * sc).astype(BF16)

    @pl.when(jnp.logical_not(is_rope))
    def _():
        o_ref[...] = (acc * sc).astype(BF16)


def _norm_proj(h, g, w, cos, sin_signed, *, rope_tiles, scale_tiles, scale, tm, tn):
    s, d = h.shape
    n = w.shape[1]
    kern = functools.partial(_norm_proj_kernel, rope_tiles=rope_tiles,
                             scale_tiles=scale_tiles, scale=scale)
    return pl.pallas_call(
        kern,
        grid=(s // tm, n // tn),
        in_specs=[
            pl.BlockSpec((tm, d), lambda i, j: (i, 0)),
            pl.BlockSpec((1, d), lambda i, j: (0, 0)),
            pl.BlockSpec((d, tn), lambda i, j: (0, j)),
            pl.BlockSpec((tm, LANES), lambda i, j: (i, 0)),
            pl.BlockSpec((tm, LANES), lambda i, j: (i, 0)),
        ],
        out_specs=pl.BlockSpec((tm, tn), lambda i, j: (i, j)),
        out_shape=jax.ShapeDtypeStruct((s, n), BF16),
        scratch_shapes=[pltpu.VMEM((tm, d), BF16)],
        compiler_params=_params(("parallel", "arbitrary")),
        name="norm_proj",
    )(h, g, w, cos, sin_signed)


def _norm_proj_t_kernel(h_ref, g_ref, wt_ref, cos_ref, sin_ref, o_ref, n_scr, *,
                        rope_tiles, scale_tiles, scale):
    j = pl.program_id(1)

    @pl.when(j == 0)
    def _():
        n_scr[...] = _rms(h_ref[...], g_ref[...], NORM_EPS).astype(BF16)

    acc = lax.dot_general(wt_ref[...], n_scr[...], NT_DIMS, preferred_element_type=F32)
    sc = jnp.where(_tile_in(j, scale_tiles), scale, 1.0).astype(F32)
    half = HEAD_DIM // 2

    def plain():
        o_ref[...] = (acc * sc).astype(BF16)

    def roped():
        cos, sin = cos_ref[...], sin_ref[...]
        for hd in range(acc.shape[0] // HEAD_DIM):
            x1 = acc[hd * HEAD_DIM:hd * HEAD_DIM + half]
            x2 = acc[hd * HEAD_DIM + half:(hd + 1) * HEAD_DIM]
            o_ref[hd * HEAD_DIM:hd * HEAD_DIM + half] = ((x1 * cos - x2 * sin) * sc).astype(BF16)
            o_ref[hd * HEAD_DIM + half:(hd + 1) * HEAD_DIM] = ((x2 * cos + x1 * sin) * sc).astype(BF16)

    if rope_tiles:
        is_rope = _tile_in(j, rope_tiles)
        pl.when(is_rope)(roped)
        pl.when(jnp.logical_not(is_rope))(plain)
    else:
        plain()


def _norm_proj_t(h, g, wt, cos_t, sin_t, *, rope_tiles, scale_tiles, scale, tm, tn):
    s, d = h.shape
    n = wt.shape[0]
    kern = functools.partial(_norm_proj_t_kernel, rope_tiles=rope_tiles,
                             scale_tiles=scale_tiles, scale=scale)
    return pl.pallas_call(
        kern,
        grid=(s // tm, n // tn),
        in_specs=[
            pl.BlockSpec((tm, d), lambda i, j: (i, 0)),
            pl.BlockSpec((1, d), lambda i, j: (0, 0)),
            pl.BlockSpec((tn, d), lambda i, j: (j, 0)),
            pl.BlockSpec((HEAD_DIM // 2, tm), lambda i, j: (0, i)),
            pl.BlockSpec((HEAD_DIM // 2, tm), lambda i, j: (0, i)),
        ],
        out_specs=pl.BlockSpec((tn, tm), lambda i, j: (j, i)),
        out_shape=jax.ShapeDtypeStruct((n, s), BF16),
        scratch_shapes=[pltpu.VMEM((tm, d), BF16)],
        compiler_params=_params(("parallel", "arbitrary")),
        name="norm_proj_t",
    )(h, g, wt, cos_t, sin_t)


GATE_TERMS = 3
GATE_STRIDE = 8
LOG2E = math.log2(math.e)


def _fox_gate_kernel(h_ref, g_ref, wf_ref, bf_ref, cf_ref, carry, *, tc, n_heads):
    i = pl.program_id(0)

    @pl.when(i == 0)
    def _():
        carry[...] = jnp.zeros_like(carry)

    n = _rms(h_ref[...], g_ref[...], NORM_EPS).astype(BF16)
    z = jnp.dot(n, wf_ref[...], preferred_element_type=F32) + bf_ref[...]
    logf = jnp.minimum(z, 0.0) - jnp.log(1.0 + jnp.exp(-jnp.abs(z)))
    dst = lax.broadcasted_iota(jnp.int32, (tc, tc), 0)
    src = lax.broadcasted_iota(jnp.int32, (tc, tc), 1)
    prefix = jnp.where(src <= dst, 1.0, 0.0).astype(F32)
    cs = jnp.dot(prefix, logf, precision=lax.Precision.HIGHEST,
                 preferred_element_type=F32) + carry[0:1, :]
    carry[...] = jnp.broadcast_to(cs[tc - 1:tc, :], carry.shape)
    lane = lax.broadcasted_iota(jnp.int32, (tc, LANES), 1)
    rem = jnp.where(lane < n_heads, cs * LOG2E, 0.0)
    feat = jnp.zeros_like(rem)
    for term in range(GATE_TERMS):
        part = rem.astype(BF16).astype(F32)
        rem = rem - part
        feat = feat + (pltpu.roll(part, term * GATE_STRIDE, 1) if term else part)
    cf_ref[...] = feat.astype(BF16)


def _fox_gate(h, g, wf, b_f, *, tc, n_heads):
    s, d = h.shape
    return pl.pallas_call(
        functools.partial(_fox_gate_kernel, tc=tc, n_heads=n_heads),
        grid=(s // tc,),
        in_specs=[
            pl.BlockSpec((tc, d), lambda i: (i, 0)),
            pl.BlockSpec((1, d), lambda i: (0, 0)),
            pl.BlockSpec((d, LANES), lambda i: (0, 0)),
            pl.BlockSpec((1, LANES), lambda i: (0, 0)),
        ],
        out_specs=pl.BlockSpec((tc, LANES), lambda i: (i, 0)),
        out_shape=jax.ShapeDtypeStruct((s, LANES), BF16),
        scratch_shapes=[pltpu.VMEM((8, LANES), F32)],
        compiler_params=_params(("arbitrary",)),
        name="fox_gate",
    )(h, g, wf, b_f)


def _dilated_kernel(q_ref, kc_ref, kh_ref, vc_ref, vh_ref, o_ref, l_ref, kbuf, vbuf, *, tq):
    i = pl.program_id(1)
    kbuf[0:BAND] = kh_ref[...]
    kbuf[BAND:] = kc_ref[...]
    vbuf[0:BAND] = vh_ref[...]
    vbuf[BAND:] = vc_ref[...]
    row = lax.broadcasted_iota(jnp.int32, (BAND, 2 * BAND), 0)
    col = lax.broadcasted_iota(jnp.int32, (BAND, 2 * BAND), 1)
    delta = row - col + BAND
    band = (delta >= 0) & (delta <= BAND)
    lo = _lane_lo(BAND)
    for a in range(tq // BAND):
        first_key = i * tq + (a - 1) * BAND
        valid = band & (col + first_key >= 0)
        rows = slice(a * BAND, (a + 1) * BAND)
        for hp in range(q_ref.shape[1] // LANES):
            lanes = slice(hp * LANES, (hp + 1) * LANES)
            q = q_ref[rows, lanes]
            kk = kbuf[a * BAND:(a + 2) * BAND, lanes]
            vv = vbuf[a * BAND:(a + 2) * BAND, lanes]
            outs, lses = [], []
            for qh in (jnp.where(lo, q, jnp.zeros_like(q)), jnp.where(lo, jnp.zeros_like(q), q)):
                s = lax.dot_general(qh, kk, NT_DIMS, preferred_element_type=F32)
                s = jnp.where(valid, s, NEG)
                m = jnp.max(s, axis=1, keepdims=True)
                p = jnp.exp(s - m)
                den = jnp.sum(p, axis=1, keepdims=True)
                pv = jnp.dot(p.astype(BF16), vv, preferred_element_type=F32)
                outs.append(pv / den)
                lses.append(jnp.broadcast_to(m + jnp.log(den), (BAND, LANES)))
            o_ref[rows, lanes] = jnp.where(lo, outs[0], outs[1])
            l_ref[rows, lanes] = jnp.where(lo, lses[0], lses[1])


def _dilated_pattern(proj, dil, *, width, tq):
    s, n = proj.shape
    L = s // dil
    view = proj.reshape(L, dil * n)
    per_res = n // width
    halo_per_tile = tq // BAND
    cur = lambda part: pl.BlockSpec((tq, width), lambda r, i: (i, r * per_res + part))
    halo = lambda part: pl.BlockSpec(
        (BAND, width), lambda r, i: (jnp.maximum(i * halo_per_tile - 1, 0), r * per_res + part))
    out_spec = pl.BlockSpec((tq, width), lambda r, i: (i, r))
    o, lse = pl.pallas_call(
        functools.partial(_dilated_kernel, tq=tq),
        grid=(dil, L // tq),
        in_specs=[cur(0), cur(1), halo(1), cur(2), halo(2)],
        out_specs=[out_spec, out_spec],
        out_shape=[jax.ShapeDtypeStruct((L, dil * width), F32)] * 2,
        scratch_shapes=[pltpu.VMEM((tq + BAND, width), BF16)] * 2,
        compiler_params=_params(("parallel", "arbitrary")),
        name=f"dilated_d{dil}",
    )(view, view, view, view, view)
    return o.reshape(s, width), lse.reshape(s, width)


ONES_ROWS = BF16_SUBLANES


COL_TILE = 256
STRIPS_PER_TRIP = 2


def _flash_pair_kernel(qi_ref, kj_ref, qt_ref, k_ref, *rest, mode, tq, tk, gps, lam_init):
    if mode == "fox":
        cf_ref, vt_ref, o_ref, wq, m_scr, acc, kx_scr, vx_scr, *bufs = rest
    else:
        vt_ref, lam_ref, g_ref, o_ref, wq, m_scr, acc, kx_scr, vx_scr, *bufs = rest
    s_buf, mx_buf, p_buf, al_buf = bufs[0:2], bufs[2:4], bufs[4:6], bufs[6:8]
    U = STRIPS_PER_TRIP
    step = pl.program_id(1)
    qi = qi_ref[step]
    kj = kj_ref[step]
    last_kj = (qi * tq + tq - 1) // tk
    rows = acc.shape[1]
    dv = rows - ONES_ROWS
    n_ct = tq // COL_TILE
    n_strips = 2 * gps * n_ct
    n_trips = n_strips // U
    par = kj % 2

    @pl.when(jnp.logical_and(pl.program_id(0) == 0, step == 0))
    def _():
        for buf in bufs:
            buf[...] = jnp.zeros_like(buf)
        vx_scr[...] = jnp.zeros_like(vx_scr)

    @pl.when(kj == 0)
    def _():
        r = lax.broadcasted_iota(jnp.int32, (LANES, COL_TILE), 0)
        first = r < HEAD_DIM
        for gi in range(gps):
            for a in range(2):
                keep_q = first if a == 0 else jnp.logical_not(first)
                if mode == "fox":
                    head = 2 * (pl.program_id(0) * gps + gi) + a
                    pick = (r - head) == 0
                    for term in range(1, GATE_TERMS):
                        pick = pick | ((r - head) == term * GATE_STRIDE)
                    gate_rows = jnp.where(pick, -1.0, 0.0).astype(BF16)
                for ct in range(n_ct):
                    t = (2 * gi + a) * n_ct + ct
                    qt = qt_ref[gi * LANES:(gi + 1) * LANES, ct * COL_TILE:(ct + 1) * COL_TILE]
                    wq[t, 0:LANES] = jnp.where(keep_q, qt, jnp.zeros_like(qt))
                    if mode == "fox":
                        wq[t, LANES:2 * LANES] = gate_rows
        m_scr[...] = jnp.full_like(m_scr, NEG)
        acc[...] = jnp.zeros_like(acc)

    ones = jnp.ones((ONES_ROWS, tk), BF16)
    for gi in range(gps):
        kx_scr[gi, :, 0:LANES] = k_ref[:, gi * LANES:(gi + 1) * LANES]
        if mode == "fox":
            kx_scr[gi, :, LANES:2 * LANES] = cf_ref[...]
        for a in range(2):
            if mode == "fox":
                v0 = gi * LANES + a * HEAD_DIM
                vx_scr[par, 2 * gi + a, 0:dv] = vt_ref[v0:v0 + HEAD_DIM, :]
            else:
                vx_scr[par, 2 * gi + a, 0:dv] = vt_ref[gi * LANES:(gi + 1) * LANES, :]
            vx_scr[par, 2 * gi + a, dv:rows] = ones

    def stage_a(j, slot, u, masked):
        s = jnp.dot(kx_scr[j // (2 * n_ct)], wq[j], preferred_element_type=F32)
        if masked:
            key = kj * tk + lax.broadcasted_iota(jnp.int32, (tk, COL_TILE), 0)
            qry = (qi * tq + (j % n_ct) * COL_TILE
                   + lax.broadcasted_iota(jnp.int32, (tk, COL_TILE), 1))
            s = jnp.where(key <= qry, s, NEG)
        s_buf[slot][u] = s
        mx_buf[slot][u] = jnp.broadcast_to(jnp.max(s, axis=0, keepdims=True), (8, COL_TILE))


    def stage_b(j0, slot, valid):
        m_prevs = [m_scr[j0 + u] for u in range(U)]
        for u, m_prev in enumerate(m_prevs):
            m_next = jnp.maximum(m_prev, mx_buf[slot][u])
            if valid is not None:
                m_next = jnp.where(valid, m_next, m_prev)
            al_buf[slot][u] = jnp.exp2(m_prev - m_next)
            s3 = s_buf[slot][u].reshape(tk // 8, 8, COL_TILE)
            p_buf[slot][u] = jnp.exp2(s3 - m_next[None]).reshape(tk, COL_TILE).astype(BF16)
            m_prevs[u] = m_next
        for u, m_next in enumerate(m_prevs):
            m_scr[j0 + u] = m_next

    def stage_c(j0, slot, vpar, valid):
        pvs = [jnp.dot(vx_scr[vpar, (j0 + u) // n_ct], p_buf[slot][u],
                       preferred_element_type=F32) for u in range(U)]
        olds = [acc[j0 + u] for u in range(U)]
        for u, (old, pv) in enumerate(zip(olds, pvs)):
            new = ((old.reshape(rows // 8, 8, COL_TILE) * al_buf[slot][u][None])
                   .reshape(rows, COL_TILE) + pv)
            acc[j0 + u] = new if valid is None else jnp.where(valid, new, old)

    def sweep(masked):
        def trip(k, slot):
            later_step = kj > 0
            kc = (k + n_trips - 2) % n_trips
            kb = (k + n_trips - 1) % n_trips
            valid_c = jnp.logical_or(later_step, k >= 2)
            valid_b = jnp.logical_or(later_step, k >= 1)
            vpar = jnp.where(k >= 2, par, 1 - par)
            stage_c(kc * U, slot, vpar, valid_c)
            stage_b(kb * U, 1 - slot, valid_b)
            for u in range(U):
                stage_a(k * U + u, slot, u, masked)

        def trip_pair(half, carry):
            trip(2 * half, 0)
            trip(2 * half + 1, 1)
            return carry

        lax.fori_loop(0, n_trips // 2, trip_pair, 0)

    needs_mask = kj * tk + tk - 1 > qi * tq
    pl.when(needs_mask)(lambda: sweep(True))
    pl.when(jnp.logical_not(needs_mask))(lambda: sweep(False))

    @pl.when(kj == last_kj)
    def _():
        last = n_trips - 1
        stage_c((last - 1) * U, (last - 1) % 2, par, None)
        stage_b(last * U, last % 2, None)
        stage_c(last * U, last % 2, par, None)

        def map_out(mi):
            parts = [acc[mi * n_ct + ct] for ct in range(n_ct)]
            full = jnp.concatenate(parts, axis=1)
            return full[0:dv] / full[dv:dv + 1]

        for gi in range(gps):
            o0, o1 = map_out(2 * gi), map_out(2 * gi + 1)
            if mode == "fox":
                o = jnp.concatenate([o0, o1], axis=0)
            else:
                lp = lam_ref[...]
                t1 = jnp.sum(lp[0:1] * lp[1:2], axis=1, keepdims=True)
                t2 = jnp.sum(lp[2:3] * lp[3:4], axis=1, keepdims=True)
                lam = jnp.exp(t1) - jnp.exp(t2) + lam_init
                o = o0 - lam * o1
                ms = jnp.mean(o * o, axis=0, keepdims=True)
                o = o * lax.rsqrt(ms + SUBLN_EPS) * g_ref[...] * (1.0 - lam_init)
            o_ref[:, gi * LANES:(gi + 1) * LANES] = o.T.astype(BF16)


def _causal_steps(s, tq, tk):
    qi, kj = [], []
    for i in range(s // tq):
        for j in range((i * tq + tq - 1) // tk + 1):
            qi.append(i)
            kj.append(j)
    return jnp.asarray(qi, jnp.int32), jnp.asarray(kj, jnp.int32)


def _flash_pair(proj_t, keys, extras, *, mode, n_groups, gps, q_row, v_row, k_col, tq, tk,
                lam_init=0.0):
    s = keys.shape[0]
    qi, kj = _causal_steps(s, tq, tk)
    gw = gps * LANES
    qt_spec = pl.BlockSpec((gw, tq), lambda g, t, qi, kj: (q_row + g, qi[t]))
    k_spec = pl.BlockSpec((tk, gw), lambda g, t, qi, kj: (kj[t], k_col + g))
    vt_spec = pl.BlockSpec((gw, tk), lambda g, t, qi, kj: (v_row + g, kj[t]))
    const = lambda x: pl.BlockSpec(x.shape, lambda g, t, qi, kj: (0, 0))
    if mode == "fox":
        (cfeat,) = extras
        in_specs = [qt_spec, k_spec,
                    pl.BlockSpec((tk, LANES), lambda g, t, qi, kj: (kj[t], 0)), vt_spec]
        args = (proj_t, keys, cfeat, proj_t)
        kd, dv = 2 * LANES, HEAD_DIM
    else:
        lam_params, subln_g = extras
        in_specs = [qt_spec, k_spec, vt_spec, const(lam_params), const(subln_g)]
        args = (proj_t, keys, proj_t, lam_params, subln_g)
        kd, dv = LANES, LANES
    rows = dv + ONES_ROWS
    n_strips = 2 * gps * (tq // COL_TILE)
    u = STRIPS_PER_TRIP
    assert n_strips % (2 * u) == 0
    assert (tq - 1) // tk >= 1
    kern = functools.partial(_flash_pair_kernel, mode=mode, tq=tq, tk=tk, gps=gps,
                             lam_init=lam_init)
    return pl.pallas_call(
        kern,
        grid_spec=pltpu.PrefetchScalarGridSpec(
            num_scalar_prefetch=2,
            grid=(n_groups // gps, qi.shape[0]),
            in_specs=in_specs,
            out_specs=pl.BlockSpec((tq, gw), lambda g, t, qi, kj: (qi[t], g)),
            scratch_shapes=[
                pltpu.VMEM((n_strips, kd, COL_TILE), BF16),
                pltpu.VMEM((n_strips, 8, COL_TILE), F32),
                pltpu.VMEM((n_strips, rows, COL_TILE), F32),
                pltpu.VMEM((gps, tk, kd), BF16),
                pltpu.VMEM((2, 2 * gps, rows, tk), BF16),
            ] + [pltpu.VMEM((u, tk, COL_TILE), F32)] * 2
              + [pltpu.VMEM((u, 8, COL_TILE), F32)] * 2
              + [pltpu.VMEM((u, tk, COL_TILE), BF16)] * 2
              + [pltpu.VMEM((u, 8, COL_TILE), F32)] * 2,
        ),
        out_shape=jax.ShapeDtypeStruct((s, n_groups * LANES), BF16),
        compiler_params=_params(("arbitrary", "arbitrary")),
        name=f"flash_{mode}",
    )(qi, kj, *args)


def _hyb_out_kernel(o1, o2, o3, l1, l2, l3, ob_ref, w_ref, h_ref, out_ref):
    a1, a2, a3 = l1[...], l2[...], l3[...]
    m = jnp.maximum(jnp.maximum(a1, a2), a3)
    e1, e2, e3 = jnp.exp(a1 - m), jnp.exp(a2 - m), jnp.exp(a3 - m)
    oa = (e1 * o1[...] + e2 * o2[...] + e3 * o3[...]) / (e1 + e2 + e3)
    wa = oa.shape[1]
    acc = jnp.dot(oa.astype(BF16), w_ref[0:wa, :], preferred_element_type=F32)
    acc = acc + jnp.dot(ob_ref[...], w_ref[wa:, :], preferred_element_type=F32)
    out_ref[...] = h_ref[...] + acc


def _hyb_out(pattern_outs, ob, w, h, *, tm):
    s, d = h.shape
    wa = pattern_outs[0][0].shape[1]
    part = pl.BlockSpec((tm, wa), lambda i: (i, 0))
    os_ = [o for o, _ in pattern_outs]
    ls_ = [l for _, l in pattern_outs]
    return pl.pallas_call(
        _hyb_out_kernel,
        grid=(s // tm,),
        in_specs=[part] * 6 + [
            pl.BlockSpec((tm, ob.shape[1]), lambda i: (i, 0)),
            pl.BlockSpec(w.shape, lambda i: (0, 0)),
            pl.BlockSpec((tm, d), lambda i: (i, 0)),
        ],
        out_specs=pl.BlockSpec((tm, d), lambda i: (i, 0)),
        out_shape=jax.ShapeDtypeStruct((s, d), F32),
        compiler_params=_params(("parallel",)),
        name="hyb_out",
    )(*os_, *ls_, ob, w, h)


def _proj_res_kernel(a_ref, w_ref, h_ref, out_ref):
    out_ref[...] = h_ref[...] + jnp.dot(a_ref[...], w_ref[...], preferred_element_type=F32)


def _proj_res(a, w, h, *, tm):
    s, d = h.shape
    return pl.pallas_call(
        _proj_res_kernel,
        grid=(s // tm,),
        in_specs=[
            pl.BlockSpec((tm, a.shape[1]), lambda i: (i, 0)),
            pl.BlockSpec(w.shape, lambda i: (0, 0)),
            pl.BlockSpec((tm, d), lambda i: (i, 0)),
        ],
        out_specs=pl.BlockSpec((tm, d), lambda i: (i, 0)),
        out_shape=jax.ShapeDtypeStruct((s, d), F32),
        compiler_params=_params(("parallel",)),
        name="proj_res",
    )(a, w, h)


HALO = BF16_SUBLANES


def _ffn_kernel(h_ref, halo_ref, g_ref, wg_ref, wu_ref, cw_ref, cb_ref, wd_ref, *rest,
                tm, final):
    if final:
        fg_ref, out_ref, n_scr, gate_scr, acc = rest
    else:
        out_ref, n_scr, gate_scr, acc = rest
    i = pl.program_id(0)
    j = pl.program_id(1)

    @pl.when(j == 0)
    def _():
        g = g_ref[...]
        prev = jnp.where(i > 0, halo_ref[...], 0.0)
        n_scr[0:HALO] = _rms(prev, g, NORM_EPS).astype(BF16)
        n_scr[HALO:] = _rms(h_ref[...], g, NORM_EPS).astype(BF16)
        acc[...] = jnp.zeros_like(acc)

    gate_scr[...] = jnp.dot(n_scr[...], wg_ref[...], preferred_element_type=F32)
    up = jnp.dot(n_scr[HALO:], wu_ref[...], preferred_element_type=F32)
    conv = cb_ref[...]
    for t in range(CONV_WIDTH):
        start = HALO - (CONV_WIDTH - 1) + t
        conv = conv + gate_scr[start:start + tm] * cw_ref[t:t + 1, :]
    act = conv * (1.0 / (1.0 + jnp.exp(-conv))) * up
    acc[...] += jnp.dot(act.astype(BF16), wd_ref[...], preferred_element_type=F32)

    @pl.when(j == pl.num_programs(1) - 1)
    def _():
        y = h_ref[...] + acc[...]
        if final:
            y = _rms(y, fg_ref[...], NORM_EPS)
        out_ref[...] = y


def _ffn(h, g, w_up, conv_w, conv_b, w_down, final_g, *, tm, tf):
    s, d = h.shape
    d_ff = w_down.shape[0]
    nf = d_ff // tf
    final = final_g is not None
    in_specs = [
        pl.BlockSpec((tm, d), lambda i, j: (i, 0)),
        pl.BlockSpec((HALO, d), lambda i, j: (jnp.maximum(i * (tm // HALO) - 1, 0), 0)),
        pl.BlockSpec((1, d), lambda i, j: (0, 0)),
        pl.BlockSpec((d, tf), lambda i, j: (0, j)),
        pl.BlockSpec((d, tf), lambda i, j: (0, nf + j)),
        pl.BlockSpec((CONV_WIDTH, tf), lambda i, j: (0, j)),
        pl.BlockSpec((1, tf), lambda i, j: (0, j)),
        pl.BlockSpec((tf, d), lambda i, j: (j, 0)),
    ]
    args = [h, h, g, w_up, w_up, conv_w, conv_b, w_down]
    if final:
        in_specs.append(pl.BlockSpec((1, d), lambda i, j: (0, 0)))
        args.append(final_g)
    return pl.pallas_call(
        functools.partial(_ffn_kernel, tm=tm, final=final),
        grid=(s // tm, nf),
        in_specs=in_specs,
        out_specs=pl.BlockSpec((tm, d), lambda i, j: (i, 0)),
        out_shape=jax.ShapeDtypeStruct((s, d), F32),
        scratch_shapes=[
            pltpu.VMEM((tm + HALO, d), BF16),
            pltpu.VMEM((tm + HALO, tf), F32),
            pltpu.VMEM((tm, d), F32),
        ],
        compiler_params=_params(("parallel", "arbitrary")),
        name="ffn_final" if final else "ffn",
    )(*args)


def _rope_tables(s):
    inv = 1.0 / (ROPE_THETA ** (jnp.arange(0, HEAD_DIM, 2, dtype=F32) / HEAD_DIM))
    ang = jnp.arange(s, dtype=F32)[:, None] * inv[None, :]
    cos, sin = jnp.cos(ang), jnp.sin(ang)
    sign = jnp.where((jnp.arange(LANES) & 32) == 0, -1.0, 1.0).astype(F32)
    cos_l = jnp.tile(cos, (1, LANES // 32))
    sin_signed = jnp.tile(sin, (1, LANES // 32)) * sign[None, :]
    return cos_l, sin_signed, cos.T, sin.T


def kernel(x, attn_norm, ffn_norm, final_norm, hyb_w_in, hyb_b_f, hyb_w_out, diff_w_qkv,
           diff_lambda, diff_subln, diff_w_out, ffn_w_up, ffn_conv_w, ffn_conv_b, ffn_w_down):
    b, s, d = x.shape
    assert b == 1
    depth = attn_norm.shape[0]
    width = hyb_w_out.shape[1] // 2
    n_pairs = width // LANES
    n_heads_b = width // HEAD_DIM
    n_diff_heads = diff_w_out.shape[1] // LANES
    dq = n_diff_heads * LANES
    tm = min(1024, s)
    tq_flash, tk_flash = min(1024, s), min(512, s)
    gps = 2
    q_scale = HEAD_DIM ** -0.5
    cos, sin_signed, cos_t, sin_t = _rope_tables(s)
    h = x[0]

    for l in range(depth):
        g_attn = attn_norm[l][None, :]
        if l % 2 == 0:
            e = l // 2
            w_in = hyb_w_in[e]
            qa_ka_va, qb, kb, vb, wf = (w_in[:, :3 * width], w_in[:, 3 * width:4 * width],
                                        w_in[:, 4 * width:5 * width], w_in[:, 5 * width:6 * width],
                                        w_in[:, 6 * width:])
            proj = _norm_proj(h, g_attn, jnp.concatenate([qa_ka_va, kb], axis=1).astype(BF16),
                              cos, sin_signed, rope_tiles=(0, 1), scale_tiles=(0,),
                              scale=q_scale, tm=tm, tn=width)
            proj_t = _norm_proj_t(h, g_attn, jnp.concatenate([qb, vb], axis=1).T.astype(BF16),
                                  cos_t, sin_t, rope_tiles=(), scale_tiles=(0,),
                                  scale=q_scale * LOG2E, tm=tm, tn=width)
            wf_pad = jnp.pad(wf, ((0, 0), (0, LANES - n_heads_b))).astype(BF16)
            bf_pad = jnp.pad(hyb_b_f[e], (0, LANES - n_heads_b))[None, :]
            cfeat = _fox_gate(h, g_attn, wf_pad, bf_pad, tc=min(512, s), n_heads=n_heads_b)
            pats = [_dilated_pattern(proj, dil, width=width, tq=min(512, s // dil))
                    for _, dil in DILATED_PATTERNS]
            ob = _flash_pair(proj_t, proj, (cfeat,), mode="fox", n_groups=n_pairs, gps=gps,
                             q_row=0, v_row=n_pairs // gps, k_col=3 * n_pairs // gps,
                             tq=tq_flash, tk=tk_flash)
            h = _hyb_out(pats, ob, hyb_w_out[e].astype(BF16), h, tm=min(512, s))
        else:
            o = l // 2
            w = diff_w_qkv[o]
            wq, wk, wv = w[:, :dq], w[:, dq:2 * dq], w[:, 2 * dq:]
            k_tiles = dq // 512
            keys = _norm_proj(h, g_attn, wk.astype(BF16), cos, sin_signed,
                              rope_tiles=tuple(range(k_tiles)), scale_tiles=(),
                              scale=1.0, tm=tm, tn=512)
            proj_t = _norm_proj_t(h, g_attn, jnp.concatenate([wq, wv], axis=1).T.astype(BF16),
                                  cos_t, sin_t, rope_tiles=tuple(range(k_tiles)),
                                  scale_tiles=tuple(range(k_tiles)), scale=q_scale * LOG2E,
                                  tm=tm, tn=512)
            lam_init = 0.8 - 0.6 * math.exp(-0.3 * l)
            att = _flash_pair(proj_t, keys, (diff_lambda[o], diff_subln[o][:, None]),
                              mode="diff", n_groups=n_diff_heads, gps=gps, q_row=0,
                              v_row=n_diff_heads // gps, k_col=0, tq=tq_flash, tk=tk_flash,
                              lam_init=lam_init)
            h = _proj_res(att, diff_w_out[o].astype(BF16), h, tm=min(512, s))
        h = _ffn(h, ffn_norm[l][None, :], ffn_w_up[l].astype(BF16), ffn_conv_w[l],
                 ffn_conv_b[l][None, :], ffn_w_down[l].astype(BF16),
                 final_norm[None, :] if l == depth - 1 else None, tm=tm, tf=256)
    return h[None]
```

```python
import functools
import math

import jax
import jax.numpy as jnp
from jax import lax
from jax.experimental import pallas as pl
from jax.experimental.pallas import tpu as pltpu

F32 = jnp.float32
BF16 = jnp.bfloat16

HEAD_DIM = 64
LANES = 128
BF16_SUBLANES = 16
ROPE_THETA = 10000.0
DILATED_PATTERNS = ((128, 1), (512, 4), (2048, 16))
BAND = 128
NORM_EPS = 1e-6
SUBLN_EPS = 1e-5
CONV_WIDTH = 3
NEG = -1e30
VMEM_LIMIT = 48 * 1024 * 1024

NT_DIMS = (((1,), (1,)), ((), ()))


def _params(sem):
    return pltpu.CompilerParams(dimension_semantics=sem, vmem_limit_bytes=VMEM_LIMIT)


def _rms(x, g, eps):
    return x * lax.rsqrt(jnp.mean(x * x, axis=-1, keepdims=True) + eps) * g


def _tile_in(j, tiles):
    pred = j < 0
    for t in tiles:
        pred = jnp.logical_or(pred, j == t)
    return pred


def _lane_lo(rows):
    return lax.broadcasted_iota(jnp.int32, (rows, LANES), 1) < HEAD_DIM


def _rope_tile(x, cos, sin_signed):
    rows = x.shape[0]
    first_half = (lax.broadcasted_iota(jnp.int32, (rows, LANES), 1) & 32) == 0
    out = []
    for c in range(x.shape[1] // LANES):
        xc = x[:, c * LANES:(c + 1) * LANES]
        ahead = pltpu.roll(xc, LANES - 32, 1)
        behind = pltpu.roll(xc, 32, 1)
        rot = jnp.where(first_half, ahead, behind)
        out.append(xc * cos + rot * sin_signed)
    return jnp.concatenate(out, axis=1)


def _norm_proj_kernel(h_ref, g_ref, w_ref, cos_ref, sin_ref, o_ref, n_scr, *,
                      rope_tiles, scale_tiles, scale):
    j = pl.program_id(1)

    @pl.when(j == 0)
    def _():
        n_scr[...] = _rms(h_ref[...], g_ref[...], NORM_EPS).astype(BF16)

    acc = jnp.dot(n_scr[...], w_ref[...], preferred_element_type=F32)
    sc = jnp.where(_tile_in(j, scale_tiles), scale, 1.0).astype(F32)
    is_rope = _tile_in(j, rope_tiles)

    @pl.when(is_rope)
    def _():
        o_ref[...] = (_rope_tile(acc, cos_ref[...], sin_ref[...]) * sc).astype(BF16)

    @pl.when(jnp.logical_not(is_rope))
    def _():
        o_ref[...] = (acc * sc).astype(BF16)


def _norm_proj(h, g, w, cos, sin_signed, *, rope_tiles, scale_tiles, scale, tm, tn):
    s, d = h.shape
    n = w.shape[1]
    kern = functools.partial(_norm_proj_kernel, rope_tiles=rope_tiles,
                             scale_tiles=scale_tiles, scale=scale)
    return pl.pallas_call(
        kern,
        grid=(s // tm, n // tn),
        in_specs=[
            pl.BlockSpec((tm, d), lambda i, j: (i, 0)),
            pl.BlockSpec((1, d), lambda i, j: (0, 0)),
            pl.BlockSpec((d, tn), lambda i, j: (0, j)),
            pl.BlockSpec((tm, LANES), lambda i, j: (i, 0)),
            pl.BlockSpec((tm, LANES), lambda i, j: (i, 0)),
        ],
        out_specs=pl.BlockSpec((tm, tn), lambda i, j: (i, j)),
        out_shape=jax.ShapeDtypeStruct((s, n), BF16),
        scratch_shapes=[pltpu.VMEM((tm, d), BF16)],
        compiler_params=_params(("parallel", "arbitrary")),
        name="norm_proj",
    )(h, g, w, cos, sin_signed)


def _norm_proj_t_kernel(h_ref, g_ref, wt_ref, cos_ref, sin_ref, o_ref, n_scr, *,
                        rope_tiles, scale_tiles, scale):
    j = pl.program_id(1)

    @pl.when(j == 0)
    def _():
        n_scr[...] = _rms(h_ref[...], g_ref[...], NORM_EPS).astype(BF16)

    acc = lax.dot_general(wt_ref[...], n_scr[...], NT_DIMS, preferred_element_type=F32)
    sc = jnp.where(_tile_in(j, scale_tiles), scale, 1.0).astype(F32)
    half = HEAD_DIM // 2

    def plain():
        o_ref[...] = (acc * sc).astype(BF16)

    def roped():
        cos, sin = cos_ref[...], sin_ref[...]
        for hd in range(acc.shape[0] // HEAD_DIM):
            x1 = acc[hd * HEAD_DIM:hd * HEAD_DIM + half]
            x2 = acc[hd * HEAD_DIM + half:(hd + 1) * HEAD_DIM]
            o_ref[hd * HEAD_DIM:hd * HEAD_DIM + half] = ((x1 * cos - x2 * sin) * sc).astype(BF16)
            o_ref[hd * HEAD_DIM + half:(hd + 1) * HEAD_DIM] = ((x2 * cos + x1 * sin) * sc).astype(BF16)

    if rope_tiles:
        is_rope = _tile_in(j, rope_tiles)
        pl.when(is_rope)(roped)
        pl.when(jnp.logical_not(is_rope))(plain)
    else:
        plain()


def _norm_proj_t(h, g, wt, cos_t, sin_t, *, rope_tiles, scale_tiles, scale, tm, tn):
    s, d = h.shape
    n = wt.shape[0]
    kern = functools.partial(_norm_proj_t_kernel, rope_tiles=rope_tiles,
                             scale_tiles=scale_tiles, scale=scale)
    return pl.pallas_call(
        kern,
        grid=(s // tm, n // tn),
        in_specs=[
            pl.BlockSpec((tm, d), lambda i, j: (i, 0)),
            pl.BlockSpec((1, d), lambda i, j: (0, 0)),
            pl.BlockSpec((tn, d), lambda i, j: (j, 0)),
            pl.BlockSpec((HEAD_DIM // 2, tm), lambda i, j: (0, i)),
            pl.BlockSpec((HEAD_DIM // 2, tm), lambda i, j: (0, i)),
        ],
        out_specs=pl.BlockSpec((tn, tm), lambda i, j: (j, i)),
        out_shape=jax.ShapeDtypeStruct((n, s), BF16),
        scratch_shapes=[pltpu.VMEM((tm, d), BF16)],
        compiler_params=_params(("parallel", "arbitrary")),
        name="norm_proj_t",
    )(h, g, wt, cos_t, sin_t)


GATE_TERMS = 3
GATE_STRIDE = 8
LOG2E = math.log2(math.e)


def _fox_gate_kernel(h_ref, g_ref, wf_ref, bf_ref, cf_ref, carry, *, tc, n_heads):
    i = pl.program_id(0)

    @pl.when(i == 0)
    def _():
        carry[...] = jnp.zeros_like(carry)

    n = _rms(h_ref[...], g_ref[...], NORM_EPS).astype(BF16)
    z = jnp.dot(n, wf_ref[...], preferred_element_type=F32) + bf_ref[...]
    logf = jnp.minimum(z, 0.0) - jnp.log(1.0 + jnp.exp(-jnp.abs(z)))
    dst = lax.broadcasted_iota(jnp.int32, (tc, tc), 0)
    src = lax.broadcasted_iota(jnp.int32, (tc, tc), 1)
    prefix = jnp.where(src <= dst, 1.0, 0.0).astype(F32)
    cs = jnp.dot(prefix, logf, precision=lax.Precision.HIGHEST,
                 preferred_element_type=F32) + carry[0:1, :]
    carry[...] = jnp.broadcast_to(cs[tc - 1:tc, :], carry.shape)
    lane = lax.broadcasted_iota(jnp.int32, (tc, LANES), 1)
    rem = jnp.where(lane < n_heads, cs * LOG2E, 0.0)
    feat = jnp.zeros_like(rem)
    for term in range(GATE_TERMS):
        part = rem.astype(BF16).astype(F32)
        rem = rem - part
        feat = feat + (pltpu.roll(part, term * GATE_STRIDE, 1) if term else part)
    cf_ref[...] = feat.astype(BF16)


def _fox_gate(h, g, wf, b_f, *, tc, n_heads):
    s, d = h.shape
    return pl.pallas_call(
        functools.partial(_fox_gate_kernel, tc=tc, n_heads=n_heads),
        grid=(s // tc,),
        in_specs=[
            pl.BlockSpec((tc, d), lambda i: (i, 0)),
            pl.BlockSpec((1, d), lambda i: (0, 0)),
            pl.BlockSpec((d, LANES), lambda i: (0, 0)),
            pl.BlockSpec((1, LANES), lambda i: (0, 0)),
        ],
        out_specs=pl.BlockSpec((tc, LANES), lambda i: (i, 0)),
        out_shape=jax.ShapeDtypeStruct((s, LANES), BF16),
        scratch_shapes=[pltpu.VMEM((8, LANES), F32)],
        compiler_params=_params(("arbitrary",)),
        name="fox_gate",
    )(h, g, wf, b_f)


def _dilated_kernel(q_ref, kc_ref, kh_ref, vc_ref, vh_ref, o_ref, l_ref, kbuf, vbuf, *, tq):
    i = pl.program_id(1)
    kbuf[0:BAND] = kh_ref[...]
    kbuf[BAND:] = kc_ref[...]
    vbuf[0:BAND] = vh_ref[...]
    vbuf[BAND:] = vc_ref[...]
    row = lax.broadcasted_iota(jnp.int32, (BAND, 2 * BAND), 0)
    col = lax.broadcasted_iota(jnp.int32, (BAND, 2 * BAND), 1)
    delta = row - col + BAND
    band = (delta >= 0) & (delta <= BAND)
    lo = _lane_lo(BAND)
    for a in range(tq // BAND):
        first_key = i * tq + (a - 1) * BAND
        valid = band & (col + first_key >= 0)
        rows = slice(a * BAND, (a + 1) * BAND)
        for hp in range(q_ref.shape[1] // LANES):
            lanes = slice(hp * LANES, (hp + 1) * LANES)
            q = q_ref[rows, lanes]
            kk = kbuf[a * BAND:(a + 2) * BAND, lanes]
            vv = vbuf[a * BAND:(a + 2) * BAND, lanes]
            outs, lses = [], []
            for qh in (jnp.where(lo, q, jnp.zeros_like(q)), jnp.where(lo, jnp.zeros_like(q), q)):
                s = lax.dot_general(qh, kk, NT_DIMS, preferred_element_type=F32)
                s = jnp.where(valid, s, NEG)
                m = jnp.max(s, axis=1, keepdims=True)
                p = jnp.exp(s - m)
                den = jnp.sum(p, axis=1, keepdims=True)
                pv = jnp.dot(p.astype(BF16), vv, preferred_element_type=F32)
                outs.append(pv / den)
                lses.append(jnp.broadcast_to(m + jnp.log(den), (BAND, LANES)))
            o_ref[rows, lanes] = jnp.where(lo, outs[0], outs[1])
            l_ref[rows, lanes] = jnp.where(lo, lses[0], lses[1])


def _dilated_pattern(proj, dil, *, width, tq):
    s, n = proj.shape
    L = s // dil
    view = proj.reshape(L, dil * n)
    per_res = n // width
    halo_per_tile = tq // BAND
    cur = lambda part: pl.BlockSpec((tq, width), lambda r, i: (i, r * per_res + part))
    halo = lambda part: pl.BlockSpec(
        (BAND, width), lambda r, i: (jnp.maximum(i * halo_per_tile - 1, 0), r * per_res + part))
    out_spec = pl.BlockSpec((tq, width), lambda r, i: (i, r))
    o, lse = pl.pallas_call(
        functools.partial(_dilated_kernel, tq=tq),
        grid=(dil, L // tq),
        in_specs=[cur(0), cur(1), halo(1), cur(2), halo(2)],
        out_specs=[out_spec, out_spec],
        out_shape=[jax.ShapeDtypeStruct((L, dil * width), F32)] * 2,
        scratch_shapes=[pltpu.VMEM((tq + BAND, width), BF16)] * 2,
        compiler_params=_params(("parallel", "arbitrary")),
        name=f"dilated_d{dil}",
    )(view, view, view, view, view)
    return o.reshape(s, width), lse.reshape(s, width)


ONES_ROWS = BF16_SUBLANES


COL_TILE = 256
STRIPS_PER_TRIP = 4


def _flash_pair_kernel(qi_ref, kj_ref, qt_ref, k_ref, *rest, mode, tq, tk, gps, lam_init):
    if mode == "fox":
        cf_ref, vt_ref, o_ref, wq, m_scr, acc, kx_scr, vx_scr, *bufs = rest
    else:
        vt_ref, lam_ref, g_ref, o_ref, wq, m_scr, acc, kx_scr, vx_scr, *bufs = rest
    s_buf, mx_buf, p_buf, al_buf = bufs[0:2], bufs[2:4], bufs[4:6], bufs[6:8]
    U = STRIPS_PER_TRIP
    step = pl.program_id(1)
    qi = qi_ref[step]
    kj = kj_ref[step]
    last_kj = (qi * tq + tq - 1) // tk
    rows = acc.shape[1]
    dv = rows - ONES_ROWS
    n_ct = tq // COL_TILE
    n_strips = 2 * gps * n_ct
    n_trips = n_strips // U
    par = kj % 2

    @pl.when(jnp.logical_and(pl.program_id(0) == 0, step == 0))
    def _():
        for buf in bufs:
            buf[...] = jnp.zeros_like(buf)
        vx_scr[...] = jnp.zeros_like(vx_scr)

    @pl.when(kj == 0)
    def _():
        r = lax.broadcasted_iota(jnp.int32, (LANES, COL_TILE), 0)
        first = r < HEAD_DIM
        for gi in range(gps):
            for a in range(2):
                keep_q = first if a == 0 else jnp.logical_not(first)
                if mode == "fox":
                    head = 2 * (pl.program_id(0) * gps + gi) + a
                    pick = (r - head) == 0
                    for term in range(1, GATE_TERMS):
                        pick = pick | ((r - head) == term * GATE_STRIDE)
                    gate_rows = jnp.where(pick, -1.0, 0.0).astype(BF16)
                for ct in range(n_ct):
                    t = (2 * gi + a) * n_ct + ct
                    qt = qt_ref[gi * LANES:(gi + 1) * LANES, ct * COL_TILE:(ct + 1) * COL_TILE]
                    wq[t, 0:LANES] = jnp.where(keep_q, qt, jnp.zeros_like(qt))
                    if mode == "fox":
                        wq[t, LANES:2 * LANES] = gate_rows
        m_scr[...] = jnp.full_like(m_scr, NEG)
        acc[...] = jnp.zeros_like(acc)

    ones = jnp.ones((ONES_ROWS, tk), BF16)
    for gi in range(gps):
        kx_scr[gi, :, 0:LANES] = k_ref[:, gi * LANES:(gi + 1) * LANES]
        if mode == "fox":
            kx_scr[gi, :, LANES:2 * LANES] = cf_ref[...]
        for a in range(2):
            if mode == "fox":
                v0 = gi * LANES + a * HEAD_DIM
                vx_scr[par, 2 * gi + a, 0:dv] = vt_ref[v0:v0 + HEAD_DIM, :]
            else:
                vx_scr[par, 2 * gi + a, 0:dv] = vt_ref[gi * LANES:(gi + 1) * LANES, :]
            vx_scr[par, 2 * gi + a, dv:rows] = ones

    def stage_a(j, slot, u, masked):
        s = jnp.dot(kx_scr[j // (2 * n_ct)], wq[j], preferred_element_type=F32)
        if masked:
            key = kj * tk + lax.broadcasted_iota(jnp.int32, (tk, COL_TILE), 0)
            qry = (qi * tq + (j % n_ct) * COL_TILE
                   + lax.broadcasted_iota(jnp.int32, (tk, COL_TILE), 1))
            s = jnp.where(key <= qry, s, NEG)
        s_buf[slot][u] = s
        mx_buf[slot][u] = jnp.broadcast_to(jnp.max(s, axis=0, keepdims=True), (8, COL_TILE))


    def stage_b(j0, slot, valid, us):
        m_prevs = {u: m_scr[j0 + u] for u in us}
        for u, m_prev in list(m_prevs.items()):
            m_next = jnp.maximum(m_prev, mx_buf[slot][u])
            if valid is not None:
                m_next = jnp.where(valid, m_next, m_prev)
            al_buf[slot][u] = jnp.exp2(m_prev - m_next)
            s3 = s_buf[slot][u].reshape(tk // 8, 8, COL_TILE)
            p_buf[slot][u] = jnp.exp2(s3 - m_next[None]).reshape(tk, COL_TILE).astype(BF16)
            m_prevs[u] = m_next
        for u, m_next in m_prevs.items():
            m_scr[j0 + u] = m_next

    def stage_c(j0, slot, vpar, valid, us):
        pvs = [jnp.dot(vx_scr[vpar, (j0 + u) // n_ct], p_buf[slot][u],
                       preferred_element_type=F32) for u in us]
        olds = [acc[j0 + u] for u in us]
        for u, old, pv in zip(us, olds, pvs):
            new = ((old.reshape(rows // 8, 8, COL_TILE) * al_buf[slot][u][None])
                   .reshape(rows, COL_TILE) + pv)
            acc[j0 + u] = new if valid is None else jnp.where(valid, new, old)

    def sweep(masked):
        def trip(k, slot):
            later_step = kj > 0
            kc = (k + n_trips - 2) % n_trips
            kb = (k + n_trips - 1) % n_trips
            valid_c = jnp.logical_or(later_step, k >= 2)
            valid_b = jnp.logical_or(later_step, k >= 1)
            vpar = jnp.where(k >= 2, par, 1 - par)
            for u in range(U):
                stage_a(k * U + u, slot, u, masked)
            stage_b(kb * U, 1 - slot, valid_b, range(U))
            stage_c(kc * U, slot, vpar, valid_c, range(U))

        def trip_pair(half, carry):
            trip(2 * half, 0)
            trip(2 * half + 1, 1)
            return carry

        if n_trips == 2:
            trip_pair(0, 0)
        else:
            lax.fori_loop(0, n_trips // 2, trip_pair, 0)

    needs_mask = kj * tk + tk - 1 > qi * tq
    pl.when(needs_mask)(lambda: sweep(True))
    pl.when(jnp.logical_not(needs_mask))(lambda: sweep(False))

    @pl.when(kj == last_kj)
    def _():
        last = n_trips - 1
        stage_c((last - 1) * U, (last - 1) % 2, par, None, range(U))
        stage_b(last * U, last % 2, None, range(U))
        stage_c(last * U, last % 2, par, None, range(U))

        def map_out(mi):
            parts = [acc[mi * n_ct + ct] for ct in range(n_ct)]
            full = jnp.concatenate(parts, axis=1)
            return full[0:dv] / full[dv:dv + 1]

        for gi in range(gps):
            o0, o1 = map_out(2 * gi), map_out(2 * gi + 1)
            if mode == "fox":
                o = jnp.concatenate([o0, o1], axis=0)
            else:
                lp = lam_ref[...]
                t1 = jnp.sum(lp[0:1] * lp[1:2], axis=1, keepdims=True)
                t2 = jnp.sum(lp[2:3] * lp[3:4], axis=1, keepdims=True)
                lam = jnp.exp(t1) - jnp.exp(t2) + lam_init
                o = o0 - lam * o1
                ms = jnp.mean(o * o, axis=0, keepdims=True)
                o = o * lax.rsqrt(ms + SUBLN_EPS) * g_ref[...] * (1.0 - lam_init)
            o_ref[:, gi * LANES:(gi + 1) * LANES] = o.T.astype(BF16)


def _causal_steps(s, tq, tk):
    qi, kj = [], []
    for i in range(s // tq):
        for j in range((i * tq + tq - 1) // tk + 1):
            qi.append(i)
            kj.append(j)
    return jnp.asarray(qi, jnp.int32), jnp.asarray(kj, jnp.int32)


def _flash_pair(proj_t, keys, extras, *, mode, n_groups, gps, q_row, v_row, k_col, tq, tk,
                lam_init=0.0):
    s = keys.shape[0]
    qi, kj = _causal_steps(s, tq, tk)
    gw = gps * LANES
    qt_spec = pl.BlockSpec((gw, tq), lambda g, t, qi, kj: (q_row + g, qi[t]))
    k_spec = pl.BlockSpec((tk, gw), lambda g, t, qi, kj: (kj[t], k_col + g))
    vt_spec = pl.BlockSpec((gw, tk), lambda g, t, qi, kj: (v_row + g, kj[t]))
    const = lambda x: pl.BlockSpec(x.shape, lambda g, t, qi, kj: (0, 0))
    if mode == "fox":
        (cfeat,) = extras
        in_specs = [qt_spec, k_spec,
                    pl.BlockSpec((tk, LANES), lambda g, t, qi, kj: (kj[t], 0)), vt_spec]
        args = (proj_t, keys, cfeat, proj_t)
        kd, dv = 2 * LANES, HEAD_DIM
    else:
        lam_params, subln_g = extras
        in_specs = [qt_spec, k_spec, vt_spec, const(lam_params), const(subln_g)]
        args = (proj_t, keys, proj_t, lam_params, subln_g)
        kd, dv = LANES, LANES
    rows = dv + ONES_ROWS
    n_strips = 2 * gps * (tq // COL_TILE)
    u = STRIPS_PER_TRIP
    assert n_strips % (2 * u) == 0
    kern = functools.partial(_flash_pair_kernel, mode=mode, tq=tq, tk=tk, gps=gps,
                             lam_init=lam_init)
    return pl.pallas_call(
        kern,
        grid_spec=pltpu.PrefetchScalarGridSpec(
            num_scalar_prefetch=2,
            grid=(n_groups // gps, qi.shape[0]),
            in_specs=in_specs,
            out_specs=pl.BlockSpec((tq, gw), lambda g, t, qi, kj: (qi[t], g)),
            scratch_shapes=[
                pltpu.VMEM((n_strips, kd, COL_TILE), BF16),
                pltpu.VMEM((n_strips, 8, COL_TILE), F32),
                pltpu.VMEM((n_strips, rows, COL_TILE), F32),
                pltpu.VMEM((gps, tk, kd), BF16),
                pltpu.VMEM((2, 2 * gps, rows, tk), BF16),
            ] + [pltpu.VMEM((u, tk, COL_TILE), F32)] * 2
              + [pltpu.VMEM((u, 8, COL_TILE), F32)] * 2
              + [pltpu.VMEM((u, tk, COL_TILE), BF16)] * 2
              + [pltpu.VMEM((u, 8, COL_TILE), F32)] * 2,
        ),
        out_shape=jax.ShapeDtypeStruct((s, n_groups * LANES), BF16),
        compiler_params=_params(("arbitrary", "arbitrary")),
        name=f"flash_{mode}",
    )(qi, kj, *args)


def _hyb_out_kernel(o1, o2, o3, l1, l2, l3, ob_ref, w_ref, h_ref, out_ref):
    a1, a2, a3 = l1[...], l2[...], l3[...]
    m = jnp.maximum(jnp.maximum(a1, a2), a3)
    e1, e2, e3 = jnp.exp(a1 - m), jnp.exp(a2 - m), jnp.exp(a3 - m)
    oa = (e1 * o1[...] + e2 * o2[...] + e3 * o3[...]) / (e1 + e2 + e3)
    wa = oa.shape[1]
    acc = jnp.dot(oa.astype(BF16), w_ref[0:wa, :], preferred_element_type=F32)
    acc = acc + jnp.dot(ob_ref[...], w_ref[wa:, :], preferred_element_type=F32)
    out_ref[...] = h_ref[...] + acc


def _hyb_out(pattern_outs, ob, w, h, *, tm):
    s, d = h.shape
    wa = pattern_outs[0][0].shape[1]
    part = pl.BlockSpec((tm, wa), lambda i: (i, 0))
    os_ = [o for o, _ in pattern_outs]
    ls_ = [l for _, l in pattern_outs]
    return pl.pallas_call(
        _hyb_out_kernel,
        grid=(s // tm,),
        in_specs=[part] * 6 + [
            pl.BlockSpec((tm, ob.shape[1]), lambda i: (i, 0)),
            pl.BlockSpec(w.shape, lambda i: (0, 0)),
            pl.BlockSpec((tm, d), lambda i: (i, 0)),
        ],
        out_specs=pl.BlockSpec((tm, d), lambda i: (i, 0)),
        out_shape=jax.ShapeDtypeStruct((s, d), F32),
        compiler_params=_params(("parallel",)),
        name="hyb_out",
    )(*os_, *ls_, ob, w, h)


def _proj_res_kernel(a_ref, w_ref, h_ref, out_ref):
    out_ref[...] = h_ref[...] + jnp.dot(a_ref[...], w_ref[...], preferred_element_type=F32)


def _proj_res(a, w, h, *, tm):
    s, d = h.shape
    return pl.pallas_call(
        _proj_res_kernel,
        grid=(s // tm,),
        in_specs=[
            pl.BlockSpec((tm, a.shape[1]), lambda i: (i, 0)),
            pl.BlockSpec(w.shape, lambda i: (0, 0)),
            pl.BlockSpec((tm, d), lambda i: (i, 0)),
        ],
        out_specs=pl.BlockSpec((tm, d), lambda i: (i, 0)),
        out_shape=jax.ShapeDtypeStruct((s, d), F32),
        compiler_params=_params(("parallel",)),
        name="proj_res",
    )(a, w, h)


HALO = BF16_SUBLANES


def _ffn_kernel(h_ref, halo_ref, g_ref, wg_ref, wu_ref, cw_ref, cb_ref, wd_ref, *rest,
                tm, final):
    if final:
        fg_ref, out_ref, n_scr, gate_scr, acc = rest
    else:
        out_ref, n_scr, gate_scr, acc = rest
    i = pl.program_id(0)
    j = pl.program_id(1)

    @pl.when(j == 0)
    def _():
        g = g_ref[...]
        prev = jnp.where(i > 0, halo_ref[...], 0.0)
        n_scr[0:HALO] = _rms(prev, g, NORM_EPS).astype(BF16)
        n_scr[HALO:] = _rms(h_ref[...], g, NORM_EPS).astype(BF16)
        acc[...] = jnp.zeros_like(acc)

    gate_scr[...] = jnp.dot(n_scr[...], wg_ref[...], preferred_element_type=F32)
    up = jnp.dot(n_scr[HALO:], wu_ref[...], preferred_element_type=F32)
    conv = cb_ref[...]
    for t in range(CONV_WIDTH):
        start = HALO - (CONV_WIDTH - 1) + t
        conv = conv + gate_scr[start:start + tm] * cw_ref[t:t + 1, :]
    act = conv * (1.0 / (1.0 + jnp.exp(-conv))) * up
    acc[...] += jnp.dot(act.astype(BF16), wd_ref[...], preferred_element_type=F32)

    @pl.when(j == pl.num_programs(1) - 1)
    def _():
        y = h_ref[...] + acc[...]
        if final:
            y = _rms(y, fg_ref[...], NORM_EPS)
        out_ref[...] = y


def _ffn(h, g, w_up, conv_w, conv_b, w_down, final_g, *, tm, tf):
    s, d = h.shape
    d_ff = w_down.shape[0]
    nf = d_ff // tf
    final = final_g is not None
    in_specs = [
        pl.BlockSpec((tm, d), lambda i, j: (i, 0)),
        pl.BlockSpec((HALO, d), lambda i, j: (jnp.maximum(i * (tm // HALO) - 1, 0), 0)),
        pl.BlockSpec((1, d), lambda i, j: (0, 0)),
        pl.BlockSpec((d, tf), lambda i, j: (0, j)),
        pl.BlockSpec((d, tf), lambda i, j: (0, nf + j)),
        pl.BlockSpec((CONV_WIDTH, tf), lambda i, j: (0, j)),
        pl.BlockSpec((1, tf), lambda i, j: (0, j)),
        pl.BlockSpec((tf, d), lambda i, j: (j, 0)),
    ]
    args = [h, h, g, w_up, w_up, conv_w, conv_b, w_down]
    if final:
        in_specs.append(pl.BlockSpec((1, d), lambda i, j: (0, 0)))
        args.append(final_g)
    return pl.pallas_call(
        functools.partial(_ffn_kernel, tm=tm, final=final),
        grid=(s // tm, nf),
        in_specs=in_specs,
        out_specs=pl.BlockSpec((tm, d), lambda i, j: (i, 0)),
        out_shape=jax.ShapeDtypeStruct((s, d), F32),
        scratch_shapes=[
            pltpu.VMEM((tm + HALO, d), BF16),
            pltpu.VMEM((tm + HALO, tf), F32),
            pltpu.VMEM((tm, d), F32),
        ],
        compiler_params=_params(("parallel", "arbitrary")),
        name="ffn_final" if final else "ffn",
    )(*args)


def _rope_tables(s):
    inv = 1.0 / (ROPE_THETA ** (jnp.arange(0, HEAD_DIM, 2, dtype=F32) / HEAD_DIM))
    ang = jnp.arange(s, dtype=F32)[:, None] * inv[None, :]
    cos, sin = jnp.cos(ang), jnp.sin(ang)
    sign = jnp.where((jnp.arange(LANES) & 32) == 0, -1.0, 1.0).astype(F32)
    cos_l = jnp.tile(cos, (1, LANES // 32))
    sin_signed = jnp.tile(sin, (1, LANES // 32)) * sign[None, :]
    return cos_l, sin_signed, cos.T, sin.T


def kernel(x, attn_norm, ffn_norm, final_norm, hyb_w_in, hyb_b_f, hyb_w_out, diff_w_qkv,
           diff_lambda, diff_subln, diff_w_out, ffn_w_up, ffn_conv_w, ffn_conv_b, ffn_w_down):
    b, s, d = x.shape
    assert b == 1
    depth = attn_norm.shape[0]
    width = hyb_w_out.shape[1] // 2
    n_pairs = width // LANES
    n_heads_b = width // HEAD_DIM
    n_diff_heads = diff_w_out.shape[1] // LANES
    dq = n_diff_heads * LANES
    tm = min(1024, s)
    tq_flash, tk_flash = min(1024, s), min(512, s)
    gps = 2
    q_scale = HEAD_DIM ** -0.5
    cos, sin_signed, cos_t, sin_t = _rope_tables(s)
    h = x[0]

    for l in range(depth):
        g_attn = attn_norm[l][None, :]
        if l % 2 == 0:
            e = l // 2
            w_in = hyb_w_in[e]
            qa_ka_va, qb, kb, vb, wf = (w_in[:, :3 * width], w_in[:, 3 * width:4 * width],
                                        w_in[:, 4 * width:5 * width], w_in[:, 5 * width:6 * width],
                                        w_in[:, 6 * width:])
            proj = _norm_proj(h, g_attn, jnp.concatenate([qa_ka_va, kb], axis=1).astype(BF16),
                              cos, sin_signed, rope_tiles=(0, 1), scale_tiles=(0,),
                              scale=q_scale, tm=tm, tn=width)
            proj_t = _norm_proj_t(h, g_attn, jnp.concatenate([qb, vb], axis=1).T.astype(BF16),
                                  cos_t, sin_t, rope_tiles=(), scale_tiles=(0,),
                                  scale=q_scale * LOG2E, tm=tm, tn=width)
            wf_pad = jnp.pad(wf, ((0, 0), (0, LANES - n_heads_b))).astype(BF16)
            bf_pad = jnp.pad(hyb_b_f[e], (0, LANES - n_heads_b))[None, :]
            cfeat = _fox_gate(h, g_attn, wf_pad, bf_pad, tc=min(512, s), n_heads=n_heads_b)
            pats = [_dilated_pattern(proj, dil, width=width, tq=min(512, s // dil))
                    for _, dil in DILATED_PATTERNS]
            ob = _flash_pair(proj_t, proj, (cfeat,), mode="fox", n_groups=n_pairs, gps=gps,
                             q_row=0, v_row=n_pairs // gps, k_col=3 * n_pairs // gps,
                             tq=tq_flash, tk=tk_flash)
            h = _hyb_out(pats, ob, hyb_w_out[e].astype(BF16), h, tm=min(512, s))
        else:
            o = l // 2
            w = diff_w_qkv[o]
            wq, wk, wv = w[:, :dq], w[:, dq:2 * dq], w[:, 2 * dq:]
            k_tiles = dq // 512
            keys = _norm_proj(h, g_attn, wk.astype(BF16), cos, sin_signed,
                              rope_tiles=tuple(range(k_tiles)), scale_tiles=(),
                              scale=1.0, tm=tm, tn=512)
            proj_t = _norm_proj_t(h, g_attn, jnp.concatenate([wq, wv], axis=1).T.astype(BF16),
                                  cos_t, sin_t, rope_tiles=tuple(range(k_tiles)),
                                  scale_tiles=tuple(range(k_tiles)), scale=q_scale * LOG2E,
                                  tm=tm, tn=512)
            lam_init = 0.8 - 0.6 * math.exp(-0.3 * l)
            att = _flash_pair(proj_t, keys, (diff_lambda[o], diff_subln[o][:, None]),
                              mode="diff", n_groups=n_diff_heads, gps=gps, q_row=0,
                              v_row=n_diff_heads // gps, k_col=0, tq=tq_flash, tk=tk_flash,
                              lam_init=lam_init)
            h = _proj_res(att, diff_w_out[o].astype(BF16), h, tm=min(512, s))
        h = _ffn(h, ffn_norm[l][None, :], ffn_w_up[l].astype(BF16), ffn_conv_w[l],
                 ffn_conv_b[l][None, :], ffn_w_down[l].astype(BF16),
                 final_norm[None, :] if l == depth - 1 else None, tm=tm, tf=256)
    return h[None]
```

```python
import functools
import math

import jax
import jax.numpy as jnp
from jax import lax
from jax.experimental import pallas as pl
from jax.experimental.pallas import tpu as pltpu

F32 = jnp.float32
BF16 = jnp.bfloat16

HEAD_DIM = 64
LANES = 128
BF16_SUBLANES = 16
ROPE_THETA = 10000.0
DILATED_PATTERNS = ((128, 1), (512, 4), (2048, 16))
BAND = 128
NORM_EPS = 1e-6
SUBLN_EPS = 1e-5
CONV_WIDTH = 3
NEG = -1e30
VMEM_LIMIT = 48 * 1024 * 1024

NT_DIMS = (((1,), (1,)), ((), ()))


def _params(sem):
    return pltpu.CompilerParams(dimension_semantics=sem, vmem_limit_bytes=VMEM_LIMIT)


def _rms(x, g, eps):
    return x * lax.rsqrt(jnp.mean(x * x, axis=-1, keepdims=True) + eps) * g


def _tile_in(j, tiles):
    pred = j < 0
    for t in tiles:
        pred = jnp.logical_or(pred, j == t)
    return pred


def _lane_lo(rows):
    return lax.broadcasted_iota(jnp.int32, (rows, LANES), 1) < HEAD_DIM


def _rope_tile(x, cos, sin_signed):
    rows = x.shape[0]
    first_half = (lax.broadcasted_iota(jnp.int32, (rows, LANES), 1) & 32) == 0
    out = []
    for c in range(x.shape[1] // LANES):
        xc = x[:, c * LANES:(c + 1) * LANES]
        ahead = pltpu.roll(xc, LANES - 32, 1)
        behind = pltpu.roll(xc, 32, 1)
        rot = jnp.where(first_half, ahead, behind)
        out.append(xc * cos + rot * sin_signed)
    return jnp.concatenate(out, axis=1)


def _norm_proj_kernel(h_ref, g_ref, w_ref, cos_ref, sin_ref, o_ref, n_scr, *,
                      rope_tiles, scale_tiles, scale):
    j = pl.program_id(1)

    @pl.when(j == 0)
    def _():
        n_scr[...] = _rms(h_ref[...], g_ref[...], NORM_EPS).astype(BF16)

    acc = jnp.dot(n_scr[...], w_ref[...], preferred_element_type=F32)
    sc = jnp.where(_tile_in(j, scale_tiles), scale, 1.0).astype(F32)
    is_rope = _tile_in(j, rope_tiles)

    @pl.when(is_rope)
    def _():
        o_ref[...] = (_rope_tile(acc, cos_ref[...], sin_ref[...]) * sc).astype(BF16)

    @pl.when(jnp.logical_not(is_rope))
    def _():
        o_ref[...] = (acc * sc).astype(BF16)


def _norm_proj(h, g, w, cos, sin_signed, *, rope_tiles, scale_tiles, scale, tm, tn):
    s, d = h.shape
    n = w.shape[1]
    kern = functools.partial(_norm_proj_kernel, rope_tiles=rope_tiles,
                             scale_tiles=scale_tiles, scale=scale)
    return pl.pallas_call(
        kern,
        grid=(s // tm, n // tn),
        in_specs=[
            pl.BlockSpec((tm, d), lambda i, j: (i, 0)),
            pl.BlockSpec((1, d), lambda i, j: (0, 0)),
            pl.BlockSpec((d, tn), lambda i, j: (0, j)),
            pl.BlockSpec((tm, LANES), lambda i, j: (i, 0)),
            pl.BlockSpec((tm, LANES), lambda i, j: (i, 0)),
        ],
        out_specs=pl.BlockSpec((tm, tn), lambda i, j: (i, j)),
        out_shape=jax.ShapeDtypeStruct((s, n), BF16),
        scratch_shapes=[pltpu.VMEM((tm, d), BF16)],
        compiler_params=_params(("parallel", "arbitrary")),
        name="norm_proj",
    )(h, g, w, cos, sin_signed)


def _norm_proj_t_kernel(h_ref, g_ref, wt_ref, cos_ref, sin_ref, o_ref, n_scr, *,
                        rope_tiles, scale_tiles, scale):
    j = pl.program_id(1)

    @pl.when(j == 0)
    def _():
        n_scr[...] = _rms(h_ref[...], g_ref[...], NORM_EPS).astype(BF16)

    acc = lax.dot_general(wt_ref[...], n_scr[...], NT_DIMS, preferred_element_type=F32)
    sc = jnp.where(_tile_in(j, scale_tiles), scale, 1.0).astype(F32)
    half = HEAD_DIM // 2

    def plain():
        o_ref[...] = (acc * sc).astype(BF16)

    def roped():
        cos, sin = cos_ref[...], sin_ref[...]
        for hd in range(acc.shape[0] // HEAD_DIM):
            x1 = acc[hd * HEAD_DIM:hd * HEAD_DIM + half]
            x2 = acc[hd * HEAD_DIM + half:(hd + 1) * HEAD_DIM]
            o_ref[hd * HEAD_DIM:hd * HEAD_DIM + half] = ((x1 * cos - x2 * sin) * sc).astype(BF16)
            o_ref[hd * HEAD_DIM + half:(hd + 1) * HEAD_DIM] = ((x2 * cos + x1 * sin) * sc).astype(BF16)

    if rope_tiles:
        is_rope = _tile_in(j, rope_tiles)
        pl.when(is_rope)(roped)
        pl.when(jnp.logical_not(is_rope))(plain)
    else:
        plain()


def _norm_proj_t(h, g, wt, cos_t, sin_t, *, rope_tiles, scale_tiles, scale, tm, tn):
    s, d = h.shape
    n = wt.shape[0]
    kern = functools.partial(_norm_proj_t_kernel, rope_tiles=rope_tiles,
                             scale_tiles=scale_tiles, scale=scale)
    return pl.pallas_call(
        kern,
        grid=(s // tm, n // tn),
        in_specs=[
            pl.BlockSpec((tm, d), lambda i, j: (i, 0)),
            pl.BlockSpec((1, d), lambda i, j: (0, 0)),
            pl.BlockSpec((tn, d), lambda i, j: (j, 0)),
            pl.BlockSpec((HEAD_DIM // 2, tm), lambda i, j: (0, i)),
            pl.BlockSpec((HEAD_DIM // 2, tm), lambda i, j: (0, i)),
        ],
        out_specs=pl.BlockSpec((tn, tm), lambda i, j: (j, i)),
        out_shape=jax.ShapeDtypeStruct((n, s), BF16),
        scratch_shapes=[pltpu.VMEM((tm, d), BF16)],
        compiler_params=_params(("parallel", "arbitrary")),
        name="norm_proj_t",
    )(h, g, wt, cos_t, sin_t)


GATE_TERMS = 3
GATE_STRIDE = 8
LOG2E = math.log2(math.e)


def _fox_gate_kernel(h_ref, g_ref, wf_ref, bf_ref, cf_ref, carry, *, tc, n_heads):
    i = pl.program_id(0)

    @pl.when(i == 0)
    def _():
        carry[...] = jnp.zeros_like(carry)

    n = _rms(h_ref[...], g_ref[...], NORM_EPS).astype(BF16)
    z = jnp.dot(n, wf_ref[...], preferred_element_type=F32) + bf_ref[...]
    logf = jnp.minimum(z, 0.0) - jnp.log(1.0 + jnp.exp(-jnp.abs(z)))
    dst = lax.broadcasted_iota(jnp.int32, (tc, tc), 0)
    src = lax.broadcasted_iota(jnp.int32, (tc, tc), 1)
    prefix = jnp.where(src <= dst, 1.0, 0.0).astype(F32)
    cs = jnp.dot(prefix, logf, precision=lax.Precision.HIGHEST,
                 preferred_element_type=F32) + carry[0:1, :]
    carry[...] = jnp.broadcast_to(cs[tc - 1:tc, :], carry.shape)
    lane = lax.broadcasted_iota(jnp.int32, (tc, LANES), 1)
    rem = jnp.where(lane < n_heads, cs * LOG2E, 0.0)
    feat = jnp.zeros_like(rem)
    for term in range(GATE_TERMS):
        part = rem.astype(BF16).astype(F32)
        rem = rem - part
        feat = feat + (pltpu.roll(part, term * GATE_STRIDE, 1) if term else part)
    cf_ref[...] = feat.astype(BF16)


def _fox_gate(h, g, wf, b_f, *, tc, n_heads):
    s, d = h.shape
    return pl.pallas_call(
        functools.partial(_fox_gate_kernel, tc=tc, n_heads=n_heads),
        grid=(s // tc,),
        in_specs=[
            pl.BlockSpec((tc, d), lambda i: (i, 0)),
            pl.BlockSpec((1, d), lambda i: (0, 0)),
            pl.BlockSpec((d, LANES), lambda i: (0, 0)),
            pl.BlockSpec((1, LANES), lambda i: (0, 0)),
        ],
        out_specs=pl.BlockSpec((tc, LANES), lambda i: (i, 0)),
        out_shape=jax.ShapeDtypeStruct((s, LANES), BF16),
        scratch_shapes=[pltpu.VMEM((8, LANES), F32)],
        compiler_params=_params(("arbitrary",)),
        name="fox_gate",
    )(h, g, wf, b_f)


def _dilated_kernel(q_ref, kc_ref, kh_ref, vc_ref, vh_ref, o_ref, l_ref, kbuf, vbuf, *, tq):
    i = pl.program_id(1)
    kbuf[0:BAND] = kh_ref[...]
    kbuf[BAND:] = kc_ref[...]
    vbuf[0:BAND] = vh_ref[...]
    vbuf[BAND:] = vc_ref[...]
    row = lax.broadcasted_iota(jnp.int32, (BAND, 2 * BAND), 0)
    col = lax.broadcasted_iota(jnp.int32, (BAND, 2 * BAND), 1)
    delta = row - col + BAND
    band = (delta >= 0) & (delta <= BAND)
    lo = _lane_lo(BAND)
    for a in range(tq // BAND):
        first_key = i * tq + (a - 1) * BAND
        valid = band & (col + first_key >= 0)
        rows = slice(a * BAND, (a + 1) * BAND)
        for hp in range(q_ref.shape[1] // LANES):
            lanes = slice(hp * LANES, (hp + 1) * LANES)
            q = q_ref[rows, lanes]
            kk = kbuf[a * BAND:(a + 2) * BAND, lanes]
            vv = vbuf[a * BAND:(a + 2) * BAND, lanes]
            outs, lses = [], []
            for qh in (jnp.where(lo, q, jnp.zeros_like(q)), jnp.where(lo, jnp.zeros_like(q), q)):
                s = lax.dot_general(qh, kk, NT_DIMS, preferred_element_type=F32)
                s = jnp.where(valid, s, NEG)
                m = jnp.max(s, axis=1, keepdims=True)
                p = jnp.exp(s - m)
                den = jnp.sum(p, axis=1, keepdims=True)
                pv = jnp.dot(p.astype(BF16), vv, preferred_element_type=F32)
                outs.append(pv / den)
                lses.append(jnp.broadcast_to(m + jnp.log(den), (BAND, LANES)))
            o_ref[rows, lanes] = jnp.where(lo, outs[0], outs[1])
            l_ref[rows, lanes] = jnp.where(lo, lses[0], lses[1])


def _dilated_pattern(proj, dil, *, width, tq):
    s, n = proj.shape
    L = s // dil
    view = proj.reshape(L, dil * n)
    per_res = n // width
    halo_per_tile = tq // BAND
    cur = lambda part: pl.BlockSpec((tq, width), lambda r, i: (i, r * per_res + part))
    halo = lambda part: pl.BlockSpec(
        (BAND, width), lambda r, i: (jnp.maximum(i * halo_per_tile - 1, 0), r * per_res + part))
    out_spec = pl.BlockSpec((tq, width), lambda r, i: (i, r))
    o, lse = pl.pallas_call(
        functools.partial(_dilated_kernel, tq=tq),
        grid=(dil, L // tq),
        in_specs=[cur(0), cur(1), halo(1), cur(2), halo(2)],
        out_specs=[out_spec, out_spec],
        out_shape=[jax.ShapeDtypeStruct((L, dil * width), F32)] * 2,
        scratch_shapes=[pltpu.VMEM((tq + BAND, width), BF16)] * 2,
        compiler_params=_params(("parallel", "arbitrary")),
        name=f"dilated_d{dil}",
    )(view, view, view, view, view)
    return o.reshape(s, width), lse.reshape(s, width)


ONES_ROWS = BF16_SUBLANES


COL_TILE = 256
STRIPS_PER_TRIP = 4


def _flash_pair_kernel(qi_ref, kj_ref, qt_ref, k_ref, *rest, mode, tq, tk, gps, lam_init):
    if mode == "fox":
        cf_ref, vt_ref, o_ref, wq, m_scr, acc, kx_scr, vx_scr, *bufs = rest
    else:
        vt_ref, lam_ref, g_ref, o_ref, wq, m_scr, acc, kx_scr, vx_scr, *bufs = rest
    s_buf, mx_buf, p_buf, al_buf = bufs[0:2], bufs[2:4], bufs[4:6], bufs[6:8]
    U = STRIPS_PER_TRIP
    step = pl.program_id(1)
    qi = qi_ref[step]
    kj = kj_ref[step]
    last_kj = (qi * tq + tq - 1) // tk
    rows = acc.shape[1]
    dv = rows - ONES_ROWS
    n_ct = tq // COL_TILE
    n_strips = 2 * gps * n_ct
    n_trips = n_strips // U
    par = kj % 2

    @pl.when(jnp.logical_and(pl.program_id(0) == 0, step == 0))
    def _():
        for buf in bufs:
            buf[...] = jnp.zeros_like(buf)
        vx_scr[...] = jnp.zeros_like(vx_scr)

    @pl.when(kj == 0)
    def _():
        r = lax.broadcasted_iota(jnp.int32, (LANES, COL_TILE), 0)
        first = r < HEAD_DIM
        for gi in range(gps):
            for a in range(2):
                keep_q = first if a == 0 else jnp.logical_not(first)
                if mode == "fox":
                    head = 2 * (pl.program_id(0) * gps + gi) + a
                    pick = (r - head) == 0
                    for term in range(1, GATE_TERMS):
                        pick = pick | ((r - head) == term * GATE_STRIDE)
                    gate_rows = jnp.where(pick, -1.0, 0.0).astype(BF16)
                for ct in range(n_ct):
                    t = (2 * gi + a) * n_ct + ct
                    qt = qt_ref[gi * LANES:(gi + 1) * LANES, ct * COL_TILE:(ct + 1) * COL_TILE]
                    wq[t, 0:LANES] = jnp.where(keep_q, qt, jnp.zeros_like(qt))
                    if mode == "fox":
                        wq[t, LANES:2 * LANES] = gate_rows
        m_scr[...] = jnp.full_like(m_scr, NEG)
        acc[...] = jnp.zeros_like(acc)

    ones = jnp.ones((ONES_ROWS, tk), BF16)
    for gi in range(gps):
        kx_scr[gi, :, 0:LANES] = k_ref[:, gi * LANES:(gi + 1) * LANES]
        if mode == "fox":
            kx_scr[gi, :, LANES:2 * LANES] = cf_ref[...]
        for a in range(2):
            if mode == "fox":
                v0 = gi * LANES + a * HEAD_DIM
                vx_scr[par, 2 * gi + a, 0:dv] = vt_ref[v0:v0 + HEAD_DIM, :]
            else:
                vx_scr[par, 2 * gi + a, 0:dv] = vt_ref[gi * LANES:(gi + 1) * LANES, :]
            vx_scr[par, 2 * gi + a, dv:rows] = ones

    def stage_a(j, slot, u, masked):
        s = jnp.dot(kx_scr[j // (2 * n_ct)], wq[j], preferred_element_type=F32)
        if masked:
            key = kj * tk + lax.broadcasted_iota(jnp.int32, (tk, COL_TILE), 0)
            qry = (qi * tq + (j % n_ct) * COL_TILE
                   + lax.broadcasted_iota(jnp.int32, (tk, COL_TILE), 1))
            s = jnp.where(key <= qry, s, NEG)
        s_buf[slot][u] = s
        mx_buf[slot][u] = jnp.broadcast_to(jnp.max(s, axis=0, keepdims=True), (8, COL_TILE))


    def stage_b(j0, slot, valid, us):
        m_prevs = {u: m_scr[j0 + u] for u in us}
        for u, m_prev in list(m_prevs.items()):
            m_next = jnp.maximum(m_prev, mx_buf[slot][u])
            if valid is not None:
                m_next = jnp.where(valid, m_next, m_prev)
            al_buf[slot][u] = jnp.exp2(m_prev - m_next)
            s3 = s_buf[slot][u].reshape(tk // 8, 8, COL_TILE)
            p_buf[slot][u] = jnp.exp2((s3 - m_next[None]).reshape(tk, COL_TILE).astype(BF16))
            m_prevs[u] = m_next
        for u, m_next in m_prevs.items():
            m_scr[j0 + u] = m_next

    def stage_c(j0, slot, vpar, valid, us):
        pvs = [jnp.dot(vx_scr[vpar, (j0 + u) // n_ct], p_buf[slot][u],
                       preferred_element_type=F32) for u in us]
        olds = [acc[j0 + u] for u in us]
        for u, old, pv in zip(us, olds, pvs):
            new = ((old.reshape(rows // 8, 8, COL_TILE) * al_buf[slot][u][None])
                   .reshape(rows, COL_TILE) + pv)
            acc[j0 + u] = new if valid is None else jnp.where(valid, new, old)

    def sweep(masked):
        def trip(k, slot):
            later_step = kj > 0
            kc = (k + n_trips - 2) % n_trips
            kb = (k + n_trips - 1) % n_trips
            valid_c = jnp.logical_or(later_step, k >= 2)
            valid_b = jnp.logical_or(later_step, k >= 1)
            vpar = jnp.where(k >= 2, par, 1 - par)
            for u in range(U):
                stage_a(k * U + u, slot, u, masked)
            stage_b(kb * U, 1 - slot, valid_b, range(U))
            stage_c(kc * U, slot, vpar, valid_c, range(U))

        def trip_pair(half, carry):
            trip(2 * half, 0)
            trip(2 * half + 1, 1)
            return carry

        if n_trips == 2:
            trip_pair(0, 0)
        else:
            lax.fori_loop(0, n_trips // 2, trip_pair, 0)

    needs_mask = kj * tk + tk - 1 > qi * tq
    pl.when(needs_mask)(lambda: sweep(True))
    pl.when(jnp.logical_not(needs_mask))(lambda: sweep(False))

    @pl.when(kj == last_kj)
    def _():
        last = n_trips - 1
        stage_c((last - 1) * U, (last - 1) % 2, par, None, range(U))
        stage_b(last * U, last % 2, None, range(U))
        stage_c(last * U, last % 2, par, None, range(U))

        def map_out(mi):
            parts = [acc[mi * n_ct + ct] for ct in range(n_ct)]
            full = jnp.concatenate(parts, axis=1)
            return full[0:dv] / full[dv:dv + 1]

        for gi in range(gps):
            o0, o1 = map_out(2 * gi), map_out(2 * gi + 1)
            if mode == "fox":
                o = jnp.concatenate([o0, o1], axis=0)
            else:
                lp = lam_ref[...]
                t1 = jnp.sum(lp[0:1] * lp[1:2], axis=1, keepdims=True)
                t2 = jnp.sum(lp[2:3] * lp[3:4], axis=1, keepdims=True)
                lam = jnp.exp(t1) - jnp.exp(t2) + lam_init
                o = o0 - lam * o1
                ms = jnp.mean(o * o, axis=0, keepdims=True)
                o = o * lax.rsqrt(ms + SUBLN_EPS) * g_ref[...] * (1.0 - lam_init)
            o_ref[:, gi * LANES:(gi + 1) * LANES] = o.T.astype(BF16)


def _causal_steps(s, tq, tk):
    qi, kj = [], []
    for i in range(s // tq):
        for j in range((i * tq + tq - 1) // tk + 1):
            qi.append(i)
            kj.append(j)
    return jnp.asarray(qi, jnp.int32), jnp.asarray(kj, jnp.int32)


def _flash_pair(proj_t, keys, extras, *, mode, n_groups, gps, q_row, v_row, k_col, tq, tk,
                lam_init=0.0):
    s = keys.shape[0]
    qi, kj = _causal_steps(s, tq, tk)
    gw = gps * LANES
    qt_spec = pl.BlockSpec((gw, tq), lambda g, t, qi, kj: (q_row + g, qi[t]))
    k_spec = pl.BlockSpec((tk, gw), lambda g, t, qi, kj: (kj[t], k_col + g))
    vt_spec = pl.BlockSpec((gw, tk), lambda g, t, qi, kj: (v_row + g, kj[t]))
    const = lambda x: pl.BlockSpec(x.shape, lambda g, t, qi, kj: (0, 0))
    if mode == "fox":
        (cfeat,) = extras
        in_specs = [qt_spec, k_spec,
                    pl.BlockSpec((tk, LANES), lambda g, t, qi, kj: (kj[t], 0)), vt_spec]
        args = (proj_t, keys, cfeat, proj_t)
        kd, dv = 2 * LANES, HEAD_DIM
    else:
        lam_params, subln_g = extras
        in_specs = [qt_spec, k_spec, vt_spec, const(lam_params), const(subln_g)]
        args = (proj_t, keys, proj_t, lam_params, subln_g)
        kd, dv = LANES, LANES
    rows = dv + ONES_ROWS
    n_strips = 2 * gps * (tq // COL_TILE)
    u = STRIPS_PER_TRIP
    assert n_strips % (2 * u) == 0
    kern = functools.partial(_flash_pair_kernel, mode=mode, tq=tq, tk=tk, gps=gps,
                             lam_init=lam_init)
    return pl.pallas_call(
        kern,
        grid_spec=pltpu.PrefetchScalarGridSpec(
            num_scalar_prefetch=2,
            grid=(n_groups // gps, qi.shape[0]),
            in_specs=in_specs,
            out_specs=pl.BlockSpec((tq, gw), lambda g, t, qi, kj: (qi[t], g)),
            scratch_shapes=[
                pltpu.VMEM((n_strips, kd, COL_TILE), BF16),
                pltpu.VMEM((n_strips, 8, COL_TILE), F32),
                pltpu.VMEM((n_strips, rows, COL_TILE), F32),
                pltpu.VMEM((gps, tk, kd), BF16),
                pltpu.VMEM((2, 2 * gps, rows, tk), BF16),
            ] + [pltpu.VMEM((u, tk, COL_TILE), F32)] * 2
              + [pltpu.VMEM((u, 8, COL_TILE), F32)] * 2
              + [pltpu.VMEM((u, tk, COL_TILE), BF16)] * 2
              + [pltpu.VMEM((u, 8, COL_TILE), F32)] * 2,
        ),
        out_shape=jax.ShapeDtypeStruct((s, n_groups * LANES), BF16),
        compiler_params=_params(("arbitrary", "arbitrary")),
        name=f"flash_{mode}",
    )(qi, kj, *args)


def _hyb_out_kernel(o1, o2, o3, l1, l2, l3, ob_ref, w_ref, h_ref, out_ref):
    a1, a2, a3 = l1[...], l2[...], l3[...]
    m = jnp.maximum(jnp.maximum(a1, a2), a3)
    e1, e2, e3 = jnp.exp(a1 - m), jnp.exp(a2 - m), jnp.exp(a3 - m)
    oa = (e1 * o1[...] + e2 * o2[...] + e3 * o3[...]) / (e1 + e2 + e3)
    wa = oa.shape[1]
    acc = jnp.dot(oa.astype(BF16), w_ref[0:wa, :], preferred_element_type=F32)
    acc = acc + jnp.dot(ob_ref[...], w_ref[wa:, :], preferred_element_type=F32)
    out_ref[...] = h_ref[...] + acc


def _hyb_out(pattern_outs, ob, w, h, *, tm):
    s, d = h.shape
    wa = pattern_outs[0][0].shape[1]
    part = pl.BlockSpec((tm, wa), lambda i: (i, 0))
    os_ = [o for o, _ in pattern_outs]
    ls_ = [l for _, l in pattern_outs]
    return pl.pallas_call(
        _hyb_out_kernel,
        grid=(s // tm,),
        in_specs=[part] * 6 + [
            pl.BlockSpec((tm, ob.shape[1]), lambda i: (i, 0)),
            pl.BlockSpec(w.shape, lambda i: (0, 0)),
            pl.BlockSpec((tm, d), lambda i: (i, 0)),
        ],
        out_specs=pl.BlockSpec((tm, d), lambda i: (i, 0)),
        out_shape=jax.ShapeDtypeStruct((s, d), F32),
        compiler_params=_params(("parallel",)),
        name="hyb_out",
    )(*os_, *ls_, ob, w, h)


def _proj_res_kernel(a_ref, w_ref, h_ref, out_ref):
    out_ref[...] = h_ref[...] + jnp.dot(a_ref[...], w_ref[...], preferred_element_type=F32)


def _proj_res(a, w, h, *, tm):
    s, d = h.shape
    return pl.pallas_call(
        _proj_res_kernel,
        grid=(s // tm,),
        in_specs=[
            pl.BlockSpec((tm, a.shape[1]), lambda i: (i, 0)),
            pl.BlockSpec(w.shape, lambda i: (0, 0)),
            pl.BlockSpec((tm, d), lambda i: (i, 0)),
        ],
        out_specs=pl.BlockSpec((tm, d), lambda i: (i, 0)),
        out_shape=jax.ShapeDtypeStruct((s, d), F32),
        compiler_params=_params(("parallel",)),
        name="proj_res",
    )(a, w, h)


HALO = BF16_SUBLANES


def _ffn_kernel(h_ref, halo_ref, g_ref, wg_ref, wu_ref, cw_ref, cb_ref, wd_ref, *rest,
                tm, final):
    if final:
        fg_ref, out_ref, n_scr, gate_scr, acc = rest
    else:
        out_ref, n_scr, gate_scr, acc = rest
    i = pl.program_id(0)
    j = pl.program_id(1)

    @pl.when(j == 0)
    def _():
        g = g_ref[...]
        prev = jnp.where(i > 0, halo_ref[...], 0.0)
        n_scr[0:HALO] = _rms(prev, g, NORM_EPS).astype(BF16)
        n_scr[HALO:] = _rms(h_ref[...], g, NORM_EPS).astype(BF16)
        acc[...] = jnp.zeros_like(acc)

    gate_scr[...] = jnp.dot(n_scr[...], wg_ref[...], preferred_element_type=F32)
    up = jnp.dot(n_scr[HALO:], wu_ref[...], preferred_element_type=F32)
    conv = cb_ref[...]
    for t in range(CONV_WIDTH):
        start = HALO - (CONV_WIDTH - 1) + t
        conv = conv + gate_scr[start:start + tm] * cw_ref[t:t + 1, :]
    act = conv * (1.0 / (1.0 + jnp.exp(-conv))) * up
    acc[...] += jnp.dot(act.astype(BF16), wd_ref[...], preferred_element_type=F32)

    @pl.when(j == pl.num_programs(1) - 1)
    def _():
        y = h_ref[...] + acc[...]
        if final:
            y = _rms(y, fg_ref[...], NORM_EPS)
        out_ref[...] = y


def _ffn(h, g, w_up, conv_w, conv_b, w_down, final_g, *, tm, tf):
    s, d = h.shape
    d_ff = w_down.shape[0]
    nf = d_ff // tf
    final = final_g is not None
    in_specs = [
        pl.BlockSpec((tm, d), lambda i, j: (i, 0)),
        pl.BlockSpec((HALO, d), lambda i, j: (jnp.maximum(i * (tm // HALO) - 1, 0), 0)),
        pl.BlockSpec((1, d), lambda i, j: (0, 0)),
        pl.BlockSpec((d, tf), lambda i, j: (0, j)),
        pl.BlockSpec((d, tf), lambda i, j: (0, nf + j)),
        pl.BlockSpec((CONV_WIDTH, tf), lambda i, j: (0, j)),
        pl.BlockSpec((1, tf), lambda i, j: (0, j)),
        pl.BlockSpec((tf, d), lambda i, j: (j, 0)),
    ]
    args = [h, h, g, w_up, w_up, conv_w, conv_b, w_down]
    if final:
        in_specs.append(pl.BlockSpec((1, d), lambda i, j: (0, 0)))
        args.append(final_g)
    return pl.pallas_call(
        functools.partial(_ffn_kernel, tm=tm, final=final),
        grid=(s // tm, nf),
        in_specs=in_specs,
        out_specs=pl.BlockSpec((tm, d), lambda i, j: (i, 0)),
        out_shape=jax.ShapeDtypeStruct((s, d), F32),
        scratch_shapes=[
            pltpu.VMEM((tm + HALO, d), BF16),
            pltpu.VMEM((tm + HALO, tf), F32),
            pltpu.VMEM((tm, d), F32),
        ],
        compiler_params=_params(("parallel", "arbitrary")),
        name="ffn_final" if final else "ffn",
    )(*args)


def _rope_tables(s):
    inv = 1.0 / (ROPE_THETA ** (jnp.arange(0, HEAD_DIM, 2, dtype=F32) / HEAD_DIM))
    ang = jnp.arange(s, dtype=F32)[:, None] * inv[None, :]
    cos, sin = jnp.cos(ang), jnp.sin(ang)
    sign = jnp.where((jnp.arange(LANES) & 32) == 0, -1.0, 1.0).astype(F32)
    cos_l = jnp.tile(cos, (1, LANES // 32))
    sin_signed = jnp.tile(sin, (1, LANES // 32)) * sign[None, :]
    return cos_l, sin_signed, cos.T, sin.T


def kernel(x, attn_norm, ffn_norm, final_norm, hyb_w_in, hyb_b_f, hyb_w_out, diff_w_qkv,
           diff_lambda, diff_subln, diff_w_out, ffn_w_up, ffn_conv_w, ffn_conv_b, ffn_w_down):
    b, s, d = x.shape
    assert b == 1
    depth = attn_norm.shape[0]
    width = hyb_w_out.shape[1] // 2
    n_pairs = width // LANES
    n_heads_b = width // HEAD_DIM
    n_diff_heads = diff_w_out.shape[1] // LANES
    dq = n_diff_heads * LANES
    tm = min(1024, s)
    tq_flash, tk_flash = min(1024, s), min(512, s)
    gps = 2
    q_scale = HEAD_DIM ** -0.5
    cos, sin_signed, cos_t, sin_t = _rope_tables(s)
    h = x[0]

    for l in range(depth):
        g_attn = attn_norm[l][None, :]
        if l % 2 == 0:
            e = l // 2
            w_in = hyb_w_in[e]
            qa_ka_va, qb, kb, vb, wf = (w_in[:, :3 * width], w_in[:, 3 * width:4 * width],
                                        w_in[:, 4 * width:5 * width], w_in[:, 5 * width:6 * width],
                                        w_in[:, 6 * width:])
            proj = _norm_proj(h, g_attn, jnp.concatenate([qa_ka_va, kb], axis=1).astype(BF16),
                              cos, sin_signed, rope_tiles=(0, 1), scale_tiles=(0,),
                              scale=q_scale, tm=tm, tn=width)
            proj_t = _norm_proj_t(h, g_attn, jnp.concatenate([qb, vb], axis=1).T.astype(BF16),
                                  cos_t, sin_t, rope_tiles=(), scale_tiles=(0,),
                                  scale=q_scale * LOG2E, tm=tm, tn=width)
            wf_pad = jnp.pad(wf, ((0, 0), (0, LANES - n_heads_b))).astype(BF16)
            bf_pad = jnp.pad(hyb_b_f[e], (0, LANES - n_heads_b))[None, :]
            cfeat = _fox_gate(h, g_attn, wf_pad, bf_pad, tc=min(512, s), n_heads=n_heads_b)
            pats = [_dilated_pattern(proj, dil, width=width, tq=min(512, s // dil))
                    for _, dil in DILATED_PATTERNS]
            ob = _flash_pair(proj_t, proj, (cfeat,), mode="fox", n_groups=n_pairs, gps=gps,
                             q_row=0, v_row=n_pairs // gps, k_col=3 * n_pairs // gps,
                             tq=tq_flash, tk=tk_flash)
            h = _hyb_out(pats, ob, hyb_w_out[e].astype(BF16), h, tm=min(512, s))
        else:
            o = l // 2
            w = diff_w_qkv[o]
            wq, wk, wv = w[:, :dq], w[:, dq:2 * dq], w[:, 2 * dq:]
            k_tiles = dq // 512
            keys = _norm_proj(h, g_attn, wk.astype(BF16), cos, sin_signed,
                              rope_tiles=tuple(range(k_tiles)), scale_tiles=(),
                              scale=1.0, tm=tm, tn=512)
            proj_t = _norm_proj_t(h, g_attn, jnp.concatenate([wq, wv], axis=1).T.astype(BF16),
                                  cos_t, sin_t, rope_tiles=tuple(range(k_tiles)),
                                  scale_tiles=tuple(range(k_tiles)), scale=q_scale * LOG2E,
                                  tm=tm, tn=512)
            lam_init = 0.8 - 0.6 * math.exp(-0.3 * l)
            att = _flash_pair(proj_t, keys, (diff_lambda[o], diff_subln[o][:, None]),
                              mode="diff", n_groups=n_diff_heads, gps=gps, q_row=0,
                              v_row=n_diff_heads // gps, k_col=0, tq=tq_flash, tk=tk_flash,
                              lam_init=lam_init)
            h = _proj_res(att, diff_w_out[o].astype(BF16), h, tm=min(512, s))
        h = _ffn(h, ffn_norm[l][None, :], ffn_w_up[l].astype(BF16), ffn_conv_w[l],
                 ffn_conv_b[l][None, :], ffn_w_down[l].astype(BF16),
                 final_norm[None, :] if l == depth - 1 else None, tm=tm, tf=256)
    return h[None]
```

```python
import functools
import math

import jax
import jax.numpy as jnp
from jax import lax
from jax.experimental import pallas as pl
from jax.experimental.pallas import tpu as pltpu

F32 = jnp.float32
BF16 = jnp.bfloat16

HEAD_DIM = 64
LANES = 128
BF16_SUBLANES = 16
ROPE_THETA = 10000.0
DILATED_PATTERNS = ((128, 1), (512, 4), (2048, 16))
BAND = 128
NORM_EPS = 1e-6
SUBLN_EPS = 1e-5
CONV_WIDTH = 3
NEG = -1e30
VMEM_LIMIT = 48 * 1024 * 1024

NT_DIMS = (((1,), (1,)), ((), ()))


def _params(sem):
    return pltpu.CompilerParams(dimension_semantics=sem, vmem_limit_bytes=VMEM_LIMIT)


def _rms(x, g, eps):
    return x * lax.rsqrt(jnp.mean(x * x, axis=-1, keepdims=True) + eps) * g


def _lane_lo(rows):
    return lax.broadcasted_iota(jnp.int32, (rows, LANES), 1) < HEAD_DIM


def _rope_tile(x, cos, sin_signed):
    rows = x.shape[0]
    first_half = (lax.broadcasted_iota(jnp.int32, (rows, LANES), 1) & 32) == 0
    out = []
    for c in range(x.shape[1] // LANES):
        xc = x[:, c * LANES:(c + 1) * LANES]
        ahead = pltpu.roll(xc, LANES - 32, 1)
        behind = pltpu.roll(xc, 32, 1)
        rot = jnp.where(first_half, ahead, behind)
        out.append(xc * cos + rot * sin_signed)
    return jnp.concatenate(out, axis=1)


def _norm_proj_kernel(h_ref, g_ref, w_ref, cos_ref, sin_ref, o_ref, *,
                      rope_tiles, scale_tiles, scale, tn):
    n = _rms(h_ref[...], g_ref[...], NORM_EPS).astype(BF16)
    for j in range(w_ref.shape[1] // tn):
        cols = slice(j * tn, (j + 1) * tn)
        acc = jnp.dot(n, w_ref[:, cols], preferred_element_type=F32)
        if j in rope_tiles:
            acc = _rope_tile(acc, cos_ref[...], sin_ref[...])
        if j in scale_tiles:
            acc = acc * scale
        o_ref[:, cols] = acc.astype(BF16)


def _norm_proj(h, g, w, cos, sin_signed, *, rope_tiles, scale_tiles, scale, tm, tn):
    s, d = h.shape
    n = w.shape[1]
    kern = functools.partial(_norm_proj_kernel, rope_tiles=rope_tiles,
                             scale_tiles=scale_tiles, scale=scale, tn=tn)
    return pl.pallas_call(
        kern,
        grid=(s // tm,),
        in_specs=[
            pl.BlockSpec((tm, d), lambda i: (i, 0)),
            pl.BlockSpec((1, d), lambda i: (0, 0)),
            pl.BlockSpec((d, n), lambda i: (0, 0)),
            pl.BlockSpec((tm, LANES), lambda i: (i, 0)),
            pl.BlockSpec((tm, LANES), lambda i: (i, 0)),
        ],
        out_specs=pl.BlockSpec((tm, n), lambda i: (i, 0)),
        out_shape=jax.ShapeDtypeStruct((s, n), BF16),
        compiler_params=_params(("parallel",)),
        name="norm_proj",
    )(h, g, w, cos, sin_signed)


def _norm_proj_t_kernel(h_ref, g_ref, wt_ref, cos_ref, sin_ref, o_ref, *,
                        rope_tiles, scale_tiles, scale, tn):
    n = _rms(h_ref[...], g_ref[...], NORM_EPS).astype(BF16)
    half = HEAD_DIM // 2
    for j in range(wt_ref.shape[0] // tn):
        acc = lax.dot_general(wt_ref[j * tn:(j + 1) * tn, :], n, NT_DIMS,
                              preferred_element_type=F32)
        sc = scale if j in scale_tiles else 1.0
        if j in rope_tiles:
            cos, sin = cos_ref[...], sin_ref[...]
            for hd in range(tn // HEAD_DIM):
                r0 = j * tn + hd * HEAD_DIM
                x1 = acc[hd * HEAD_DIM:hd * HEAD_DIM + half]
                x2 = acc[hd * HEAD_DIM + half:(hd + 1) * HEAD_DIM]
                o_ref[r0:r0 + half] = ((x1 * cos - x2 * sin) * sc).astype(BF16)
                o_ref[r0 + half:r0 + HEAD_DIM] = ((x2 * cos + x1 * sin) * sc).astype(BF16)
        else:
            o_ref[j * tn:(j + 1) * tn] = (acc * sc).astype(BF16)


def _norm_proj_t(h, g, wt, cos_t, sin_t, *, rope_tiles, scale_tiles, scale, tm, tn):
    s, d = h.shape
    n = wt.shape[0]
    kern = functools.partial(_norm_proj_t_kernel, rope_tiles=rope_tiles,
                             scale_tiles=scale_tiles, scale=scale, tn=tn)
    return pl.pallas_call(
        kern,
        grid=(s // tm,),
        in_specs=[
            pl.BlockSpec((tm, d), lambda i: (i, 0)),
            pl.BlockSpec((1, d), lambda i: (0, 0)),
            pl.BlockSpec((n, d), lambda i: (0, 0)),
            pl.BlockSpec((HEAD_DIM // 2, tm), lambda i: (0, i)),
            pl.BlockSpec((HEAD_DIM // 2, tm), lambda i: (0, i)),
        ],
        out_specs=pl.BlockSpec((n, tm), lambda i: (0, i)),
        out_shape=jax.ShapeDtypeStruct((n, s), BF16),
        compiler_params=_params(("parallel",)),
        name="norm_proj_t",
    )(h, g, wt, cos_t, sin_t)


GATE_TERMS = 3
GATE_STRIDE = 8
LOG2E = math.log2(math.e)


def _fox_gate_kernel(h_ref, g_ref, wf_ref, bf_ref, cf_ref, carry, *, tc, n_heads):
    i = pl.program_id(0)

    @pl.when(i == 0)
    def _():
        carry[...] = jnp.zeros_like(carry)

    n = _rms(h_ref[...], g_ref[...], NORM_EPS).astype(BF16)
    z = jnp.dot(n, wf_ref[...], preferred_element_type=F32) + bf_ref[...]
    logf = jnp.minimum(z, 0.0) - jnp.log(1.0 + jnp.exp(-jnp.abs(z)))
    dst = lax.broadcasted_iota(jnp.int32, (tc, tc), 0)
    src = lax.broadcasted_iota(jnp.int32, (tc, tc), 1)
    prefix = jnp.where(src <= dst, 1.0, 0.0).astype(F32)
    cs = jnp.dot(prefix, logf, precision=lax.Precision.HIGHEST,
                 preferred_element_type=F32) + carry[0:1, :]
    carry[...] = jnp.broadcast_to(cs[tc - 1:tc, :], carry.shape)
    lane = lax.broadcasted_iota(jnp.int32, (tc, LANES), 1)
    rem = jnp.where(lane < n_heads, cs * LOG2E, 0.0)
    feat = jnp.zeros_like(rem)
    for term in range(GATE_TERMS):
        part = rem.astype(BF16).astype(F32)
        rem = rem - part
        feat = feat + (pltpu.roll(part, term * GATE_STRIDE, 1) if term else part)
    cf_ref[...] = feat.astype(BF16)


def _fox_gate(h, g, wf, b_f, *, tc, n_heads):
    s, d = h.shape
    return pl.pallas_call(
        functools.partial(_fox_gate_kernel, tc=tc, n_heads=n_heads),
        grid=(s // tc,),
        in_specs=[
            pl.BlockSpec((tc, d), lambda i: (i, 0)),
            pl.BlockSpec((1, d), lambda i: (0, 0)),
            pl.BlockSpec((d, LANES), lambda i: (0, 0)),
            pl.BlockSpec((1, LANES), lambda i: (0, 0)),
        ],
        out_specs=pl.BlockSpec((tc, LANES), lambda i: (i, 0)),
        out_shape=jax.ShapeDtypeStruct((s, LANES), BF16),
        scratch_shapes=[pltpu.VMEM((8, LANES), F32)],
        compiler_params=_params(("arbitrary",)),
        name="fox_gate",
    )(h, g, wf, b_f)


def _dilated_kernel(q_ref, kc_ref, kh_ref, vc_ref, vh_ref, o_ref, l_ref, kbuf, vbuf, *, tq):
    i = pl.program_id(1)
    kbuf[0:BAND] = kh_ref[...]
    kbuf[BAND:] = kc_ref[...]
    vbuf[0:BAND] = vh_ref[...]
    vbuf[BAND:] = vc_ref[...]
    row = lax.broadcasted_iota(jnp.int32, (BAND, 2 * BAND), 0)
    col = lax.broadcasted_iota(jnp.int32, (BAND, 2 * BAND), 1)
    delta = row - col + BAND
    band = (delta >= 0) & (delta <= BAND)
    lo = _lane_lo(BAND)
    for a in range(tq // BAND):
        first_key = i * tq + (a - 1) * BAND
        valid = band & (col + first_key >= 0)
        rows = slice(a * BAND, (a + 1) * BAND)
        for hp in range(q_ref.shape[1] // LANES):
            lanes = slice(hp * LANES, (hp + 1) * LANES)
            q = q_ref[rows, lanes]
            kk = kbuf[a * BAND:(a + 2) * BAND, lanes]
            vv = vbuf[a * BAND:(a + 2) * BAND, lanes]
            outs, lses = [], []
            for qh in (jnp.where(lo, q, jnp.zeros_like(q)), jnp.where(lo, jnp.zeros_like(q), q)):
                s = lax.dot_general(qh, kk, NT_DIMS, preferred_element_type=F32)
                s = jnp.where(valid, s, NEG)
                m = jnp.max(s, axis=1, keepdims=True)
                p = jnp.exp(s - m)
                den = jnp.sum(p, axis=1, keepdims=True)
                pv = jnp.dot(p.astype(BF16), vv, preferred_element_type=F32)
                outs.append(pv / den)
                lses.append(jnp.broadcast_to(m + jnp.log(den), (BAND, LANES)))
            o_ref[rows, lanes] = jnp.where(lo, outs[0], outs[1])
            l_ref[rows, lanes] = jnp.where(lo, lses[0], lses[1])


def _dilated_pattern(proj, dil, *, width, tq):
    s, n = proj.shape
    L = s // dil
    view = proj.reshape(L, dil * n)
    per_res = n // width
    halo_per_tile = tq // BAND
    cur = lambda part: pl.BlockSpec((tq, width), lambda r, i: (i, r * per_res + part))
    halo = lambda part: pl.BlockSpec(
        (BAND, width), lambda r, i: (jnp.maximum(i * halo_per_tile - 1, 0), r * per_res + part))
    out_spec = pl.BlockSpec((tq, width), lambda r, i: (i, r))
    o, lse = pl.pallas_call(
        functools.partial(_dilated_kernel, tq=tq),
        grid=(dil, L // tq),
        in_specs=[cur(0), cur(1), halo(1), cur(2), halo(2)],
        out_specs=[out_spec, out_spec],
        out_shape=[jax.ShapeDtypeStruct((L, dil * width), F32)] * 2,
        scratch_shapes=[pltpu.VMEM((tq + BAND, width), BF16)] * 2,
        compiler_params=_params(("parallel", "arbitrary")),
        name=f"dilated_d{dil}",
    )(view, view, view, view, view)
    return o.reshape(s, width), lse.reshape(s, width)


ONES_ROWS = BF16_SUBLANES


COL_TILE = 256
STRIPS_PER_TRIP = 4


def _flash_pair_kernel(qi_ref, kj_ref, qt_ref, k_ref, *rest, mode, tq, tk, gps, lam_init):
    if mode == "fox":
        cf_ref, vt_ref, o_ref, wq, m_scr, acc, kx_scr, vx_scr, *bufs = rest
    else:
        vt_ref, lam_ref, g_ref, o_ref, wq, m_scr, acc, kx_scr, vx_scr, *bufs = rest
    s_buf, mx_buf, p_buf, al_buf = bufs[0:2], bufs[2:4], bufs[4:6], bufs[6:8]
    U = STRIPS_PER_TRIP
    step = pl.program_id(1)
    qi = qi_ref[step]
    kj = kj_ref[step]
    last_kj = (qi * tq + tq - 1) // tk
    rows = acc.shape[1]
    dv = rows - ONES_ROWS
    n_ct = tq // COL_TILE
    n_strips = 2 * gps * n_ct
    n_trips = n_strips // U
    par = kj % 2

    @pl.when(jnp.logical_and(pl.program_id(0) == 0, step == 0))
    def _():
        for buf in bufs:
            buf[...] = jnp.zeros_like(buf)
        vx_scr[...] = jnp.zeros_like(vx_scr)

    @pl.when(kj == 0)
    def _():
        r = lax.broadcasted_iota(jnp.int32, (LANES, COL_TILE), 0)
        first = r < HEAD_DIM
        for gi in range(gps):
            for a in range(2):
                keep_q = first if a == 0 else jnp.logical_not(first)
                if mode == "fox":
                    head = 2 * (pl.program_id(0) * gps + gi) + a
                    pick = (r - head) == 0
                    for term in range(1, GATE_TERMS):
                        pick = pick | ((r - head) == term * GATE_STRIDE)
                    gate_rows = jnp.where(pick, -1.0, 0.0).astype(BF16)
                for ct in range(n_ct):
                    t = (2 * gi + a) * n_ct + ct
                    qt = qt_ref[gi * LANES:(gi + 1) * LANES, ct * COL_TILE:(ct + 1) * COL_TILE]
                    wq[t, 0:LANES] = jnp.where(keep_q, qt, jnp.zeros_like(qt))
                    if mode == "fox":
                        wq[t, LANES:2 * LANES] = gate_rows
        m_scr[...] = jnp.full_like(m_scr, NEG)
        acc[...] = jnp.zeros_like(acc)

    ones = jnp.ones((ONES_ROWS, tk), BF16)
    for gi in range(gps):
        kx_scr[gi, :, 0:LANES] = k_ref[:, gi * LANES:(gi + 1) * LANES]
        if mode == "fox":
            kx_scr[gi, :, LANES:2 * LANES] = cf_ref[...]
        for a in range(2):
            if mode == "fox":
                v0 = gi * LANES + a * HEAD_DIM
                vx_scr[par, 2 * gi + a, 0:dv] = vt_ref[v0:v0 + HEAD_DIM, :]
            else:
                vx_scr[par, 2 * gi + a, 0:dv] = vt_ref[gi * LANES:(gi + 1) * LANES, :]
            vx_scr[par, 2 * gi + a, dv:rows] = ones

    def stage_a(j, slot, u, masked):
        s = jnp.dot(kx_scr[j // (2 * n_ct)], wq[j], preferred_element_type=F32)
        if masked:
            key = kj * tk + lax.broadcasted_iota(jnp.int32, (tk, COL_TILE), 0)
            qry = (qi * tq + (j % n_ct) * COL_TILE
                   + lax.broadcasted_iota(jnp.int32, (tk, COL_TILE), 1))
            s = jnp.where(key <= qry, s, NEG)
        s_buf[slot][u] = s
        mx_buf[slot][u] = jnp.broadcast_to(jnp.max(s, axis=0, keepdims=True), (8, COL_TILE))


    def stage_b(j0, slot, valid, us):
        m_prevs = {u: m_scr[j0 + u] for u in us}
        for u, m_prev in list(m_prevs.items()):
            m_next = jnp.maximum(m_prev, mx_buf[slot][u])
            if valid is not None:
                m_next = jnp.where(valid, m_next, m_prev)
            al_buf[slot][u] = jnp.exp2(m_prev - m_next)
            s3 = s_buf[slot][u].reshape(tk // 8, 8, COL_TILE)
            p_buf[slot][u] = jnp.exp2(s3 - m_next[None]).reshape(tk, COL_TILE).astype(BF16)
            m_prevs[u] = m_next
        for u, m_next in m_prevs.items():
            m_scr[j0 + u] = m_next

    def stage_c(j0, slot, vpar, valid, us):
        pvs = [jnp.dot(vx_scr[vpar, (j0 + u) // n_ct], p_buf[slot][u],
                       preferred_element_type=F32) for u in us]
        olds = [acc[j0 + u] for u in us]
        for u, old, pv in zip(us, olds, pvs):
            new = ((old.reshape(rows // 8, 8, COL_TILE) * al_buf[slot][u][None])
                   .reshape(rows, COL_TILE) + pv)
            acc[j0 + u] = new if valid is None else jnp.where(valid, new, old)

    def sweep(masked):
        def trip(k, slot):
            later_step = kj > 0
            kc = (k + n_trips - 2) % n_trips
            kb = (k + n_trips - 1) % n_trips
            valid_c = jnp.logical_or(later_step, k >= 2)
            valid_b = jnp.logical_or(later_step, k >= 1)
            vpar = jnp.where(k >= 2, par, 1 - par)
            for u in range(U):
                stage_a(k * U + u, slot, u, masked)
            stage_b(kb * U, 1 - slot, valid_b, range(U))
            stage_c(kc * U, slot, vpar, valid_c, range(U))

        def trip_pair(half, carry):
            trip(2 * half, 0)
            trip(2 * half + 1, 1)
            return carry

        if n_trips == 2:
            trip_pair(0, 0)
        else:
            lax.fori_loop(0, n_trips // 2, trip_pair, 0)

    needs_mask = kj * tk + tk - 1 > qi * tq
    pl.when(needs_mask)(lambda: sweep(True))
    pl.when(jnp.logical_not(needs_mask))(lambda: sweep(False))

    @pl.when(kj == last_kj)
    def _():
        last = n_trips - 1
        stage_c((last - 1) * U, (last - 1) % 2, par, None, range(U))
        stage_b(last * U, last % 2, None, range(U))
        stage_c(last * U, last % 2, par, None, range(U))

        def map_out(mi):
            parts = [acc[mi * n_ct + ct] for ct in range(n_ct)]
            full = jnp.concatenate(parts, axis=1)
            return full[0:dv] / full[dv:dv + 1]

        for gi in range(gps):
            o0, o1 = map_out(2 * gi), map_out(2 * gi + 1)
            if mode == "fox":
                o = jnp.concatenate([o0, o1], axis=0)
            else:
                lp = lam_ref[...]
                t1 = jnp.sum(lp[0:1] * lp[1:2], axis=1, keepdims=True)
                t2 = jnp.sum(lp[2:3] * lp[3:4], axis=1, keepdims=True)
                lam = jnp.exp(t1) - jnp.exp(t2) + lam_init
                o = o0 - lam * o1
                ms = jnp.mean(o * o, axis=0, keepdims=True)
                o = o * lax.rsqrt(ms + SUBLN_EPS) * g_ref[...] * (1.0 - lam_init)
            o_ref[:, gi * LANES:(gi + 1) * LANES] = o.T.astype(BF16)


def _causal_steps(s, tq, tk):
    qi, kj = [], []
    for i in range(s // tq):
        for j in range((i * tq + tq - 1) // tk + 1):
            qi.append(i)
            kj.append(j)
    return jnp.asarray(qi, jnp.int32), jnp.asarray(kj, jnp.int32)


def _flash_pair(proj_t, keys, extras, *, mode, n_groups, gps, q_row, v_row, k_col, tq, tk,
                lam_init=0.0):
    s = keys.shape[0]
    qi, kj = _causal_steps(s, tq, tk)
    gw = gps * LANES
    qt_spec = pl.BlockSpec((gw, tq), lambda g, t, qi, kj: (q_row + g, qi[t]))
    k_spec = pl.BlockSpec((tk, gw), lambda g, t, qi, kj: (kj[t], k_col + g))
    vt_spec = pl.BlockSpec((gw, tk), lambda g, t, qi, kj: (v_row + g, kj[t]))
    const = lambda x: pl.BlockSpec(x.shape, lambda g, t, qi, kj: (0, 0))
    if mode == "fox":
        (cfeat,) = extras
        in_specs = [qt_spec, k_spec,
                    pl.BlockSpec((tk, LANES), lambda g, t, qi, kj: (kj[t], 0)), vt_spec]
        args = (proj_t, keys, cfeat, proj_t)
        kd, dv = 2 * LANES, HEAD_DIM
    else:
        lam_params, subln_g = extras
        in_specs = [qt_spec, k_spec, vt_spec, const(lam_params), const(subln_g)]
        args = (proj_t, keys, proj_t, lam_params, subln_g)
        kd, dv = LANES, LANES
    rows = dv + ONES_ROWS
    n_strips = 2 * gps * (tq // COL_TILE)
    u = STRIPS_PER_TRIP
    assert n_strips % (2 * u) == 0
    kern = functools.partial(_flash_pair_kernel, mode=mode, tq=tq, tk=tk, gps=gps,
                             lam_init=lam_init)
    return pl.pallas_call(
        kern,
        grid_spec=pltpu.PrefetchScalarGridSpec(
            num_scalar_prefetch=2,
            grid=(n_groups // gps, qi.shape[0]),
            in_specs=in_specs,
            out_specs=pl.BlockSpec((tq, gw), lambda g, t, qi, kj: (qi[t], g)),
            scratch_shapes=[
                pltpu.VMEM((n_strips, kd, COL_TILE), BF16),
                pltpu.VMEM((n_strips, 8, COL_TILE), F32),
                pltpu.VMEM((n_strips, rows, COL_TILE), F32),
                pltpu.VMEM((gps, tk, kd), BF16),
                pltpu.VMEM((2, 2 * gps, rows, tk), BF16),
            ] + [pltpu.VMEM((u, tk, COL_TILE), F32)] * 2
              + [pltpu.VMEM((u, 8, COL_TILE), F32)] * 2
              + [pltpu.VMEM((u, tk, COL_TILE), BF16)] * 2
              + [pltpu.VMEM((u, 8, COL_TILE), F32)] * 2,
        ),
        out_shape=jax.ShapeDtypeStruct((s, n_groups * LANES), BF16),
        compiler_params=_params(("arbitrary", "arbitrary")),
        name=f"flash_{mode}",
    )(qi, kj, *args)


def _hyb_out_kernel(o1, o2, o3, l1, l2, l3, ob_ref, w_ref, h_ref, out_ref):
    a1, a2, a3 = l1[...], l2[...], l3[...]
    m = jnp.maximum(jnp.maximum(a1, a2), a3)
    e1, e2, e3 = jnp.exp(a1 - m), jnp.exp(a2 - m), jnp.exp(a3 - m)
    oa = (e1 * o1[...] + e2 * o2[...] + e3 * o3[...]) / (e1 + e2 + e3)
    wa = oa.shape[1]
    acc = jnp.dot(oa.astype(BF16), w_ref[0:wa, :], preferred_element_type=F32)
    acc = acc + jnp.dot(ob_ref[...], w_ref[wa:, :], preferred_element_type=F32)
    out_ref[...] = h_ref[...] + acc


def _hyb_out(pattern_outs, ob, w, h, *, tm):
    s, d = h.shape
    wa = pattern_outs[0][0].shape[1]
    part = pl.BlockSpec((tm, wa), lambda i: (i, 0))
    os_ = [o for o, _ in pattern_outs]
    ls_ = [l for _, l in pattern_outs]
    return pl.pallas_call(
        _hyb_out_kernel,
        grid=(s // tm,),
        in_specs=[part] * 6 + [
            pl.BlockSpec((tm, ob.shape[1]), lambda i: (i, 0)),
            pl.BlockSpec(w.shape, lambda i: (0, 0)),
            pl.BlockSpec((tm, d), lambda i: (i, 0)),
        ],
        out_specs=pl.BlockSpec((tm, d), lambda i: (i, 0)),
        out_shape=jax.ShapeDtypeStruct((s, d), F32),
        compiler_params=_params(("parallel",)),
        name="hyb_out",
    )(*os_, *ls_, ob, w, h)


def _proj_res_kernel(a_ref, w_ref, h_ref, out_ref):
    out_ref[...] = h_ref[...] + jnp.dot(a_ref[...], w_ref[...], preferred_element_type=F32)


def _proj_res(a, w, h, *, tm):
    s, d = h.shape
    return pl.pallas_call(
        _proj_res_kernel,
        grid=(s // tm,),
        in_specs=[
            pl.BlockSpec((tm, a.shape[1]), lambda i: (i, 0)),
            pl.BlockSpec(w.shape, lambda i: (0, 0)),
            pl.BlockSpec((tm, d), lambda i: (i, 0)),
        ],
        out_specs=pl.BlockSpec((tm, d), lambda i: (i, 0)),
        out_shape=jax.ShapeDtypeStruct((s, d), F32),
        compiler_params=_params(("parallel",)),
        name="proj_res",
    )(a, w, h)


HALO = BF16_SUBLANES


def _ffn_kernel(h_ref, halo_ref, g_ref, wg_ref, wu_ref, cw_ref, cb_ref, wd_ref, *rest,
                tm, final):
    if final:
        fg_ref, out_ref, n_scr, gate_scr, acc = rest
    else:
        out_ref, n_scr, gate_scr, acc = rest
    i = pl.program_id(0)
    j = pl.program_id(1)

    @pl.when(j == 0)
    def _():
        g = g_ref[...]
        prev = jnp.where(i > 0, halo_ref[...], 0.0)
        n_scr[0:HALO] = _rms(prev, g, NORM_EPS).astype(BF16)
        n_scr[HALO:] = _rms(h_ref[...], g, NORM_EPS).astype(BF16)
        acc[...] = jnp.zeros_like(acc)

    gate_scr[...] = jnp.dot(n_scr[...], wg_ref[...], preferred_element_type=F32)
    up = jnp.dot(n_scr[HALO:], wu_ref[...], preferred_element_type=F32)
    conv = cb_ref[...]
    for t in range(CONV_WIDTH):
        start = HALO - (CONV_WIDTH - 1) + t
        conv = conv + gate_scr[start:start + tm] * cw_ref[t:t + 1, :]
    act = conv * (1.0 / (1.0 + jnp.exp(-conv))) * up
    acc[...] += jnp.dot(act.astype(BF16), wd_ref[...], preferred_element_type=F32)

    @pl.when(j == pl.num_programs(1) - 1)
    def _():
        y = h_ref[...] + acc[...]
        if final:
            y = _rms(y, fg_ref[...], NORM_EPS)
        out_ref[...] = y


def _ffn(h, g, w_up, conv_w, conv_b, w_down, final_g, *, tm, tf):
    s, d = h.shape
    d_ff = w_down.shape[0]
    nf = d_ff // tf
    final = final_g is not None
    in_specs = [
        pl.BlockSpec((tm, d), lambda i, j: (i, 0)),
        pl.BlockSpec((HALO, d), lambda i, j: (jnp.maximum(i * (tm // HALO) - 1, 0), 0)),
        pl.BlockSpec((1, d), lambda i, j: (0, 0)),
        pl.BlockSpec((d, tf), lambda i, j: (0, j)),
        pl.BlockSpec((d, tf), lambda i, j: (0, nf + j)),
        pl.BlockSpec((CONV_WIDTH, tf), lambda i, j: (0, j)),
        pl.BlockSpec((1, tf), lambda i, j: (0, j)),
        pl.BlockSpec((tf, d), lambda i, j: (j, 0)),
    ]
    args = [h, h, g, w_up, w_up, conv_w, conv_b, w_down]
    if final:
        in_specs.append(pl.BlockSpec((1, d), lambda i, j: (0, 0)))
        args.append(final_g)
    return pl.pallas_call(
        functools.partial(_ffn_kernel, tm=tm, final=final),
        grid=(s // tm, nf),
        in_specs=in_specs,
        out_specs=pl.BlockSpec((tm, d), lambda i, j: (i, 0)),
        out_shape=jax.ShapeDtypeStruct((s, d), F32),
        scratch_shapes=[
            pltpu.VMEM((tm + HALO, d), BF16),
            pltpu.VMEM((tm + HALO, tf), F32),
            pltpu.VMEM((tm, d), F32),
        ],
        compiler_params=_params(("parallel", "arbitrary")),
        name="ffn_final" if final else "ffn",
    )(*args)


def _rope_tables(s):
    inv = 1.0 / (ROPE_THETA ** (jnp.arange(0, HEAD_DIM, 2, dtype=F32) / HEAD_DIM))
    ang = jnp.arange(s, dtype=F32)[:, None] * inv[None, :]
    cos, sin = jnp.cos(ang), jnp.sin(ang)
    sign = jnp.where((jnp.arange(LANES) & 32) == 0, -1.0, 1.0).astype(F32)
    cos_l = jnp.tile(cos, (1, LANES // 32))
    sin_signed = jnp.tile(sin, (1, LANES // 32)) * sign[None, :]
    return cos_l, sin_signed, cos.T, sin.T


def kernel(x, attn_norm, ffn_norm, final_norm, hyb_w_in, hyb_b_f, hyb_w_out, diff_w_qkv,
           diff_lambda, diff_subln, diff_w_out, ffn_w_up, ffn_conv_w, ffn_conv_b, ffn_w_down):
    b, s, d = x.shape
    assert b == 1
    depth = attn_norm.shape[0]
    width = hyb_w_out.shape[1] // 2
    n_pairs = width // LANES
    n_heads_b = width // HEAD_DIM
    n_diff_heads = diff_w_out.shape[1] // LANES
    dq = n_diff_heads * LANES
    tm = min(1024, s)
    tq_flash, tk_flash = min(1024, s), min(512, s)
    gps = 2
    q_scale = HEAD_DIM ** -0.5
    cos, sin_signed, cos_t, sin_t = _rope_tables(s)
    h = x[0]

    for l in range(depth):
        g_attn = attn_norm[l][None, :]
        if l % 2 == 0:
            e = l // 2
            w_in = hyb_w_in[e]
            qa_ka_va, qb, kb, vb, wf = (w_in[:, :3 * width], w_in[:, 3 * width:4 * width],
                                        w_in[:, 4 * width:5 * width], w_in[:, 5 * width:6 * width],
                                        w_in[:, 6 * width:])
            proj = _norm_proj(h, g_attn, jnp.concatenate([qa_ka_va, kb], axis=1).astype(BF16),
                              cos, sin_signed, rope_tiles=(0, 1), scale_tiles=(0,),
                              scale=q_scale, tm=tm, tn=width)
            proj_t = _norm_proj_t(h, g_attn, jnp.concatenate([qb, vb], axis=1).T.astype(BF16),
                                  cos_t, sin_t, rope_tiles=(), scale_tiles=(0,),
                                  scale=q_scale * LOG2E, tm=tm, tn=width)
            wf_pad = jnp.pad(wf, ((0, 0), (0, LANES - n_heads_b))).astype(BF16)
            bf_pad = jnp.pad(hyb_b_f[e], (0, LANES - n_heads_b))[None, :]
            cfeat = _fox_gate(h, g_attn, wf_pad, bf_pad, tc=min(512, s), n_heads=n_heads_b)
            pats = [_dilated_pattern(proj, dil, width=width, tq=min(512, s // dil))
                    for _, dil in DILATED_PATTERNS]
            ob = _flash_pair(proj_t, proj, (cfeat,), mode="fox", n_groups=n_pairs, gps=gps,
                             q_row=0, v_row=n_pairs // gps, k_col=3 * n_pairs // gps,
                             tq=tq_flash, tk=tk_flash)
            h = _hyb_out(pats, ob, hyb_w_out[e].astype(BF16), h, tm=min(512, s))
        else:
            o = l // 2
            w = diff_w_qkv[o]
            wq, wk, wv = w[:, :dq], w[:, dq:2 * dq], w[:, 2 * dq:]
            k_tiles = dq // 512
            keys = _norm_proj(h, g_attn, wk.astype(BF16), cos, sin_signed,
                              rope_tiles=tuple(range(k_tiles)), scale_tiles=(),
                              scale=1.0, tm=tm, tn=512)
            proj_t = _norm_proj_t(h, g_attn, jnp.concatenate([wq, wv], axis=1).T.astype(BF16),
                                  cos_t, sin_t, rope_tiles=tuple(range(k_tiles)),
                                  scale_tiles=tuple(range(k_tiles)), scale=q_scale * LOG2E,
                                  tm=tm, tn=512)
            lam_init = 0.8 - 0.6 * math.exp(-0.3 * l)
            att = _flash_pair(proj_t, keys, (diff_lambda[o], diff_subln[o][:, None]),
                              mode="diff", n_groups=n_diff_heads, gps=gps, q_row=0,
                              v_row=n_diff_heads // gps, k_col=0, tq=tq_flash, tk=tk_flash,
                              lam_init=lam_init)
            h = _proj_res(att, diff_w_out[o].astype(BF16), h, tm=min(512, s))
        h = _ffn(h, ffn_norm[l][None, :], ffn_w_up[l].astype(BF16), ffn_conv_w[l],
                 ffn_conv_b[l][None, :], ffn_w_down[l].astype(BF16),
                 final_norm[None, :] if l == depth - 1 else None, tm=tm, tf=256)
    return h[None]
```

```python
import functools
import math

import jax
import jax.numpy as jnp
from jax import lax
from jax.experimental import pallas as pl
from jax.experimental.pallas import tpu as pltpu

F32 = jnp.float32
BF16 = jnp.bfloat16

HEAD_DIM = 64
LANES = 128
BF16_SUBLANES = 16
ROPE_THETA = 10000.0
DILATED_PATTERNS = ((128, 1), (512, 4), (2048, 16))
BAND = 128
NORM_EPS = 1e-6
SUBLN_EPS = 1e-5
CONV_WIDTH = 3
NEG = -1e30
VMEM_LIMIT = 48 * 1024 * 1024

NT_DIMS = (((1,), (1,)), ((), ()))


def _params(sem):
    return pltpu.CompilerParams(dimension_semantics=sem, vmem_limit_bytes=VMEM_LIMIT)


def _rms(x, g, eps):
    return x * lax.rsqrt(jnp.mean(x * x, axis=-1, keepdims=True) + eps) * g


def _lane_lo(rows):
    return lax.broadcasted_iota(jnp.int32, (rows, LANES), 1) < HEAD_DIM


def _rope_tile(x, cos, sin_signed):
    rows = x.shape[0]
    first_half = (lax.broadcasted_iota(jnp.int32, (rows, LANES), 1) & 32) == 0
    out = []
    for c in range(x.shape[1] // LANES):
        xc = x[:, c * LANES:(c + 1) * LANES]
        ahead = pltpu.roll(xc, LANES - 32, 1)
        behind = pltpu.roll(xc, 32, 1)
        rot = jnp.where(first_half, ahead, behind)
        out.append(xc * cos + rot * sin_signed)
    return jnp.concatenate(out, axis=1)


def _norm_proj_kernel(h_ref, g_ref, w_ref, cos_ref, sin_ref, o_ref, *rest,
                      rope_tiles, scale_tiles, scale, tn, strided_tiles, dilations):
    n = _rms(h_ref[...], g_ref[...], NORM_EPS).astype(BF16)
    tm = h_ref.shape[0]
    for j in range(w_ref.shape[1] // tn):
        cols = slice(j * tn, (j + 1) * tn)
        acc = jnp.dot(n, w_ref[:, cols], preferred_element_type=F32)
        if j in rope_tiles:
            acc = _rope_tile(acc, cos_ref[...], sin_ref[...])
        if j in scale_tiles:
            acc = acc * scale
        o_ref[:, cols] = acc.astype(BF16)
        if j < strided_tiles:
            stage = rest[-1]
            for c in range(tn // LANES):
                stage[c] = acc[:, c * LANES:(c + 1) * LANES]
            for d_ref, dil in zip(rest[:-1], dilations):
                for r in range(dil):
                    for c in range(tn // LANES):
                        c0 = (r * strided_tiles + j) * tn + c * LANES
                        rows = stage[c, pl.ds(r, tm // dil, stride=dil), :]
                        d_ref[:, c0:c0 + LANES] = rows.astype(BF16)


def _norm_proj(h, g, w, cos, sin_signed, *, rope_tiles, scale_tiles, scale, tm, tn,
               strided_tiles=0, dilations=()):
    s, d = h.shape
    n = w.shape[1]
    kern = functools.partial(_norm_proj_kernel, rope_tiles=rope_tiles, scale_tiles=scale_tiles,
                             scale=scale, tn=tn, strided_tiles=strided_tiles,
                             dilations=dilations)
    out_specs = [pl.BlockSpec((tm, n), lambda i: (i, 0))]
    out_shape = [jax.ShapeDtypeStruct((s, n), BF16)]
    for dil in dilations:
        wide = dil * strided_tiles * tn
        out_specs.append(pl.BlockSpec((tm // dil, wide), lambda i: (i, 0)))
        out_shape.append(jax.ShapeDtypeStruct((s // dil, wide), BF16))
    outs = pl.pallas_call(
        kern,
        grid=(s // tm,),
        in_specs=[
            pl.BlockSpec((tm, d), lambda i: (i, 0)),
            pl.BlockSpec((1, d), lambda i: (0, 0)),
            pl.BlockSpec((d, n), lambda i: (0, 0)),
            pl.BlockSpec((tm, LANES), lambda i: (i, 0)),
            pl.BlockSpec((tm, LANES), lambda i: (i, 0)),
        ],
        out_specs=out_specs,
        out_shape=out_shape,
        scratch_shapes=[pltpu.VMEM((tn // LANES, tm, LANES), F32)] if dilations else [],
        compiler_params=_params(("parallel",)),
        name="norm_proj",
    )(h, g, w, cos, sin_signed)
    return outs if dilations else outs[0]


def _norm_proj_t_kernel(h_ref, g_ref, wt_ref, cos_ref, sin_ref, o_ref, *,
                        rope_tiles, scale_tiles, scale, tn):
    n = _rms(h_ref[...], g_ref[...], NORM_EPS).astype(BF16)
    half = HEAD_DIM // 2
    for j in range(wt_ref.shape[0] // tn):
        acc = lax.dot_general(wt_ref[j * tn:(j + 1) * tn, :], n, NT_DIMS,
                              preferred_element_type=F32)
        sc = scale if j in scale_tiles else 1.0
        if j in rope_tiles:
            cos, sin = cos_ref[...], sin_ref[...]
            for hd in range(tn // HEAD_DIM):
                r0 = j * tn + hd * HEAD_DIM
                x1 = acc[hd * HEAD_DIM:hd * HEAD_DIM + half]
                x2 = acc[hd * HEAD_DIM + half:(hd + 1) * HEAD_DIM]
                o_ref[r0:r0 + half] = ((x1 * cos - x2 * sin) * sc).astype(BF16)
                o_ref[r0 + half:r0 + HEAD_DIM] = ((x2 * cos + x1 * sin) * sc).astype(BF16)
        else:
            o_ref[j * tn:(j + 1) * tn] = (acc * sc).astype(BF16)


def _norm_proj_t(h, g, wt, cos_t, sin_t, *, rope_tiles, scale_tiles, scale, tm, tn):
    s, d = h.shape
    n = wt.shape[0]
    kern = functools.partial(_norm_proj_t_kernel, rope_tiles=rope_tiles,
                             scale_tiles=scale_tiles, scale=scale, tn=tn)
    return pl.pallas_call(
        kern,
        grid=(s // tm,),
        in_specs=[
            pl.BlockSpec((tm, d), lambda i: (i, 0)),
            pl.BlockSpec((1, d), lambda i: (0, 0)),
            pl.BlockSpec((n, d), lambda i: (0, 0)),
            pl.BlockSpec((HEAD_DIM // 2, tm), lambda i: (0, i)),
            pl.BlockSpec((HEAD_DIM // 2, tm), lambda i: (0, i)),
        ],
        out_specs=pl.BlockSpec((n, tm), lambda i: (0, i)),
        out_shape=jax.ShapeDtypeStruct((n, s), BF16),
        compiler_params=_params(("parallel",)),
        name="norm_proj_t",
    )(h, g, wt, cos_t, sin_t)


GATE_TERMS = 3
GATE_STRIDE = 8
LOG2E = math.log2(math.e)


def _fox_gate_kernel(h_ref, g_ref, wf_ref, bf_ref, cf_ref, carry, *, tc, n_heads):
    i = pl.program_id(0)

    @pl.when(i == 0)
    def _():
        carry[...] = jnp.zeros_like(carry)

    n = _rms(h_ref[...], g_ref[...], NORM_EPS).astype(BF16)
    z = jnp.dot(n, wf_ref[...], preferred_element_type=F32) + bf_ref[...]
    logf = jnp.minimum(z, 0.0) - jnp.log(1.0 + jnp.exp(-jnp.abs(z)))
    dst = lax.broadcasted_iota(jnp.int32, (tc, tc), 0)
    src = lax.broadcasted_iota(jnp.int32, (tc, tc), 1)
    prefix = jnp.where(src <= dst, 1.0, 0.0).astype(F32)
    cs = jnp.dot(prefix, logf, precision=lax.Precision.HIGHEST,
                 preferred_element_type=F32) + carry[0:1, :]
    carry[...] = jnp.broadcast_to(cs[tc - 1:tc, :], carry.shape)
    lane = lax.broadcasted_iota(jnp.int32, (tc, LANES), 1)
    rem = jnp.where(lane < n_heads, cs * LOG2E, 0.0)
    feat = jnp.zeros_like(rem)
    for term in range(GATE_TERMS):
        part = rem.astype(BF16).astype(F32)
        rem = rem - part
        feat = feat + (pltpu.roll(part, term * GATE_STRIDE, 1) if term else part)
    cf_ref[...] = feat.astype(BF16)


def _fox_gate(h, g, wf, b_f, *, tc, n_heads):
    s, d = h.shape
    return pl.pallas_call(
        functools.partial(_fox_gate_kernel, tc=tc, n_heads=n_heads),
        grid=(s // tc,),
        in_specs=[
            pl.BlockSpec((tc, d), lambda i: (i, 0)),
            pl.BlockSpec((1, d), lambda i: (0, 0)),
            pl.BlockSpec((d, LANES), lambda i: (0, 0)),
            pl.BlockSpec((1, LANES), lambda i: (0, 0)),
        ],
        out_specs=pl.BlockSpec((tc, LANES), lambda i: (i, 0)),
        out_shape=jax.ShapeDtypeStruct((s, LANES), BF16),
        scratch_shapes=[pltpu.VMEM((8, LANES), F32)],
        compiler_params=_params(("arbitrary",)),
        name="fox_gate",
    )(h, g, wf, b_f)


def _dilated_kernel(q_ref, kc_ref, kh_ref, vc_ref, vh_ref, o_ref, l_ref, kbuf, vbuf, *, tq):
    i = pl.program_id(1)
    kbuf[0:BAND] = kh_ref[...]
    kbuf[BAND:] = kc_ref[...]
    vbuf[0:BAND] = vh_ref[...]
    vbuf[BAND:] = vc_ref[...]
    row = lax.broadcasted_iota(jnp.int32, (BAND, 2 * BAND), 0)
    col = lax.broadcasted_iota(jnp.int32, (BAND, 2 * BAND), 1)
    delta = row - col + BAND
    band = (delta >= 0) & (delta <= BAND)
    lo = _lane_lo(BAND)
    for a in range(tq // BAND):
        first_key = i * tq + (a - 1) * BAND
        valid = band & (col + first_key >= 0)
        rows = slice(a * BAND, (a + 1) * BAND)
        for hp in range(q_ref.shape[1] // LANES):
            lanes = slice(hp * LANES, (hp + 1) * LANES)
            q = q_ref[rows, lanes]
            kk = kbuf[a * BAND:(a + 2) * BAND, lanes]
            vv = vbuf[a * BAND:(a + 2) * BAND, lanes]
            outs, lses = [], []
            for qh in (jnp.where(lo, q, jnp.zeros_like(q)), jnp.where(lo, jnp.zeros_like(q), q)):
                s = lax.dot_general(qh, kk, NT_DIMS, preferred_element_type=F32)
                s = jnp.where(valid, s, NEG)
                m = jnp.max(s, axis=1, keepdims=True)
                p = jnp.exp(s - m)
                den = jnp.sum(p, axis=1, keepdims=True)
                pv = jnp.dot(p.astype(BF16), vv, preferred_element_type=F32)
                outs.append(pv / den)
                lses.append(jnp.broadcast_to(m + jnp.log(den), (BAND, LANES)))
            o_ref[rows, lanes] = jnp.where(lo, outs[0], outs[1])
            l_ref[rows, lanes] = jnp.where(lo, lses[0], lses[1])


def _dilated_pattern(view, dil, *, width, per_res, tq):
    L = view.shape[0]
    halo_per_tile = tq // BAND
    cur = lambda part: pl.BlockSpec((tq, width), lambda r, i: (i, r * per_res + part))
    halo = lambda part: pl.BlockSpec(
        (BAND, width), lambda r, i: (jnp.maximum(i * halo_per_tile - 1, 0), r * per_res + part))
    out_spec = pl.BlockSpec((tq, width), lambda r, i: (i, r))
    return pl.pallas_call(
        functools.partial(_dilated_kernel, tq=tq),
        grid=(dil, L // tq),
        in_specs=[cur(0), cur(1), halo(1), cur(2), halo(2)],
        out_specs=[out_spec, out_spec],
        out_shape=[jax.ShapeDtypeStruct((L, dil * width), F32)] * 2,
        scratch_shapes=[pltpu.VMEM((tq + BAND, width), BF16)] * 2,
        compiler_params=_params(("parallel", "arbitrary")),
        name=f"dilated_d{dil}",
    )(view, view, view, view, view)


ONES_ROWS = BF16_SUBLANES


COL_TILE = 256
STRIPS_PER_TRIP = 4


def _flash_pair_kernel(qi_ref, kj_ref, qt_ref, k_ref, *rest, mode, tq, tk, gps, lam_init):
    if mode == "fox":
        cf_ref, vt_ref, o_ref, wq, m_scr, acc, kx_scr, vx_scr, *bufs = rest
    else:
        vt_ref, lam_ref, g_ref, o_ref, wq, m_scr, acc, kx_scr, vx_scr, *bufs = rest
    s_buf, mx_buf, p_buf, al_buf = bufs[0:2], bufs[2:4], bufs[4:6], bufs[6:8]
    U = STRIPS_PER_TRIP
    step = pl.program_id(1)
    qi = qi_ref[step]
    kj = kj_ref[step]
    last_kj = (qi * tq + tq - 1) // tk
    rows = acc.shape[1]
    dv = rows - ONES_ROWS
    n_ct = tq // COL_TILE
    n_strips = 2 * gps * n_ct
    n_trips = n_strips // U
    par = kj % 2

    @pl.when(jnp.logical_and(pl.program_id(0) == 0, step == 0))
    def _():
        for buf in bufs:
            buf[...] = jnp.zeros_like(buf)
        vx_scr[...] = jnp.zeros_like(vx_scr)

    @pl.when(kj == 0)
    def _():
        r = lax.broadcasted_iota(jnp.int32, (LANES, COL_TILE), 0)
        first = r < HEAD_DIM
        for gi in range(gps):
            for a in range(2):
                keep_q = first if a == 0 else jnp.logical_not(first)
                if mode == "fox":
                    head = 2 * (pl.program_id(0) * gps + gi) + a
                    pick = (r - head) == 0
                    for term in range(1, GATE_TERMS):
                        pick = pick | ((r - head) == term * GATE_STRIDE)
                    gate_rows = jnp.where(pick, -1.0, 0.0).astype(BF16)
                for ct in range(n_ct):
                    t = (2 * gi + a) * n_ct + ct
                    qt = qt_ref[gi * LANES:(gi + 1) * LANES, ct * COL_TILE:(ct + 1) * COL_TILE]
                    wq[t, 0:LANES] = jnp.where(keep_q, qt, jnp.zeros_like(qt))
                    if mode == "fox":
                        wq[t, LANES:2 * LANES] = gate_rows
        m_scr[...] = jnp.full_like(m_scr, NEG)
        acc[...] = jnp.zeros_like(acc)

    ones = jnp.ones((ONES_ROWS, tk), BF16)
    for gi in range(gps):
        kx_scr[gi, :, 0:LANES] = k_ref[:, gi * LANES:(gi + 1) * LANES]
        if mode == "fox":
            kx_scr[gi, :, LANES:2 * LANES] = cf_ref[...]
        for a in range(2):
            if mode == "fox":
                v0 = gi * LANES + a * HEAD_DIM
                vx_scr[par, 2 * gi + a, 0:dv] = vt_ref[v0:v0 + HEAD_DIM, :]
            else:
                vx_scr[par, 2 * gi + a, 0:dv] = vt_ref[gi * LANES:(gi + 1) * LANES, :]
            vx_scr[par, 2 * gi + a, dv:rows] = ones

    def stage_a(j, slot, u, masked):
        s = jnp.dot(kx_scr[j // (2 * n_ct)], wq[j], preferred_element_type=F32)
        if masked:
            key = kj * tk + lax.broadcasted_iota(jnp.int32, (tk, COL_TILE), 0)
            qry = (qi * tq + (j % n_ct) * COL_TILE
                   + lax.broadcasted_iota(jnp.int32, (tk, COL_TILE), 1))
            s = jnp.where(key <= qry, s, NEG)
        s_buf[slot][u] = s
        mx_buf[slot][u] = jnp.broadcast_to(jnp.max(s, axis=0, keepdims=True), (8, COL_TILE))


    def stage_b(j0, slot, valid, us):
        m_prevs = {u: m_scr[j0 + u] for u in us}
        for u, m_prev in list(m_prevs.items()):
            m_next = jnp.maximum(m_prev, mx_buf[slot][u])
            if valid is not None:
                m_next = jnp.where(valid, m_next, m_prev)
            al_buf[slot][u] = jnp.exp2(m_prev - m_next)
            s3 = s_buf[slot][u].reshape(tk // 8, 8, COL_TILE)
            p_buf[slot][u] = jnp.exp2(s3 - m_next[None]).reshape(tk, COL_TILE).astype(BF16)
            m_prevs[u] = m_next
        for u, m_next in m_prevs.items():
            m_scr[j0 + u] = m_next

    def stage_c(j0, slot, vpar, valid, us):
        pvs = [jnp.dot(vx_scr[vpar, (j0 + u) // n_ct], p_buf[slot][u],
                       preferred_element_type=F32) for u in us]
        olds = [acc[j0 + u] for u in us]
        for u, old, pv in zip(us, olds, pvs):
            new = ((old.reshape(rows // 8, 8, COL_TILE) * al_buf[slot][u][None])
                   .reshape(rows, COL_TILE) + pv)
            acc[j0 + u] = new if valid is None else jnp.where(valid, new, old)

    def sweep(masked):
        def trip(k, slot):
            later_step = kj > 0
            kc = (k + n_trips - 2) % n_trips
            kb = (k + n_trips - 1) % n_trips
            valid_c = jnp.logical_or(later_step, k >= 2)
            valid_b = jnp.logical_or(later_step, k >= 1)
            vpar = jnp.where(k >= 2, par, 1 - par)
            for u in range(U):
                stage_a(k * U + u, slot, u, masked)
            stage_b(kb * U, 1 - slot, valid_b, range(U))
            stage_c(kc * U, slot, vpar, valid_c, range(U))

        def trip_pair(half, carry):
            trip(2 * half, 0)
            trip(2 * half + 1, 1)
            return carry

        if n_trips == 2:
            trip_pair(0, 0)
        else:
            lax.fori_loop(0, n_trips // 2, trip_pair, 0)

    needs_mask = kj * tk + tk - 1 > qi * tq
    pl.when(needs_mask)(lambda: sweep(True))
    pl.when(jnp.logical_not(needs_mask))(lambda: sweep(False))

    @pl.when(kj == last_kj)
    def _():
        last = n_trips - 1
        stage_c((last - 1) * U, (last - 1) % 2, par, None, range(U))
        stage_b(last * U, last % 2, None, range(U))
        stage_c(last * U, last % 2, par, None, range(U))

        def map_out(mi):
            parts = [acc[mi * n_ct + ct] for ct in range(n_ct)]
            full = jnp.concatenate(parts, axis=1)
            return full[0:dv] / full[dv:dv + 1]

        for gi in range(gps):
            o0, o1 = map_out(2 * gi), map_out(2 * gi + 1)
            if mode == "fox":
                o = jnp.concatenate([o0, o1], axis=0)
            else:
                lp = lam_ref[...]
                t1 = jnp.sum(lp[0:1] * lp[1:2], axis=1, keepdims=True)
                t2 = jnp.sum(lp[2:3] * lp[3:4], axis=1, keepdims=True)
                lam = jnp.exp(t1) - jnp.exp(t2) + lam_init
                o = o0 - lam * o1
                ms = jnp.mean(o * o, axis=0, keepdims=True)
                o = o * lax.rsqrt(ms + SUBLN_EPS) * g_ref[...] * (1.0 - lam_init)
            o_ref[:, gi * LANES:(gi + 1) * LANES] = o.T.astype(BF16)


def _causal_steps(s, tq, tk):
    qi, kj = [], []
    for i in range(s // tq):
        for j in range((i * tq + tq - 1) // tk + 1):
            qi.append(i)
            kj.append(j)
    return jnp.asarray(qi, jnp.int32), jnp.asarray(kj, jnp.int32)


def _flash_pair(proj_t, keys, extras, *, mode, n_groups, gps, q_row, v_row, k_col, tq, tk,
                lam_init=0.0):
    s = keys.shape[0]
    qi, kj = _causal_steps(s, tq, tk)
    gw = gps * LANES
    qt_spec = pl.BlockSpec((gw, tq), lambda g, t, qi, kj: (q_row + g, qi[t]))
    k_spec = pl.BlockSpec((tk, gw), lambda g, t, qi, kj: (kj[t], k_col + g))
    vt_spec = pl.BlockSpec((gw, tk), lambda g, t, qi, kj: (v_row + g, kj[t]))
    const = lambda x: pl.BlockSpec(x.shape, lambda g, t, qi, kj: (0, 0))
    if mode == "fox":
        (cfeat,) = extras
        in_specs = [qt_spec, k_spec,
                    pl.BlockSpec((tk, LANES), lambda g, t, qi, kj: (kj[t], 0)), vt_spec]
        args = (proj_t, keys, cfeat, proj_t)
        kd, dv = 2 * LANES, HEAD_DIM
    else:
        lam_params, subln_g = extras
        in_specs = [qt_spec, k_spec, vt_spec, const(lam_params), const(subln_g)]
        args = (proj_t, keys, proj_t, lam_params, subln_g)
        kd, dv = LANES, LANES
    rows = dv + ONES_ROWS
    n_strips = 2 * gps * (tq // COL_TILE)
    u = STRIPS_PER_TRIP
    assert n_strips % (2 * u) == 0
    kern = functools.partial(_flash_pair_kernel, mode=mode, tq=tq, tk=tk, gps=gps,
                             lam_init=lam_init)
    return pl.pallas_call(
        kern,
        grid_spec=pltpu.PrefetchScalarGridSpec(
            num_scalar_prefetch=2,
            grid=(n_groups // gps, qi.shape[0]),
            in_specs=in_specs,
            out_specs=pl.BlockSpec((tq, gw), lambda g, t, qi, kj: (qi[t], g)),
            scratch_shapes=[
                pltpu.VMEM((n_strips, kd, COL_TILE), BF16),
                pltpu.VMEM((n_strips, 8, COL_TILE), F32),
                pltpu.VMEM((n_strips, rows, COL_TILE), F32),
                pltpu.VMEM((gps, tk, kd), BF16),
                pltpu.VMEM((2, 2 * gps, rows, tk), BF16),
            ] + [pltpu.VMEM((u, tk, COL_TILE), F32)] * 2
              + [pltpu.VMEM((u, 8, COL_TILE), F32)] * 2
              + [pltpu.VMEM((u, tk, COL_TILE), BF16)] * 2
              + [pltpu.VMEM((u, 8, COL_TILE), F32)] * 2,
        ),
        out_shape=jax.ShapeDtypeStruct((s, n_groups * LANES), BF16),
        compiler_params=_params(("arbitrary", "arbitrary")),
        name=f"flash_{mode}",
    )(qi, kj, *args)


def _hyb_out_kernel(*refs, dilations):
    n_pat = len(dilations)
    pat = refs[:2 * n_pat]
    ob_ref, w_ref, h_ref, out_ref = refs[2 * n_pat:2 * n_pat + 4]
    scratch = list(refs[2 * n_pat + 4:])
    tm = h_ref.shape[0]
    vals = []
    for idx, ref in enumerate(pat):
        dil = dilations[idx // 2]
        if dil == 1:
            vals.append(ref[...])
            continue
        buf = scratch.pop(0)
        n_chunks = buf.shape[0]
        for r in range(dil):
            for c in range(n_chunks):
                c0 = (r * n_chunks + c) * LANES
                buf[c, pl.ds(r, tm // dil, stride=dil), :] = ref[:, c0:c0 + LANES]
        vals.append(jnp.concatenate([buf[c] for c in range(n_chunks)], axis=1))
    os_, ls_ = vals[0::2], vals[1::2]
    m = functools.reduce(jnp.maximum, ls_)
    es = [jnp.exp(l - m) for l in ls_]
    oa = sum(e * o for e, o in zip(es, os_)) / sum(es)
    wa = oa.shape[1]
    acc = jnp.dot(oa.astype(BF16), w_ref[0:wa, :], preferred_element_type=F32)
    acc = acc + jnp.dot(ob_ref[...], w_ref[wa:, :], preferred_element_type=F32)
    out_ref[...] = h_ref[...] + acc


def _hyb_out(pattern_outs, dilations, ob, w, h, *, tm):
    s, d = h.shape
    wa = pattern_outs[0][0].shape[1] // dilations[0]
    in_specs, args, scratch = [], [], []
    for (o, lse), dil in zip(pattern_outs, dilations):
        for arr in (o, lse):
            in_specs.append(pl.BlockSpec((tm // dil, dil * wa), lambda i: (i, 0)))
            args.append(arr)
            if dil > 1:
                scratch.append(pltpu.VMEM((wa // LANES, tm, LANES), F32))
    return pl.pallas_call(
        functools.partial(_hyb_out_kernel, dilations=tuple(dilations)),
        grid=(s // tm,),
        in_specs=in_specs + [
            pl.BlockSpec((tm, ob.shape[1]), lambda i: (i, 0)),
            pl.BlockSpec(w.shape, lambda i: (0, 0)),
            pl.BlockSpec((tm, d), lambda i: (i, 0)),
        ],
        out_specs=pl.BlockSpec((tm, d), lambda i: (i, 0)),
        out_shape=jax.ShapeDtypeStruct((s, d), F32),
        scratch_shapes=scratch,
        compiler_params=_params(("parallel",)),
        name="hyb_out",
    )(*args, ob, w, h)


def _proj_res_kernel(a_ref, w_ref, h_ref, out_ref):
    out_ref[...] = h_ref[...] + jnp.dot(a_ref[...], w_ref[...], preferred_element_type=F32)


def _proj_res(a, w, h, *, tm):
    s, d = h.shape
    return pl.pallas_call(
        _proj_res_kernel,
        grid=(s // tm,),
        in_specs=[
            pl.BlockSpec((tm, a.shape[1]), lambda i: (i, 0)),
            pl.BlockSpec(w.shape, lambda i: (0, 0)),
            pl.BlockSpec((tm, d), lambda i: (i, 0)),
        ],
        out_specs=pl.BlockSpec((tm, d), lambda i: (i, 0)),
        out_shape=jax.ShapeDtypeStruct((s, d), F32),
        compiler_params=_params(("parallel",)),
        name="proj_res",
    )(a, w, h)


HALO = BF16_SUBLANES


def _ffn_kernel(h_ref, halo_ref, g_ref, wg_ref, wu_ref, cw_ref, cb_ref, wd_ref, *rest,
                tm, final):
    if final:
        fg_ref, out_ref, n_scr, gate_scr, acc = rest
    else:
        out_ref, n_scr, gate_scr, acc = rest
    i = pl.program_id(0)
    j = pl.program_id(1)

    @pl.when(j == 0)
    def _():
        g = g_ref[...]
        prev = jnp.where(i > 0, halo_ref[...], 0.0)
        n_scr[0:HALO] = _rms(prev, g, NORM_EPS).astype(BF16)
        n_scr[HALO:] = _rms(h_ref[...], g, NORM_EPS).astype(BF16)
        acc[...] = jnp.zeros_like(acc)

    gate_scr[...] = jnp.dot(n_scr[...], wg_ref[...], preferred_element_type=F32)
    up = jnp.dot(n_scr[HALO:], wu_ref[...], preferred_element_type=F32)
    conv = cb_ref[...]
    for t in range(CONV_WIDTH):
        start = HALO - (CONV_WIDTH - 1) + t
        conv = conv + gate_scr[start:start + tm] * cw_ref[t:t + 1, :]
    act = conv * (1.0 / (1.0 + jnp.exp(-conv))) * up
    acc[...] += jnp.dot(act.astype(BF16), wd_ref[...], preferred_element_type=F32)

    @pl.when(j == pl.num_programs(1) - 1)
    def _():
        y = h_ref[...] + acc[...]
        if final:
            y = _rms(y, fg_ref[...], NORM_EPS)
        out_ref[...] = y


def _ffn(h, g, w_up, conv_w, conv_b, w_down, final_g, *, tm, tf):
    s, d = h.shape
    d_ff = w_down.shape[0]
    nf = d_ff // tf
    final = final_g is not None
    in_specs = [
        pl.BlockSpec((tm, d), lambda i, j: (i, 0)),
        pl.BlockSpec((HALO, d), lambda i, j: (jnp.maximum(i * (tm // HALO) - 1, 0), 0)),
        pl.BlockSpec((1, d), lambda i, j: (0, 0)),
        pl.BlockSpec((d, tf), lambda i, j: (0, j)),
        pl.BlockSpec((d, tf), lambda i, j: (0, nf + j)),
        pl.BlockSpec((CONV_WIDTH, tf), lambda i, j: (0, j)),
        pl.BlockSpec((1, tf), lambda i, j: (0, j)),
        pl.BlockSpec((tf, d), lambda i, j: (j, 0)),
    ]
    args = [h, h, g, w_up, w_up, conv_w, conv_b, w_down]
    if final:
        in_specs.append(pl.BlockSpec((1, d), lambda i, j: (0, 0)))
        args.append(final_g)
    return pl.pallas_call(
        functools.partial(_ffn_kernel, tm=tm, final=final),
        grid=(s // tm, nf),
        in_specs=in_specs,
        out_specs=pl.BlockSpec((tm, d), lambda i, j: (i, 0)),
        out_shape=jax.ShapeDtypeStruct((s, d), F32),
        scratch_shapes=[
            pltpu.VMEM((tm + HALO, d), BF16),
            pltpu.VMEM((tm + HALO, tf), F32),
            pltpu.VMEM((tm, d), F32),
        ],
        compiler_params=_params(("parallel", "arbitrary")),
        name="ffn_final" if final else "ffn",
    )(*args)


def _rope_tables(s):
    inv = 1.0 / (ROPE_THETA ** (jnp.arange(0, HEAD_DIM, 2, dtype=F32) / HEAD_DIM))
    ang = jnp.arange(s, dtype=F32)[:, None] * inv[None, :]
    cos, sin = jnp.cos(ang), jnp.sin(ang)
    sign = jnp.where((jnp.arange(LANES) & 32) == 0, -1.0, 1.0).astype(F32)
    cos_l = jnp.tile(cos, (1, LANES // 32))
    sin_signed = jnp.tile(sin, (1, LANES // 32)) * sign[None, :]
    return cos_l, sin_signed, cos.T, sin.T


def kernel(x, attn_norm, ffn_norm, final_norm, hyb_w_in, hyb_b_f, hyb_w_out, diff_w_qkv,
           diff_lambda, diff_subln, diff_w_out, ffn_w_up, ffn_conv_w, ffn_conv_b, ffn_w_down):
    b, s, d = x.shape
    assert b == 1
    depth = attn_norm.shape[0]
    width = hyb_w_out.shape[1] // 2
    n_pairs = width // LANES
    n_heads_b = width // HEAD_DIM
    n_diff_heads = diff_w_out.shape[1] // LANES
    dq = n_diff_heads * LANES
    tm = min(1024, s)
    tq_flash, tk_flash = min(1024, s), min(512, s)
    gps = 2
    q_scale = HEAD_DIM ** -0.5
    cos, sin_signed, cos_t, sin_t = _rope_tables(s)
    h = x[0]

    for l in range(depth):
        g_attn = attn_norm[l][None, :]
        if l % 2 == 0:
            e = l // 2
            w_in = hyb_w_in[e]
            qa_ka_va, qb, kb, vb, wf = (w_in[:, :3 * width], w_in[:, 3 * width:4 * width],
                                        w_in[:, 4 * width:5 * width], w_in[:, 5 * width:6 * width],
                                        w_in[:, 6 * width:])
            dils = tuple(dil for _, dil in DILATED_PATTERNS)
            proj, *views = _norm_proj(
                h, g_attn, jnp.concatenate([qa_ka_va, kb], axis=1).astype(BF16), cos, sin_signed,
                rope_tiles=(0, 1), scale_tiles=(0,), scale=q_scale, tm=tm, tn=width,
                strided_tiles=3, dilations=dils[1:])
            proj_t = _norm_proj_t(h, g_attn, jnp.concatenate([qb, vb], axis=1).T.astype(BF16),
                                  cos_t, sin_t, rope_tiles=(), scale_tiles=(0,),
                                  scale=q_scale * LOG2E, tm=tm, tn=width)
            wf_pad = jnp.pad(wf, ((0, 0), (0, LANES - n_heads_b))).astype(BF16)
            bf_pad = jnp.pad(hyb_b_f[e], (0, LANES - n_heads_b))[None, :]
            cfeat = _fox_gate(h, g_attn, wf_pad, bf_pad, tc=min(512, s), n_heads=n_heads_b)
            pats = [_dilated_pattern(proj, 1, width=width, per_res=4, tq=min(512, s))]
            pats += [_dilated_pattern(v, dil, width=width, per_res=3, tq=min(512, s // dil))
                     for v, dil in zip(views, dils[1:])]
            ob = _flash_pair(proj_t, proj, (cfeat,), mode="fox", n_groups=n_pairs, gps=gps,
                             q_row=0, v_row=n_pairs // gps, k_col=3 * n_pairs // gps,
                             tq=tq_flash, tk=tk_flash)
            h = _hyb_out(pats, dils, ob, hyb_w_out[e].astype(BF16), h, tm=min(512, s))
        else:
            o = l // 2
            w = diff_w_qkv[o]
            wq, wk, wv = w[:, :dq], w[:, dq:2 * dq], w[:, 2 * dq:]
            k_tiles = dq // 512
            keys = _norm_proj(h, g_attn, wk.astype(BF16), cos, sin_signed,
                              rope_tiles=tuple(range(k_tiles)), scale_tiles=(),
                              scale=1.0, tm=tm, tn=512)
            proj_t = _norm_proj_t(h, g_attn, jnp.concatenate([wq, wv], axis=1).T.astype(BF16),
                                  cos_t, sin_t, rope_tiles=tuple(range(k_tiles)),
                                  scale_tiles=tuple(range(k_tiles)), scale=q_scale * LOG2E,
                                  tm=tm, tn=512)
            lam_init = 0.8 - 0.6 * math.exp(-0.3 * l)
            att = _flash_pair(proj_t, keys, (diff_lambda[o], diff_subln[o][:, None]),
                              mode="diff", n_groups=n_diff_heads, gps=gps, q_row=0,
                              v_row=n_diff_heads // gps, k_col=0, tq=tq_flash, tk=tk_flash,
                              lam_init=lam_init)
            h = _proj_res(att, diff_w_out[o].astype(BF16), h, tm=min(512, s))
        h = _ffn(h, ffn_norm[l][None, :], ffn_w_up[l].astype(BF16), ffn_conv_w[l],
                 ffn_conv_b[l][None, :], ffn_w_down[l].astype(BF16),
                 final_norm[None, :] if l == depth - 1 else None, tm=tm, tf=256)
    return h[None]
```

```python
import functools
import math

import jax
import jax.numpy as jnp
from jax import lax
from jax.experimental import pallas as pl
from jax.experimental.pallas import tpu as pltpu

F32 = jnp.float32
BF16 = jnp.bfloat16

HEAD_DIM = 64
LANES = 128
BF16_SUBLANES = 16
ROPE_THETA = 10000.0
DILATED_PATTERNS = ((128, 1), (512, 4), (2048, 16))
BAND = 128
NORM_EPS = 1e-6
SUBLN_EPS = 1e-5
CONV_WIDTH = 3
NEG = -1e30
VMEM_LIMIT = 48 * 1024 * 1024

NT_DIMS = (((1,), (1,)), ((), ()))


def _params(sem):
    return pltpu.CompilerParams(dimension_semantics=sem, vmem_limit_bytes=VMEM_LIMIT)


def _rms(x, g, eps):
    return x * lax.rsqrt(jnp.mean(x * x, axis=-1, keepdims=True) + eps) * g


def _lane_lo(rows):
    return lax.broadcasted_iota(jnp.int32, (rows, LANES), 1) < HEAD_DIM


def _rope_tile(x, cos, sin_signed):
    rows = x.shape[0]
    first_half = (lax.broadcasted_iota(jnp.int32, (rows, LANES), 1) & 32) == 0
    out = []
    for c in range(x.shape[1] // LANES):
        xc = x[:, c * LANES:(c + 1) * LANES]
        ahead = pltpu.roll(xc, LANES - 32, 1)
        behind = pltpu.roll(xc, 32, 1)
        rot = jnp.where(first_half, ahead, behind)
        out.append(xc * cos + rot * sin_signed)
    return jnp.concatenate(out, axis=1)


def _norm_proj_kernel(h_ref, g_ref, w_ref, cos_ref, sin_ref, o_ref, *rest,
                      rope_tiles, scale_tiles, scale, tn, strided_tiles, dilations):
    n = _rms(h_ref[...], g_ref[...], NORM_EPS).astype(BF16)
    tm = h_ref.shape[0]
    for j in range(w_ref.shape[1] // tn):
        cols = slice(j * tn, (j + 1) * tn)
        acc = jnp.dot(n, w_ref[:, cols], preferred_element_type=F32)
        if j in rope_tiles:
            acc = _rope_tile(acc, cos_ref[...], sin_ref[...])
        if j in scale_tiles:
            acc = acc * scale
        o_ref[:, cols] = acc.astype(BF16)
        if j < strided_tiles:
            stage = rest[-1]
            for c in range(tn // LANES):
                stage[c] = acc[:, c * LANES:(c + 1) * LANES]
            for d_ref, dil in zip(rest[:-1], dilations):
                for r in range(dil):
                    for c in range(tn // LANES):
                        c0 = (r * strided_tiles + j) * tn + c * LANES
                        rows = stage[c, pl.ds(r, tm // dil, stride=dil), :]
                        d_ref[:, c0:c0 + LANES] = rows.astype(BF16)


def _norm_proj(h, g, w, cos, sin_signed, *, rope_tiles, scale_tiles, scale, tm, tn,
               strided_tiles=0, dilations=()):
    s, d = h.shape
    n = w.shape[1]
    kern = functools.partial(_norm_proj_kernel, rope_tiles=rope_tiles, scale_tiles=scale_tiles,
                             scale=scale, tn=tn, strided_tiles=strided_tiles,
                             dilations=dilations)
    out_specs = [pl.BlockSpec((tm, n), lambda i: (i, 0))]
    out_shape = [jax.ShapeDtypeStruct((s, n), BF16)]
    for dil in dilations:
        wide = dil * strided_tiles * tn
        out_specs.append(pl.BlockSpec((tm // dil, wide), lambda i: (i, 0)))
        out_shape.append(jax.ShapeDtypeStruct((s // dil, wide), BF16))
    outs = pl.pallas_call(
        kern,
        grid=(s // tm,),
        in_specs=[
            pl.BlockSpec((tm, d), lambda i: (i, 0)),
            pl.BlockSpec((1, d), lambda i: (0, 0)),
            pl.BlockSpec((d, n), lambda i: (0, 0)),
            pl.BlockSpec((tm, LANES), lambda i: (i, 0)),
            pl.BlockSpec((tm, LANES), lambda i: (i, 0)),
        ],
        out_specs=out_specs,
        out_shape=out_shape,
        scratch_shapes=[pltpu.VMEM((tn // LANES, tm, LANES), F32)] if dilations else [],
        compiler_params=_params(("parallel",)),
        name="norm_proj",
    )(h, g, w, cos, sin_signed)
    return outs if dilations else outs[0]


def _norm_proj_t_kernel(h_ref, g_ref, wt_ref, cos_ref, sin_ref, o_ref, *,
                        rope_tiles, scale_tiles, scale, tn):
    n = _rms(h_ref[...], g_ref[...], NORM_EPS).astype(BF16)
    half = HEAD_DIM // 2
    for j in range(wt_ref.shape[0] // tn):
        acc = lax.dot_general(wt_ref[j * tn:(j + 1) * tn, :], n, NT_DIMS,
                              preferred_element_type=F32)
        sc = scale if j in scale_tiles else 1.0
        if j in rope_tiles:
            cos, sin = cos_ref[...], sin_ref[...]
            for hd in range(tn // HEAD_DIM):
                r0 = j * tn + hd * HEAD_DIM
                x1 = acc[hd * HEAD_DIM:hd * HEAD_DIM + half]
                x2 = acc[hd * HEAD_DIM + half:(hd + 1) * HEAD_DIM]
                o_ref[r0:r0 + half] = ((x1 * cos - x2 * sin) * sc).astype(BF16)
                o_ref[r0 + half:r0 + HEAD_DIM] = ((x2 * cos + x1 * sin) * sc).astype(BF16)
        else:
            o_ref[j * tn:(j + 1) * tn] = (acc * sc).astype(BF16)


def _norm_proj_t(h, g, wt, cos_t, sin_t, *, rope_tiles, scale_tiles, scale, tm, tn):
    s, d = h.shape
    n = wt.shape[0]
    kern = functools.partial(_norm_proj_t_kernel, rope_tiles=rope_tiles,
                             scale_tiles=scale_tiles, scale=scale, tn=tn)
    return pl.pallas_call(
        kern,
        grid=(s // tm,),
        in_specs=[
            pl.BlockSpec((tm, d), lambda i: (i, 0)),
            pl.BlockSpec((1, d), lambda i: (0, 0)),
            pl.BlockSpec((n, d), lambda i: (0, 0)),
            pl.BlockSpec((HEAD_DIM // 2, tm), lambda i: (0, i)),
            pl.BlockSpec((HEAD_DIM // 2, tm), lambda i: (0, i)),
        ],
        out_specs=pl.BlockSpec((n, tm), lambda i: (0, i)),
        out_shape=jax.ShapeDtypeStruct((n, s), BF16),
        compiler_params=_params(("parallel",)),
        name="norm_proj_t",
    )(h, g, wt, cos_t, sin_t)


GATE_TERMS = 3
GATE_STRIDE = 8
LOG2E = math.log2(math.e)


def _fox_gate_kernel(h_ref, g_ref, wf_ref, bf_ref, cf_ref, carry, *, tc, n_heads):
    i = pl.program_id(0)

    @pl.when(i == 0)
    def _():
        carry[...] = jnp.zeros_like(carry)

    n = _rms(h_ref[...], g_ref[...], NORM_EPS).astype(BF16)
    z = jnp.dot(n, wf_ref[...], preferred_element_type=F32) + bf_ref[...]
    logf = jnp.minimum(z, 0.0) - jnp.log(1.0 + jnp.exp(-jnp.abs(z)))
    dst = lax.broadcasted_iota(jnp.int32, (tc, tc), 0)
    src = lax.broadcasted_iota(jnp.int32, (tc, tc), 1)
    prefix = jnp.where(src <= dst, 1.0, 0.0).astype(F32)
    cs = jnp.dot(prefix, logf, precision=lax.Precision.HIGHEST,
                 preferred_element_type=F32) + carry[0:1, :]
    carry[...] = jnp.broadcast_to(cs[tc - 1:tc, :], carry.shape)
    lane = lax.broadcasted_iota(jnp.int32, (tc, LANES), 1)
    rem = jnp.where(lane < n_heads, cs * LOG2E, 0.0)
    feat = jnp.zeros_like(rem)
    for term in range(GATE_TERMS):
        part = rem.astype(BF16).astype(F32)
        rem = rem - part
        feat = feat + (pltpu.roll(part, term * GATE_STRIDE, 1) if term else part)
    cf_ref[...] = feat.astype(BF16)


def _fox_gate(h, g, wf, b_f, *, tc, n_heads):
    s, d = h.shape
    return pl.pallas_call(
        functools.partial(_fox_gate_kernel, tc=tc, n_heads=n_heads),
        grid=(s // tc,),
        in_specs=[
            pl.BlockSpec((tc, d), lambda i: (i, 0)),
            pl.BlockSpec((1, d), lambda i: (0, 0)),
            pl.BlockSpec((d, LANES), lambda i: (0, 0)),
            pl.BlockSpec((1, LANES), lambda i: (0, 0)),
        ],
        out_specs=pl.BlockSpec((tc, LANES), lambda i: (i, 0)),
        out_shape=jax.ShapeDtypeStruct((s, LANES), BF16),
        scratch_shapes=[pltpu.VMEM((8, LANES), F32)],
        compiler_params=_params(("arbitrary",)),
        name="fox_gate",
    )(h, g, wf, b_f)


def _dilated_kernel(q_ref, kc_ref, kh_ref, vc_ref, vh_ref, o_ref, l_ref, kbuf, vbuf, *, tq):
    i = pl.program_id(1)
    kbuf[0:BAND] = kh_ref[...]
    kbuf[BAND:] = kc_ref[...]
    vbuf[0:BAND] = vh_ref[...]
    vbuf[BAND:] = vc_ref[...]
    row = lax.broadcasted_iota(jnp.int32, (BAND, 2 * BAND), 0)
    col = lax.broadcasted_iota(jnp.int32, (BAND, 2 * BAND), 1)
    delta = row - col + BAND
    band = (delta >= 0) & (delta <= BAND)
    lo = _lane_lo(BAND)
    for a in range(tq // BAND):
        first_key = i * tq + (a - 1) * BAND
        valid = band & (col + first_key >= 0)
        rows = slice(a * BAND, (a + 1) * BAND)
        for hp in range(q_ref.shape[1] // LANES):
            lanes = slice(hp * LANES, (hp + 1) * LANES)
            q = q_ref[rows, lanes]
            kk = kbuf[a * BAND:(a + 2) * BAND, lanes]
            vv = vbuf[a * BAND:(a + 2) * BAND, lanes]
            outs, lses = [], []
            for qh in (jnp.where(lo, q, jnp.zeros_like(q)), jnp.where(lo, jnp.zeros_like(q), q)):
                s = lax.dot_general(qh, kk, NT_DIMS, preferred_element_type=F32)
                s = jnp.where(valid, s, NEG)
                m = jnp.max(s, axis=1, keepdims=True)
                p = jnp.exp(s - m)
                den = jnp.sum(p, axis=1, keepdims=True)
                pv = jnp.dot(p.astype(BF16), vv, preferred_element_type=F32)
                outs.append(pv / den)
                lses.append(jnp.broadcast_to(m + jnp.log(den), (BAND, LANES)))
            o_ref[rows, lanes] = jnp.where(lo, outs[0], outs[1])
            l_ref[rows, lanes] = jnp.where(lo, lses[0], lses[1])


def _dilated_pattern(view, dil, *, width, per_res, tq):
    L = view.shape[0]
    halo_per_tile = tq // BAND
    cur = lambda part: pl.BlockSpec((tq, width), lambda r, i: (i, r * per_res + part))
    halo = lambda part: pl.BlockSpec(
        (BAND, width), lambda r, i: (jnp.maximum(i * halo_per_tile - 1, 0), r * per_res + part))
    out_spec = pl.BlockSpec((tq, width), lambda r, i: (i, r))
    return pl.pallas_call(
        functools.partial(_dilated_kernel, tq=tq),
        grid=(dil, L // tq),
        in_specs=[cur(0), cur(1), halo(1), cur(2), halo(2)],
        out_specs=[out_spec, out_spec],
        out_shape=[jax.ShapeDtypeStruct((L, dil * width), F32)] * 2,
        scratch_shapes=[pltpu.VMEM((tq + BAND, width), BF16)] * 2,
        compiler_params=_params(("parallel", "arbitrary")),
        name=f"dilated_d{dil}",
    )(view, view, view, view, view)


ONES_ROWS = BF16_SUBLANES


COL_TILE = 256
STRIPS_PER_TRIP = 4


def _flash_pair_kernel(qi_ref, kj_ref, qt_ref, k_ref, *rest, mode, tq, tk, gps, lam_init):
    if mode == "fox":
        cf_ref, vt_ref, o_ref, wq, m_scr, acc, kx_scr, vx_scr, *bufs = rest
    else:
        vt_ref, lam_ref, g_ref, o_ref, wq, m_scr, acc, kx_scr, vx_scr, *bufs = rest
    s_buf, mx_buf, p_buf, al_buf = bufs[0:2], bufs[2:4], bufs[4:6], bufs[6:8]
    U = STRIPS_PER_TRIP
    step = pl.program_id(1)
    qi = qi_ref[step]
    kj = kj_ref[step]
    last_kj = (qi * tq + tq - 1) // tk
    rows = acc.shape[1]
    dv = rows - ONES_ROWS
    n_ct = tq // COL_TILE
    n_strips = 2 * gps * n_ct
    n_trips = n_strips // U
    par = kj % 2

    @pl.when(jnp.logical_and(pl.program_id(0) == 0, step == 0))
    def _():
        for buf in bufs:
            buf[...] = jnp.zeros_like(buf)
        vx_scr[...] = jnp.zeros_like(vx_scr)

    @pl.when(kj == 0)
    def _():
        r = lax.broadcasted_iota(jnp.int32, (LANES, COL_TILE), 0)
        first = r < HEAD_DIM
        for gi in range(gps):
            for a in range(2):
                keep_q = first if a == 0 else jnp.logical_not(first)
                if mode == "fox":
                    head = 2 * (pl.program_id(0) * gps + gi) + a
                    pick = (r - head) == 0
                    for term in range(1, GATE_TERMS):
                        pick = pick | ((r - head) == term * GATE_STRIDE)
                    gate_rows = jnp.where(pick, -1.0, 0.0).astype(BF16)
                for ct in range(n_ct):
                    t = (2 * gi + a) * n_ct + ct
                    qt = qt_ref[gi * LANES:(gi + 1) * LANES, ct * COL_TILE:(ct + 1) * COL_TILE]
                    wq[t, 0:LANES] = jnp.where(keep_q, qt, jnp.zeros_like(qt))
                    if mode == "fox":
                        wq[t, LANES:2 * LANES] = gate_rows
        m_scr[...] = jnp.full_like(m_scr, NEG)
        acc[...] = jnp.zeros_like(acc)

    ones = jnp.ones((ONES_ROWS, tk), BF16)
    for gi in range(gps):
        kx_scr[gi, :, 0:LANES] = k_ref[:, gi * LANES:(gi + 1) * LANES]
        if mode == "fox":
            kx_scr[gi, :, LANES:2 * LANES] = cf_ref[...]
        for a in range(2):
            if mode == "fox":
                v0 = gi * LANES + a * HEAD_DIM
                vx_scr[par, 2 * gi + a, 0:dv] = vt_ref[v0:v0 + HEAD_DIM, :]
            else:
                vx_scr[par, 2 * gi + a, 0:dv] = vt_ref[gi * LANES:(gi + 1) * LANES, :]
            vx_scr[par, 2 * gi + a, dv:rows] = ones

    def stage_a(j, slot, u, masked):
        s = jnp.dot(kx_scr[j // (2 * n_ct)], wq[j], preferred_element_type=F32)
        if masked:
            key = kj * tk + lax.broadcasted_iota(jnp.int32, (tk, COL_TILE), 0)
            qry = (qi * tq + (j % n_ct) * COL_TILE
                   + lax.broadcasted_iota(jnp.int32, (tk, COL_TILE), 1))
            s = jnp.where(key <= qry, s, NEG)
        s_buf[slot][u] = s
        mx_buf[slot][u] = jnp.broadcast_to(jnp.max(s, axis=0, keepdims=True), (8, COL_TILE))


    def stage_b(j0, slot, valid, us):
        m_prevs = {u: m_scr[j0 + u] for u in us}
        for u, m_prev in list(m_prevs.items()):
            m_next = jnp.maximum(m_prev, mx_buf[slot][u])
            if valid is not None:
                m_next = jnp.where(valid, m_next, m_prev)
            al_buf[slot][u] = jnp.exp2(m_prev - m_next)
            s3 = s_buf[slot][u].reshape(tk // 8, 8, COL_TILE)
            p_buf[slot][u] = jnp.exp2(s3 - m_next[None]).reshape(tk, COL_TILE).astype(BF16)
            m_prevs[u] = m_next
        for u, m_next in m_prevs.items():
            m_scr[j0 + u] = m_next

    def stage_c(j0, slot, vpar, valid, us):
        pvs = [jnp.dot(vx_scr[vpar, (j0 + u) // n_ct], p_buf[slot][u],
                       preferred_element_type=F32) for u in us]
        olds = [acc[j0 + u] for u in us]
        for u, old, pv in zip(us, olds, pvs):
            new = ((old.reshape(rows // 8, 8, COL_TILE) * al_buf[slot][u][None])
                   .reshape(rows, COL_TILE) + pv)
            acc[j0 + u] = new if valid is None else jnp.where(valid, new, old)

    def sweep(masked):
        def trip(k, slot):
            later_step = kj > 0
            kc = (k + n_trips - 2) % n_trips
            kb = (k + n_trips - 1) % n_trips
            valid_c = jnp.logical_or(later_step, k >= 2)
            valid_b = jnp.logical_or(later_step, k >= 1)
            vpar = jnp.where(k >= 2, par, 1 - par)
            for u in range(U):
                stage_a(k * U + u, slot, u, masked)
            stage_b(kb * U, 1 - slot, valid_b, range(U))
            stage_c(kc * U, slot, vpar, valid_c, range(U))

        def trip_pair(half, carry):
            trip(2 * half, 0)
            trip(2 * half + 1, 1)
            return carry

        if n_trips == 2:
            trip_pair(0, 0)
        else:
            lax.fori_loop(0, n_trips // 2, trip_pair, 0)

    needs_mask = kj * tk + tk - 1 > qi * tq
    pl.when(needs_mask)(lambda: sweep(True))
    pl.when(jnp.logical_not(needs_mask))(lambda: sweep(False))

    @pl.when(kj == last_kj)
    def _():
        last = n_trips - 1
        stage_c((last - 1) * U, (last - 1) % 2, par, None, range(U))
        stage_b(last * U, last % 2, None, range(U))
        stage_c(last * U, last % 2, par, None, range(U))

        def map_out(mi):
            parts = [acc[mi * n_ct + ct] for ct in range(n_ct)]
            full = jnp.concatenate(parts, axis=1)
            return full[0:dv] / full[dv:dv + 1]

        for gi in range(gps):
            o0, o1 = map_out(2 * gi), map_out(2 * gi + 1)
            if mode == "fox":
                o = jnp.concatenate([o0, o1], axis=0)
            else:
                lp = lam_ref[...]
                t1 = jnp.sum(lp[0:1] * lp[1:2], axis=1, keepdims=True)
                t2 = jnp.sum(lp[2:3] * lp[3:4], axis=1, keepdims=True)
                lam = jnp.exp(t1) - jnp.exp(t2) + lam_init
                o = o0 - lam * o1
                ms = jnp.mean(o * o, axis=0, keepdims=True)
                o = o * lax.rsqrt(ms + SUBLN_EPS) * g_ref[...] * (1.0 - lam_init)
            o_ref[:, gi * LANES:(gi + 1) * LANES] = o.T.astype(BF16)


def _causal_steps(s, tq, tk):
    qi, kj = [], []
    for i in range(s // tq):
        for j in range((i * tq + tq - 1) // tk + 1):
            qi.append(i)
            kj.append(j)
    return jnp.asarray(qi, jnp.int32), jnp.asarray(kj, jnp.int32)


def _flash_pair(proj_t, keys, extras, *, mode, n_groups, gps, q_row, v_row, k_col, tq, tk,
                lam_init=0.0):
    s = keys.shape[0]
    qi, kj = _causal_steps(s, tq, tk)
    gw = gps * LANES
    qt_spec = pl.BlockSpec((gw, tq), lambda g, t, qi, kj: (q_row + g, qi[t]))
    k_spec = pl.BlockSpec((tk, gw), lambda g, t, qi, kj: (kj[t], k_col + g))
    vt_spec = pl.BlockSpec((gw, tk), lambda g, t, qi, kj: (v_row + g, kj[t]))
    const = lambda x: pl.BlockSpec(x.shape, lambda g, t, qi, kj: (0, 0))
    if mode == "fox":
        (cfeat,) = extras
        in_specs = [qt_spec, k_spec,
                    pl.BlockSpec((tk, LANES), lambda g, t, qi, kj: (kj[t], 0)), vt_spec]
        args = (proj_t, keys, cfeat, proj_t)
        kd, dv = 2 * LANES, HEAD_DIM
    else:
        lam_params, subln_g = extras
        in_specs = [qt_spec, k_spec, vt_spec, const(lam_params), const(subln_g)]
        args = (proj_t, keys, proj_t, lam_params, subln_g)
        kd, dv = LANES, LANES
    rows = dv + ONES_ROWS
    n_strips = 2 * gps * (tq // COL_TILE)
    u = STRIPS_PER_TRIP
    assert n_strips % (2 * u) == 0
    kern = functools.partial(_flash_pair_kernel, mode=mode, tq=tq, tk=tk, gps=gps,
                             lam_init=lam_init)
    return pl.pallas_call(
        kern,
        grid_spec=pltpu.PrefetchScalarGridSpec(
            num_scalar_prefetch=2,
            grid=(n_groups // gps, qi.shape[0]),
            in_specs=in_specs,
            out_specs=pl.BlockSpec((tq, gw), lambda g, t, qi, kj: (qi[t], g)),
            scratch_shapes=[
                pltpu.VMEM((n_strips, kd, COL_TILE), BF16),
                pltpu.VMEM((n_strips, 8, COL_TILE), F32),
                pltpu.VMEM((n_strips, rows, COL_TILE), F32),
                pltpu.VMEM((gps, tk, kd), BF16),
                pltpu.VMEM((2, 2 * gps, rows, tk), BF16),
            ] + [pltpu.VMEM((u, tk, COL_TILE), F32)] * 2
              + [pltpu.VMEM((u, 8, COL_TILE), F32)] * 2
              + [pltpu.VMEM((u, tk, COL_TILE), BF16)] * 2
              + [pltpu.VMEM((u, 8, COL_TILE), F32)] * 2,
        ),
        out_shape=jax.ShapeDtypeStruct((s, n_groups * LANES), BF16),
        compiler_params=_params(("arbitrary", "arbitrary")),
        name=f"flash_{mode}",
    )(qi, kj, *args)


def _hyb_out_kernel(*refs, dilations):
    n_pat = len(dilations)
    pat = refs[:2 * n_pat]
    ob_ref, w_ref, h_ref, out_ref = refs[2 * n_pat:2 * n_pat + 4]
    scratch = list(refs[2 * n_pat + 4:])
    tm = h_ref.shape[0]
    vals = []
    for idx, ref in enumerate(pat):
        dil = dilations[idx // 2]
        if dil == 1:
            vals.append(ref[...])
            continue
        buf = scratch.pop(0)
        n_chunks = buf.shape[0]
        for r in range(dil):
            for c in range(n_chunks):
                c0 = (r * n_chunks + c) * LANES
                buf[c, pl.ds(r, tm // dil, stride=dil), :] = ref[:, c0:c0 + LANES]
        vals.append(jnp.concatenate([buf[c] for c in range(n_chunks)], axis=1))
    os_, ls_ = vals[0::2], vals[1::2]
    m = functools.reduce(jnp.maximum, ls_)
    es = [jnp.exp(l - m) for l in ls_]
    oa = sum(e * o for e, o in zip(es, os_)) / sum(es)
    wa = oa.shape[1]
    acc = jnp.dot(oa.astype(BF16), w_ref[0:wa, :], preferred_element_type=F32)
    acc = acc + jnp.dot(ob_ref[...], w_ref[wa:, :], preferred_element_type=F32)
    out_ref[...] = h_ref[...] + acc


def _hyb_out(pattern_outs, dilations, ob, w, h, *, tm):
    s, d = h.shape
    wa = pattern_outs[0][0].shape[1] // dilations[0]
    in_specs, args, scratch = [], [], []
    for (o, lse), dil in zip(pattern_outs, dilations):
        for arr in (o, lse):
            in_specs.append(pl.BlockSpec((tm // dil, dil * wa), lambda i: (i, 0)))
            args.append(arr)
            if dil > 1:
                scratch.append(pltpu.VMEM((wa // LANES, tm, LANES), F32))
    return pl.pallas_call(
        functools.partial(_hyb_out_kernel, dilations=tuple(dilations)),
        grid=(s // tm,),
        in_specs=in_specs + [
            pl.BlockSpec((tm, ob.shape[1]), lambda i: (i, 0)),
            pl.BlockSpec(w.shape, lambda i: (0, 0)),
            pl.BlockSpec((tm, d), lambda i: (i, 0)),
        ],
        out_specs=pl.BlockSpec((tm, d), lambda i: (i, 0)),
        out_shape=jax.ShapeDtypeStruct((s, d), F32),
        scratch_shapes=scratch,
        compiler_params=_params(("parallel",)),
        name="hyb_out",
    )(*args, ob, w, h)


def _proj_res_kernel(a_ref, w_ref, h_ref, out_ref):
    out_ref[...] = h_ref[...] + jnp.dot(a_ref[...], w_ref[...], preferred_element_type=F32)


def _proj_res(a, w, h, *, tm):
    s, d = h.shape
    return pl.pallas_call(
        _proj_res_kernel,
        grid=(s // tm,),
        in_specs=[
            pl.BlockSpec((tm, a.shape[1]), lambda i: (i, 0)),
            pl.BlockSpec(w.shape, lambda i: (0, 0)),
            pl.BlockSpec((tm, d), lambda i: (i, 0)),
        ],
        out_specs=pl.BlockSpec((tm, d), lambda i: (i, 0)),
        out_shape=jax.ShapeDtypeStruct((s, d), F32),
        compiler_params=_params(("parallel",)),
        name="proj_res",
    )(a, w, h)


HALO = BF16_SUBLANES


FFN_SUB = 256


def _ffn_kernel(h_ref, halo_ref, g_ref, wg_ref, wu_ref, cw_ref, cb_ref, wd_ref, *rest,
                tm, final):
    if final:
        fg_ref, out_ref, n_scr, gate_scr, act_scr = rest
    else:
        out_ref, n_scr, gate_scr, act_scr = rest
    i = pl.program_id(0)
    j = pl.program_id(1)
    n_chunks, _, chunk = act_scr.shape

    @pl.when(j == 0)
    def _():
        g = g_ref[...]
        prev = jnp.where(i > 0, halo_ref[...], 0.0)
        n_scr[0:HALO] = _rms(prev, g, NORM_EPS).astype(BF16)
        n_scr[HALO:] = _rms(h_ref[...], g, NORM_EPS).astype(BF16)

    for t, c0 in enumerate(range(0, chunk, FFN_SUB)):
        w = min(FFN_SUB, chunk - c0)
        cols = slice(c0, c0 + w)
        gate_scr[t, :, 0:w] = jnp.dot(n_scr[...], wg_ref[:, cols], preferred_element_type=F32)
        up = jnp.dot(n_scr[HALO:], wu_ref[:, cols], preferred_element_type=F32)
        conv = cb_ref[:, cols]
        for k in range(CONV_WIDTH):
            start = HALO - (CONV_WIDTH - 1) + k
            conv = conv + gate_scr[t, start:start + tm, 0:w] * cw_ref[k:k + 1, cols]
        act_scr[j, :, cols] = (conv * (1.0 / (1.0 + jnp.exp(-conv))) * up).astype(BF16)

    @pl.when(j == n_chunks - 1)
    def _():
        act = jnp.concatenate([act_scr[t] for t in range(n_chunks)], axis=1)
        y = h_ref[...] + jnp.dot(act, wd_ref[...], preferred_element_type=F32)
        if final:
            y = _rms(y, fg_ref[...], NORM_EPS)
        out_ref[...] = y


def _ffn(h, g, w_up, conv_w, conv_b, w_down, final_g, *, tm, n_chunks):
    s, d = h.shape
    d_ff = w_down.shape[0]
    chunk = d_ff // n_chunks
    assert chunk * n_chunks == d_ff and chunk % LANES == 0
    n_sub = -(-chunk // FFN_SUB)
    final = final_g is not None
    once = pl.Buffered(1)
    up_mode = once if n_chunks == 1 else None
    in_specs = [
        pl.BlockSpec((tm, d), lambda i, j: (i, 0)),
        pl.BlockSpec((HALO, d), lambda i, j: (jnp.maximum(i * (tm // HALO) - 1, 0), 0)),
        pl.BlockSpec((1, d), lambda i, j: (0, 0)),
        pl.BlockSpec((d, chunk), lambda i, j: (0, j), pipeline_mode=up_mode),
        pl.BlockSpec((d, chunk), lambda i, j: (0, n_chunks + j), pipeline_mode=up_mode),
        pl.BlockSpec((CONV_WIDTH, chunk), lambda i, j: (0, j)),
        pl.BlockSpec((1, chunk), lambda i, j: (0, j)),
        pl.BlockSpec((d_ff, d), lambda i, j: (0, 0), pipeline_mode=once),
    ]
    args = [h, h, g, w_up, w_up, conv_w, conv_b, w_down]
    if final:
        in_specs.append(pl.BlockSpec((1, d), lambda i, j: (0, 0)))
        args.append(final_g)
    return pl.pallas_call(
        functools.partial(_ffn_kernel, tm=tm, final=final),
        grid=(s // tm, n_chunks),
        in_specs=in_specs,
        out_specs=pl.BlockSpec((tm, d), lambda i, j: (i, 0)),
        out_shape=jax.ShapeDtypeStruct((s, d), F32),
        scratch_shapes=[
            pltpu.VMEM((tm + HALO, d), BF16),
            pltpu.VMEM((n_sub, tm + HALO, FFN_SUB), F32),
            pltpu.VMEM((n_chunks, tm, chunk), BF16),
        ],
        compiler_params=_params(("parallel", "arbitrary")),
        name="ffn_final" if final else "ffn",
    )(*args)


def _rope_tables(s):
    inv = 1.0 / (ROPE_THETA ** (jnp.arange(0, HEAD_DIM, 2, dtype=F32) / HEAD_DIM))
    ang = jnp.arange(s, dtype=F32)[:, None] * inv[None, :]
    cos, sin = jnp.cos(ang), jnp.sin(ang)
    sign = jnp.where((jnp.arange(LANES) & 32) == 0, -1.0, 1.0).astype(F32)
    cos_l = jnp.tile(cos, (1, LANES // 32))
    sin_signed = jnp.tile(sin, (1, LANES // 32)) * sign[None, :]
    return cos_l, sin_signed, cos.T, sin.T


def kernel(x, attn_norm, ffn_norm, final_norm, hyb_w_in, hyb_b_f, hyb_w_out, diff_w_qkv,
           diff_lambda, diff_subln, diff_w_out, ffn_w_up, ffn_conv_w, ffn_conv_b, ffn_w_down):
    b, s, d = x.shape
    assert b == 1
    depth = attn_norm.shape[0]
    width = hyb_w_out.shape[1] // 2
    n_pairs = width // LANES
    n_heads_b = width // HEAD_DIM
    n_diff_heads = diff_w_out.shape[1] // LANES
    dq = n_diff_heads * LANES
    tm = min(1024, s)
    tq_flash, tk_flash = min(1024, s), min(512, s)
    gps = 4
    q_scale = HEAD_DIM ** -0.5
    cos, sin_signed, cos_t, sin_t = _rope_tables(s)
    h = x[0]

    for l in range(depth):
        g_attn = attn_norm[l][None, :]
        if l % 2 == 0:
            e = l // 2
            w_in = hyb_w_in[e]
            qa_ka_va, qb, kb, vb, wf = (w_in[:, :3 * width], w_in[:, 3 * width:4 * width],
                                        w_in[:, 4 * width:5 * width], w_in[:, 5 * width:6 * width],
                                        w_in[:, 6 * width:])
            dils = tuple(dil for _, dil in DILATED_PATTERNS)
            proj, *views = _norm_proj(
                h, g_attn, jnp.concatenate([qa_ka_va, kb], axis=1).astype(BF16), cos, sin_signed,
                rope_tiles=(0, 1), scale_tiles=(0,), scale=q_scale, tm=tm, tn=width,
                strided_tiles=3, dilations=dils[1:])
            proj_t = _norm_proj_t(h, g_attn, jnp.concatenate([qb, vb], axis=1).T.astype(BF16),
                                  cos_t, sin_t, rope_tiles=(), scale_tiles=(0,),
                                  scale=q_scale * LOG2E, tm=tm, tn=width)
            wf_pad = jnp.pad(wf, ((0, 0), (0, LANES - n_heads_b))).astype(BF16)
            bf_pad = jnp.pad(hyb_b_f[e], (0, LANES - n_heads_b))[None, :]
            cfeat = _fox_gate(h, g_attn, wf_pad, bf_pad, tc=min(512, s), n_heads=n_heads_b)
            pats = [_dilated_pattern(proj, 1, width=width, per_res=4, tq=min(512, s))]
            pats += [_dilated_pattern(v, dil, width=width, per_res=3, tq=min(512, s // dil))
                     for v, dil in zip(views, dils[1:])]
            ob = _flash_pair(proj_t, proj, (cfeat,), mode="fox", n_groups=n_pairs, gps=gps,
                             q_row=0, v_row=n_pairs // gps, k_col=3 * n_pairs // gps,
                             tq=tq_flash, tk=tk_flash)
            h = _hyb_out(pats, dils, ob, hyb_w_out[e].astype(BF16), h, tm=min(512, s))
        else:
            o = l // 2
            w = diff_w_qkv[o]
            wq, wk, wv = w[:, :dq], w[:, dq:2 * dq], w[:, 2 * dq:]
            k_tiles = dq // 512
            keys = _norm_proj(h, g_attn, wk.astype(BF16), cos, sin_signed,
                              rope_tiles=tuple(range(k_tiles)), scale_tiles=(),
                              scale=1.0, tm=tm, tn=512)
            proj_t = _norm_proj_t(h, g_attn, jnp.concatenate([wq, wv], axis=1).T.astype(BF16),
                                  cos_t, sin_t, rope_tiles=tuple(range(k_tiles)),
                                  scale_tiles=tuple(range(k_tiles)), scale=q_scale * LOG2E,
                                  tm=tm, tn=512)
            lam_init = 0.8 - 0.6 * math.exp(-0.3 * l)
            att = _flash_pair(proj_t, keys, (diff_lambda[o], diff_subln[o][:, None]),
                              mode="diff", n_groups=n_diff_heads, gps=gps, q_row=0,
                              v_row=n_diff_heads // gps, k_col=0, tq=tq_flash, tk=tk_flash,
                              lam_init=lam_init)
            h = _proj_res(att, diff_w_out[o].astype(BF16), h, tm=min(512, s))
        h = _ffn(h, ffn_norm[l][None, :], ffn_w_up[l].astype(BF16), ffn_conv_w[l],
                 ffn_conv_b[l][None, :], ffn_w_down[l].astype(BF16),
                 final_norm[None, :] if l == depth - 1 else None, tm=min(512, s), n_chunks=1)
    return h[None]
```

```python
import functools
import math

import jax
import jax.numpy as jnp
from jax import lax
from jax.experimental import pallas as pl
from jax.experimental.pallas import tpu as pltpu

F32 = jnp.float32
BF16 = jnp.bfloat16

HEAD_DIM = 64
LANES = 128
BF16_SUBLANES = 16
ROPE_THETA = 10000.0
DILATED_PATTERNS = ((128, 1), (512, 4), (2048, 16))
BAND = 128
NORM_EPS = 1e-6
SUBLN_EPS = 1e-5
CONV_WIDTH = 3
NEG = -1e30
VMEM_LIMIT = 48 * 1024 * 1024
PROJ_TILE = 512

NT_DIMS = (((1,), (1,)), ((), ()))


def _params(sem):
    return pltpu.CompilerParams(dimension_semantics=sem, vmem_limit_bytes=VMEM_LIMIT)


def _rms(x, g, eps):
    return x * lax.rsqrt(jnp.mean(x * x, axis=-1, keepdims=True) + eps) * g


def _lane_lo(rows):
    return lax.broadcasted_iota(jnp.int32, (rows, LANES), 1) < HEAD_DIM


def _rope_tile(x, cos, sin_signed):
    rows = x.shape[0]
    first_half = (lax.broadcasted_iota(jnp.int32, (rows, LANES), 1) & 32) == 0
    out = []
    for c in range(x.shape[1] // LANES):
        xc = x[:, c * LANES:(c + 1) * LANES]
        ahead = pltpu.roll(xc, LANES - 32, 1)
        behind = pltpu.roll(xc, 32, 1)
        rot = jnp.where(first_half, ahead, behind)
        out.append(xc * cos + rot * sin_signed)
    return jnp.concatenate(out, axis=1)


def _norm_proj_kernel(h_ref, g_ref, w_ref, cos_ref, sin_ref, o_ref, *rest,
                      rope_tiles, scale_tiles, scale, tn, strided_tiles, dilations):
    n = _rms(h_ref[...], g_ref[...], NORM_EPS).astype(BF16)
    tm = h_ref.shape[0]
    for j in range(w_ref.shape[1] // tn):
        cols = slice(j * tn, (j + 1) * tn)
        acc = jnp.dot(n, w_ref[:, cols], preferred_element_type=F32)
        if j in rope_tiles:
            acc = _rope_tile(acc, cos_ref[...], sin_ref[...])
        if j in scale_tiles:
            acc = acc * scale
        o_ref[:, cols] = acc.astype(BF16)
        if j < strided_tiles:
            stage = rest[-1]
            for c in range(tn // LANES):
                stage[c] = acc[:, c * LANES:(c + 1) * LANES]
            for d_ref, dil in zip(rest[:-1], dilations):
                for r in range(dil):
                    for c in range(tn // LANES):
                        c0 = (r * strided_tiles + j) * tn + c * LANES
                        rows = stage[c, pl.ds(r, tm // dil, stride=dil), :]
                        d_ref[:, c0:c0 + LANES] = rows.astype(BF16)


def _norm_proj(h, g, w, cos, sin_signed, *, rope_tiles, scale_tiles, scale, tm, tn,
               strided_tiles=0, dilations=()):
    s, d = h.shape
    n = w.shape[1]
    kern = functools.partial(_norm_proj_kernel, rope_tiles=rope_tiles, scale_tiles=scale_tiles,
                             scale=scale, tn=tn, strided_tiles=strided_tiles,
                             dilations=dilations)
    out_specs = [pl.BlockSpec((tm, n), lambda i: (i, 0))]
    out_shape = [jax.ShapeDtypeStruct((s, n), BF16)]
    for dil in dilations:
        wide = dil * strided_tiles * tn
        out_specs.append(pl.BlockSpec((tm // dil, wide), lambda i: (i, 0)))
        out_shape.append(jax.ShapeDtypeStruct((s // dil, wide), BF16))
    outs = pl.pallas_call(
        kern,
        grid=(s // tm,),
        in_specs=[
            pl.BlockSpec((tm, d), lambda i: (i, 0)),
            pl.BlockSpec((1, d), lambda i: (0, 0)),
            pl.BlockSpec((d, n), lambda i: (0, 0)),
            pl.BlockSpec((tm, LANES), lambda i: (i, 0)),
            pl.BlockSpec((tm, LANES), lambda i: (i, 0)),
        ],
        out_specs=out_specs,
        out_shape=out_shape,
        scratch_shapes=[pltpu.VMEM((tn // LANES, tm, LANES), F32)] if dilations else [],
        compiler_params=_params(("parallel",)),
        name="norm_proj",
    )(h, g, w, cos, sin_signed)
    return outs if dilations else outs[0]


def _norm_proj_t_kernel(h_ref, g_ref, wt_ref, cos_ref, sin_ref, o_ref, *,
                        rope_tiles, scale_tiles, scale, tn):
    n = _rms(h_ref[...], g_ref[...], NORM_EPS).astype(BF16)
    half = HEAD_DIM // 2
    for j in range(wt_ref.shape[0] // tn):
        acc = lax.dot_general(wt_ref[j * tn:(j + 1) * tn, :], n, NT_DIMS,
                              preferred_element_type=F32)
        sc = scale if j in scale_tiles else 1.0
        if j in rope_tiles:
            cos, sin = cos_ref[...], sin_ref[...]
            for hd in range(tn // HEAD_DIM):
                r0 = j * tn + hd * HEAD_DIM
                x1 = acc[hd * HEAD_DIM:hd * HEAD_DIM + half]
                x2 = acc[hd * HEAD_DIM + half:(hd + 1) * HEAD_DIM]
                o_ref[r0:r0 + half] = ((x1 * cos - x2 * sin) * sc).astype(BF16)
                o_ref[r0 + half:r0 + HEAD_DIM] = ((x2 * cos + x1 * sin) * sc).astype(BF16)
        else:
            o_ref[j * tn:(j + 1) * tn] = (acc * sc).astype(BF16)


def _norm_proj_t(h, g, wt, cos_t, sin_t, *, rope_tiles, scale_tiles, scale, tm, tn):
    s, d = h.shape
    n = wt.shape[0]
    kern = functools.partial(_norm_proj_t_kernel, rope_tiles=rope_tiles,
                             scale_tiles=scale_tiles, scale=scale, tn=tn)
    return pl.pallas_call(
        kern,
        grid=(s // tm,),
        in_specs=[
            pl.BlockSpec((tm, d), lambda i: (i, 0)),
            pl.BlockSpec((1, d), lambda i: (0, 0)),
            pl.BlockSpec((n, d), lambda i: (0, 0)),
            pl.BlockSpec((HEAD_DIM // 2, tm), lambda i: (0, i)),
            pl.BlockSpec((HEAD_DIM // 2, tm), lambda i: (0, i)),
        ],
        out_specs=pl.BlockSpec((n, tm), lambda i: (0, i)),
        out_shape=jax.ShapeDtypeStruct((n, s), BF16),
        compiler_params=_params(("parallel",)),
        name="norm_proj_t",
    )(h, g, wt, cos_t, sin_t)


GATE_TERMS = 3
GATE_STRIDE = 8
LOG2E = math.log2(math.e)


def _fox_gate_kernel(h_ref, g_ref, wf_ref, bf_ref, cf_ref, carry, *, tc, n_heads):
    i = pl.program_id(0)

    @pl.when(i == 0)
    def _():
        carry[...] = jnp.zeros_like(carry)

    n = _rms(h_ref[...], g_ref[...], NORM_EPS).astype(BF16)
    z = jnp.dot(n, wf_ref[...], preferred_element_type=F32) + bf_ref[...]
    logf = jnp.minimum(z, 0.0) - jnp.log(1.0 + jnp.exp(-jnp.abs(z)))
    dst = lax.broadcasted_iota(jnp.int32, (tc, tc), 0)
    src = lax.broadcasted_iota(jnp.int32, (tc, tc), 1)
    prefix = jnp.where(src <= dst, 1.0, 0.0).astype(F32)
    cs = jnp.dot(prefix, logf, precision=lax.Precision.HIGHEST,
                 preferred_element_type=F32) + carry[0:1, :]
    carry[...] = jnp.broadcast_to(cs[tc - 1:tc, :], carry.shape)
    lane = lax.broadcasted_iota(jnp.int32, (tc, LANES), 1)
    rem = jnp.where(lane < n_heads, cs * LOG2E, 0.0)
    feat = jnp.zeros_like(rem)
    for term in range(GATE_TERMS):
        part = rem.astype(BF16).astype(F32)
        rem = rem - part
        feat = feat + (pltpu.roll(part, term * GATE_STRIDE, 1) if term else part)
    cf_ref[...] = feat.astype(BF16)


def _fox_gate(h, g, wf, b_f, *, tc, n_heads):
    s, d = h.shape
    return pl.pallas_call(
        functools.partial(_fox_gate_kernel, tc=tc, n_heads=n_heads),
        grid=(s // tc,),
        in_specs=[
            pl.BlockSpec((tc, d), lambda i: (i, 0)),
            pl.BlockSpec((1, d), lambda i: (0, 0)),
            pl.BlockSpec((d, LANES), lambda i: (0, 0)),
            pl.BlockSpec((1, LANES), lambda i: (0, 0)),
        ],
        out_specs=pl.BlockSpec((tc, LANES), lambda i: (i, 0)),
        out_shape=jax.ShapeDtypeStruct((s, LANES), BF16),
        scratch_shapes=[pltpu.VMEM((8, LANES), F32)],
        compiler_params=_params(("arbitrary",)),
        name="fox_gate",
    )(h, g, wf, b_f)


def _dilated_kernel(q_ref, kc_ref, kh_ref, vc_ref, vh_ref, o_ref, l_ref, kbuf, vbuf, *, tq):
    i = pl.program_id(1)
    kbuf[0:BAND] = kh_ref[...]
    kbuf[BAND:] = kc_ref[...]
    vbuf[0:BAND] = vh_ref[...]
    vbuf[BAND:] = vc_ref[...]
    row = lax.broadcasted_iota(jnp.int32, (BAND, 2 * BAND), 0)
    col = lax.broadcasted_iota(jnp.int32, (BAND, 2 * BAND), 1)
    delta = row - col + BAND
    band = (delta >= 0) & (delta <= BAND)
    lo = _lane_lo(BAND)
    for a in range(tq // BAND):
        first_key = i * tq + (a - 1) * BAND
        valid = band & (col + first_key >= 0)
        rows = slice(a * BAND, (a + 1) * BAND)
        for hp in range(q_ref.shape[1] // LANES):
            lanes = slice(hp * LANES, (hp + 1) * LANES)
            q = q_ref[rows, lanes]
            kk = kbuf[a * BAND:(a + 2) * BAND, lanes]
            vv = vbuf[a * BAND:(a + 2) * BAND, lanes]
            outs, lses = [], []
            for qh in (jnp.where(lo, q, jnp.zeros_like(q)), jnp.where(lo, jnp.zeros_like(q), q)):
                s = lax.dot_general(qh, kk, NT_DIMS, preferred_element_type=F32)
                s = jnp.where(valid, s, NEG)
                m = jnp.max(s, axis=1, keepdims=True)
                p = jnp.exp(s - m)
                den = jnp.sum(p, axis=1, keepdims=True)
                pv = jnp.dot(p.astype(BF16), vv, preferred_element_type=F32)
                outs.append(pv / den)
                lses.append(jnp.broadcast_to(m + jnp.log(den), (BAND, LANES)))
            o_ref[rows, lanes] = jnp.where(lo, outs[0], outs[1])
            l_ref[rows, lanes] = jnp.where(lo, lses[0], lses[1])


def _dilated_pattern(view, dil, *, width, per_res, tq):
    L = view.shape[0]
    halo_per_tile = tq // BAND
    cur = lambda part: pl.BlockSpec((tq, width), lambda r, i: (i, r * per_res + part))
    halo = lambda part: pl.BlockSpec(
        (BAND, width), lambda r, i: (jnp.maximum(i * halo_per_tile - 1, 0), r * per_res + part))
    out_spec = pl.BlockSpec((tq, width), lambda r, i: (i, r))
    return pl.pallas_call(
        functools.partial(_dilated_kernel, tq=tq),
        grid=(dil, L // tq),
        in_specs=[cur(0), cur(1), halo(1), cur(2), halo(2)],
        out_specs=[out_spec, out_spec],
        out_shape=[jax.ShapeDtypeStruct((L, dil * width), F32)] * 2,
        scratch_shapes=[pltpu.VMEM((tq + BAND, width), BF16)] * 2,
        compiler_params=_params(("parallel", "arbitrary")),
        name=f"dilated_d{dil}",
    )(view, view, view, view, view)


ONES_ROWS = BF16_SUBLANES


COL_TILE = 256
STRIPS_PER_TRIP = 4


def _flash_pair_kernel(qi_ref, kj_ref, qt_ref, k_ref, *rest, mode, tq, tk, gps, lam_init):
    if mode == "fox":
        cf_ref, vt_ref, o_ref, wq, m_scr, acc, kx_scr, vx_scr, *bufs = rest
    else:
        vt_ref, lam_ref, g_ref, o_ref, wq, m_scr, acc, kx_scr, vx_scr, *bufs = rest
    s_buf, mx_buf, p_buf, al_buf = bufs[0:2], bufs[2:4], bufs[4:6], bufs[6:8]
    U = STRIPS_PER_TRIP
    step = pl.program_id(1)
    qi = qi_ref[step]
    kj = kj_ref[step]
    last_kj = (qi * tq + tq - 1) // tk
    rows = acc.shape[1]
    dv = rows - ONES_ROWS
    n_ct = tq // COL_TILE
    n_strips = 2 * gps * n_ct
    n_trips = n_strips // U
    par = kj % 2

    @pl.when(jnp.logical_and(pl.program_id(0) == 0, step == 0))
    def _():
        for buf in bufs:
            buf[...] = jnp.zeros_like(buf)
        vx_scr[...] = jnp.zeros_like(vx_scr)

    @pl.when(kj == 0)
    def _():
        r = lax.broadcasted_iota(jnp.int32, (LANES, COL_TILE), 0)
        first = r < HEAD_DIM
        for gi in range(gps):
            for a in range(2):
                keep_q = first if a == 0 else jnp.logical_not(first)
                if mode == "fox":
                    head = 2 * (pl.program_id(0) * gps + gi) + a
                    pick = (r - head) == 0
                    for term in range(1, GATE_TERMS):
                        pick = pick | ((r - head) == term * GATE_STRIDE)
                    gate_rows = jnp.where(pick, -1.0, 0.0).astype(BF16)
                for ct in range(n_ct):
                    t = (2 * gi + a) * n_ct + ct
                    qt = qt_ref[gi * LANES:(gi + 1) * LANES, ct * COL_TILE:(ct + 1) * COL_TILE]
                    wq[t, 0:LANES] = jnp.where(keep_q, qt, jnp.zeros_like(qt))
                    if mode == "fox":
                        wq[t, LANES:2 * LANES] = gate_rows
        m_scr[...] = jnp.full_like(m_scr, NEG)
        acc[...] = jnp.zeros_like(acc)

    ones = jnp.ones((ONES_ROWS, tk), BF16)
    for gi in range(gps):
        kx_scr[gi, :, 0:LANES] = k_ref[:, gi * LANES:(gi + 1) * LANES]
        if mode == "fox":
            kx_scr[gi, :, LANES:2 * LANES] = cf_ref[...]
        for a in range(2):
            if mode == "fox":
                v0 = gi * LANES + a * HEAD_DIM
                vx_scr[par, 2 * gi + a, 0:dv] = vt_ref[v0:v0 + HEAD_DIM, :]
            else:
                vx_scr[par, 2 * gi + a, 0:dv] = vt_ref[gi * LANES:(gi + 1) * LANES, :]
            vx_scr[par, 2 * gi + a, dv:rows] = ones

    def stage_a(j, slot, u, masked):
        s = jnp.dot(kx_scr[j // (2 * n_ct)], wq[j], preferred_element_type=F32)
        if masked:
            key = kj * tk + lax.broadcasted_iota(jnp.int32, (tk, COL_TILE), 0)
            qry = (qi * tq + (j % n_ct) * COL_TILE
                   + lax.broadcasted_iota(jnp.int32, (tk, COL_TILE), 1))
            s = jnp.where(key <= qry, s, NEG)
        s_buf[slot][u] = s
        mx_buf[slot][u] = jnp.broadcast_to(jnp.max(s, axis=0, keepdims=True), (8, COL_TILE))


    def stage_b(j0, slot, valid, us):
        m_prevs = {u: m_scr[j0 + u] for u in us}
        for u, m_prev in list(m_prevs.items()):
            m_next = jnp.maximum(m_prev, mx_buf[slot][u])
            if valid is not None:
                m_next = jnp.where(valid, m_next, m_prev)
            al_buf[slot][u] = jnp.exp2(m_prev - m_next)
            s3 = s_buf[slot][u].reshape(tk // 8, 8, COL_TILE)
            p_buf[slot][u] = jnp.exp2(s3 - m_next[None]).reshape(tk, COL_TILE).astype(BF16)
            m_prevs[u] = m_next
        for u, m_next in m_prevs.items():
            m_scr[j0 + u] = m_next

    def stage_c(j0, slot, vpar, valid, us):
        pvs = [jnp.dot(vx_scr[vpar, (j0 + u) // n_ct], p_buf[slot][u],
                       preferred_element_type=F32) for u in us]
        olds = [acc[j0 + u] for u in us]
        for u, old, pv in zip(us, olds, pvs):
            new = ((old.reshape(rows // 8, 8, COL_TILE) * al_buf[slot][u][None])
                   .reshape(rows, COL_TILE) + pv)
            acc[j0 + u] = new if valid is None else jnp.where(valid, new, old)

    def sweep(masked):
        def trip(k, slot):
            later_step = kj > 0
            kc = (k + n_trips - 2) % n_trips
            kb = (k + n_trips - 1) % n_trips
            valid_c = jnp.logical_or(later_step, k >= 2)
            valid_b = jnp.logical_or(later_step, k >= 1)
            vpar = jnp.where(k >= 2, par, 1 - par)
            for u in range(U):
                stage_a(k * U + u, slot, u, masked)
            stage_b(kb * U, 1 - slot, valid_b, range(U))
            stage_c(kc * U, slot, vpar, valid_c, range(U))

        def trip_pair(half, carry):
            trip(2 * half, 0)
            trip(2 * half + 1, 1)
            return carry

        if n_trips == 2:
            trip_pair(0, 0)
        else:
            lax.fori_loop(0, n_trips // 2, trip_pair, 0)

    needs_mask = kj * tk + tk - 1 > qi * tq
    pl.when(needs_mask)(lambda: sweep(True))
    pl.when(jnp.logical_not(needs_mask))(lambda: sweep(False))

    @pl.when(kj == last_kj)
    def _():
        last = n_trips - 1
        stage_c((last - 1) * U, (last - 1) % 2, par, None, range(U))
        stage_b(last * U, last % 2, None, range(U))
        stage_c(last * U, last % 2, par, None, range(U))

        def map_out(mi):
            parts = [acc[mi * n_ct + ct] for ct in range(n_ct)]
            full = jnp.concatenate(parts, axis=1)
            return full[0:dv] / full[dv:dv + 1]

        for gi in range(gps):
            o0, o1 = map_out(2 * gi), map_out(2 * gi + 1)
            if mode == "fox":
                o = jnp.concatenate([o0, o1], axis=0)
            else:
                lp = lam_ref[...]
                t1 = jnp.sum(lp[0:1] * lp[1:2], axis=1, keepdims=True)
                t2 = jnp.sum(lp[2:3] * lp[3:4], axis=1, keepdims=True)
                lam = jnp.exp(t1) - jnp.exp(t2) + lam_init
                o = o0 - lam * o1
                ms = jnp.mean(o * o, axis=0, keepdims=True)
                o = o * lax.rsqrt(ms + SUBLN_EPS) * g_ref[...] * (1.0 - lam_init)
            o_ref[:, gi * LANES:(gi + 1) * LANES] = o.T.astype(BF16)


def _causal_steps(s, tq, tk):
    qi, kj = [], []
    for i in range(s // tq):
        for j in range((i * tq + tq - 1) // tk + 1):
            qi.append(i)
            kj.append(j)
    return jnp.asarray(qi, jnp.int32), jnp.asarray(kj, jnp.int32)


def _flash_pair(proj_t, keys, extras, *, mode, n_groups, gps, q_row, v_row, k_col, tq, tk,
                lam_init=0.0):
    s = keys.shape[0]
    qi, kj = _causal_steps(s, tq, tk)
    gw = gps * LANES
    qt_spec = pl.BlockSpec((gw, tq), lambda g, t, qi, kj: (q_row + g, qi[t]))
    k_spec = pl.BlockSpec((tk, gw), lambda g, t, qi, kj: (kj[t], k_col + g))
    vt_spec = pl.BlockSpec((gw, tk), lambda g, t, qi, kj: (v_row + g, kj[t]))
    const = lambda x: pl.BlockSpec(x.shape, lambda g, t, qi, kj: (0, 0))
    if mode == "fox":
        (cfeat,) = extras
        in_specs = [qt_spec, k_spec,
                    pl.BlockSpec((tk, LANES), lambda g, t, qi, kj: (kj[t], 0)), vt_spec]
        args = (proj_t, keys, cfeat, proj_t)
        kd, dv = 2 * LANES, HEAD_DIM
    else:
        lam_params, subln_g = extras
        in_specs = [qt_spec, k_spec, vt_spec, const(lam_params), const(subln_g)]
        args = (proj_t, keys, proj_t, lam_params, subln_g)
        kd, dv = LANES, LANES
    rows = dv + ONES_ROWS
    n_strips = 2 * gps * (tq // COL_TILE)
    u = STRIPS_PER_TRIP
    assert n_strips % (2 * u) == 0
    kern = functools.partial(_flash_pair_kernel, mode=mode, tq=tq, tk=tk, gps=gps,
                             lam_init=lam_init)
    return pl.pallas_call(
        kern,
        grid_spec=pltpu.PrefetchScalarGridSpec(
            num_scalar_prefetch=2,
            grid=(n_groups // gps, qi.shape[0]),
            in_specs=in_specs,
            out_specs=pl.BlockSpec((tq, gw), lambda g, t, qi, kj: (qi[t], g)),
            scratch_shapes=[
                pltpu.VMEM((n_strips, kd, COL_TILE), BF16),
                pltpu.VMEM((n_strips, 8, COL_TILE), F32),
                pltpu.VMEM((n_strips, rows, COL_TILE), F32),
                pltpu.VMEM((gps, tk, kd), BF16),
                pltpu.VMEM((2, 2 * gps, rows, tk), BF16),
            ] + [pltpu.VMEM((u, tk, COL_TILE), F32)] * 2
              + [pltpu.VMEM((u, 8, COL_TILE), F32)] * 2
              + [pltpu.VMEM((u, tk, COL_TILE), BF16)] * 2
              + [pltpu.VMEM((u, 8, COL_TILE), F32)] * 2,
        ),
        out_shape=jax.ShapeDtypeStruct((s, n_groups * LANES), BF16),
        compiler_params=_params(("arbitrary", "arbitrary")),
        name=f"flash_{mode}",
    )(qi, kj, *args)


def _hyb_out_kernel(*refs, dilations):
    n_pat = len(dilations)
    pat = refs[:2 * n_pat]
    ob_ref, w_ref, h_ref, out_ref = refs[2 * n_pat:2 * n_pat + 4]
    scratch = list(refs[2 * n_pat + 4:])
    tm = h_ref.shape[0]
    vals = []
    for idx, ref in enumerate(pat):
        dil = dilations[idx // 2]
        if dil == 1:
            vals.append(ref[...])
            continue
        buf = scratch.pop(0)
        n_chunks = buf.shape[0]
        for r in range(dil):
            for c in range(n_chunks):
                c0 = (r * n_chunks + c) * LANES
                buf[c, pl.ds(r, tm // dil, stride=dil), :] = ref[:, c0:c0 + LANES]
        vals.append(jnp.concatenate([buf[c] for c in range(n_chunks)], axis=1))
    os_, ls_ = vals[0::2], vals[1::2]
    m = functools.reduce(jnp.maximum, ls_)
    es = [jnp.exp(l - m) for l in ls_]
    oa = sum(e * o for e, o in zip(es, os_)) / sum(es)
    wa = oa.shape[1]
    acc = jnp.dot(oa.astype(BF16), w_ref[0:wa, :], preferred_element_type=F32)
    acc = acc + jnp.dot(ob_ref[...], w_ref[wa:, :], preferred_element_type=F32)
    out_ref[...] = h_ref[...] + acc


def _hyb_out(pattern_outs, dilations, ob, w, h, *, tm):
    s, d = h.shape
    wa = pattern_outs[0][0].shape[1] // dilations[0]
    in_specs, args, scratch = [], [], []
    for (o, lse), dil in zip(pattern_outs, dilations):
        for arr in (o, lse):
            in_specs.append(pl.BlockSpec((tm // dil, dil * wa), lambda i: (i, 0)))
            args.append(arr)
            if dil > 1:
                scratch.append(pltpu.VMEM((wa // LANES, tm, LANES), F32))
    return pl.pallas_call(
        functools.partial(_hyb_out_kernel, dilations=tuple(dilations)),
        grid=(s // tm,),
        in_specs=in_specs + [
            pl.BlockSpec((tm, ob.shape[1]), lambda i: (i, 0)),
            pl.BlockSpec(w.shape, lambda i: (0, 0)),
            pl.BlockSpec((tm, d), lambda i: (i, 0)),
        ],
        out_specs=pl.BlockSpec((tm, d), lambda i: (i, 0)),
        out_shape=jax.ShapeDtypeStruct((s, d), F32),
        scratch_shapes=scratch,
        compiler_params=_params(("parallel",)),
        name="hyb_out",
    )(*args, ob, w, h)


def _proj_res_kernel(a_ref, w_ref, h_ref, out_ref):
    out_ref[...] = h_ref[...] + jnp.dot(a_ref[...], w_ref[...], preferred_element_type=F32)


def _proj_res(a, w, h, *, tm):
    s, d = h.shape
    return pl.pallas_call(
        _proj_res_kernel,
        grid=(s // tm,),
        in_specs=[
            pl.BlockSpec((tm, a.shape[1]), lambda i: (i, 0)),
            pl.BlockSpec(w.shape, lambda i: (0, 0)),
            pl.BlockSpec((tm, d), lambda i: (i, 0)),
        ],
        out_specs=pl.BlockSpec((tm, d), lambda i: (i, 0)),
        out_shape=jax.ShapeDtypeStruct((s, d), F32),
        compiler_params=_params(("parallel",)),
        name="proj_res",
    )(a, w, h)


HALO = BF16_SUBLANES


FFN_SUB = 256


def _ffn_kernel(h_ref, halo_ref, g_ref, wg_ref, wu_ref, cw_ref, cb_ref, wd_ref, *rest,
                tm, final):
    if final:
        fg_ref, out_ref, n_scr, gate_scr, act_scr = rest
    else:
        out_ref, n_scr, gate_scr, act_scr = rest
    i = pl.program_id(0)
    j = pl.program_id(1)
    n_chunks, _, chunk = act_scr.shape

    @pl.when(j == 0)
    def _():
        g = g_ref[...]
        prev = jnp.where(i > 0, halo_ref[...], 0.0)
        n_scr[0:HALO] = _rms(prev, g, NORM_EPS).astype(BF16)
        n_scr[HALO:] = _rms(h_ref[...], g, NORM_EPS).astype(BF16)

    for t, c0 in enumerate(range(0, chunk, FFN_SUB)):
        w = min(FFN_SUB, chunk - c0)
        cols = slice(c0, c0 + w)
        gate_scr[t, :, 0:w] = jnp.dot(n_scr[...], wg_ref[:, cols], preferred_element_type=F32)
        up = jnp.dot(n_scr[HALO:], wu_ref[:, cols], preferred_element_type=F32)
        conv = cb_ref[:, cols]
        for k in range(CONV_WIDTH):
            start = HALO - (CONV_WIDTH - 1) + k
            conv = conv + gate_scr[t, start:start + tm, 0:w] * cw_ref[k:k + 1, cols]
        act_scr[j, :, cols] = (conv * (1.0 / (1.0 + jnp.exp(-conv))) * up).astype(BF16)

    @pl.when(j == n_chunks - 1)
    def _():
        act = jnp.concatenate([act_scr[t] for t in range(n_chunks)], axis=1)
        y = h_ref[...] + jnp.dot(act, wd_ref[...], preferred_element_type=F32)
        if final:
            y = _rms(y, fg_ref[...], NORM_EPS)
        out_ref[...] = y


def _ffn(h, g, w_up, conv_w, conv_b, w_down, final_g, *, tm, n_chunks):
    s, d = h.shape
    d_ff = w_down.shape[0]
    chunk = d_ff // n_chunks
    assert chunk * n_chunks == d_ff and chunk % LANES == 0
    n_sub = -(-chunk // FFN_SUB)
    final = final_g is not None
    once = pl.Buffered(1)
    up_mode = once if n_chunks == 1 else None
    in_specs = [
        pl.BlockSpec((tm, d), lambda i, j: (i, 0)),
        pl.BlockSpec((HALO, d), lambda i, j: (jnp.maximum(i * (tm // HALO) - 1, 0), 0)),
        pl.BlockSpec((1, d), lambda i, j: (0, 0)),
        pl.BlockSpec((d, chunk), lambda i, j: (0, j), pipeline_mode=up_mode),
        pl.BlockSpec((d, chunk), lambda i, j: (0, n_chunks + j), pipeline_mode=up_mode),
        pl.BlockSpec((CONV_WIDTH, chunk), lambda i, j: (0, j)),
        pl.BlockSpec((1, chunk), lambda i, j: (0, j)),
        pl.BlockSpec((d_ff, d), lambda i, j: (0, 0), pipeline_mode=once),
    ]
    args = [h, h, g, w_up, w_up, conv_w, conv_b, w_down]
    if final:
        in_specs.append(pl.BlockSpec((1, d), lambda i, j: (0, 0)))
        args.append(final_g)
    return pl.pallas_call(
        functools.partial(_ffn_kernel, tm=tm, final=final),
        grid=(s // tm, n_chunks),
        in_specs=in_specs,
        out_specs=pl.BlockSpec((tm, d), lambda i, j: (i, 0)),
        out_shape=jax.ShapeDtypeStruct((s, d), F32),
        scratch_shapes=[
            pltpu.VMEM((tm + HALO, d), BF16),
            pltpu.VMEM((n_sub, tm + HALO, FFN_SUB), F32),
            pltpu.VMEM((n_chunks, tm, chunk), BF16),
        ],
        compiler_params=_params(("parallel", "arbitrary")),
        name="ffn_final" if final else "ffn",
    )(*args)


def _rope_tables(s):
    inv = 1.0 / (ROPE_THETA ** (jnp.arange(0, HEAD_DIM, 2, dtype=F32) / HEAD_DIM))
    ang = jnp.arange(s, dtype=F32)[:, None] * inv[None, :]
    cos, sin = jnp.cos(ang), jnp.sin(ang)
    sign = jnp.where((jnp.arange(LANES) & 32) == 0, -1.0, 1.0).astype(F32)
    cos_l = jnp.tile(cos, (1, LANES // 32))
    sin_signed = jnp.tile(sin, (1, LANES // 32)) * sign[None, :]
    return cos_l, sin_signed, cos.T, sin.T


def kernel(x, attn_norm, ffn_norm, final_norm, hyb_w_in, hyb_b_f, hyb_w_out, diff_w_qkv,
           diff_lambda, diff_subln, diff_w_out, ffn_w_up, ffn_conv_w, ffn_conv_b, ffn_w_down):
    b, s, d = x.shape
    assert b == 1
    depth = attn_norm.shape[0]
    width = hyb_w_out.shape[1] // 2
    n_pairs = width // LANES
    n_heads_b = width // HEAD_DIM
    n_diff_heads = diff_w_out.shape[1] // LANES
    dq = n_diff_heads * LANES
    tm = min(1024, s)
    ts = min(512, s)
    tq_flash, tk_flash = min(1024, s), min(512, s)
    gps = 4
    q_scale = HEAD_DIM ** -0.5
    cos, sin_signed, cos_t, sin_t = _rope_tables(s)
    h = x[0]

    for l in range(depth):
        g_attn = attn_norm[l][None, :]
        if l % 2 == 0:
            e = l // 2
            w_in = hyb_w_in[e]
            qa_ka_va, qb, kb, vb, wf = (w_in[:, :3 * width], w_in[:, 3 * width:4 * width],
                                        w_in[:, 4 * width:5 * width], w_in[:, 5 * width:6 * width],
                                        w_in[:, 6 * width:])
            dils = tuple(dil for _, dil in DILATED_PATTERNS)
            proj, *views = _norm_proj(
                h, g_attn, jnp.concatenate([qa_ka_va, kb], axis=1).astype(BF16), cos, sin_signed,
                rope_tiles=(0, 1), scale_tiles=(0,), scale=q_scale, tm=tm, tn=PROJ_TILE,
                strided_tiles=3, dilations=dils[1:])
            proj_t = _norm_proj_t(h, g_attn, jnp.concatenate([qb, vb], axis=1).T.astype(BF16),
                                  cos_t, sin_t, rope_tiles=(), scale_tiles=(0,),
                                  scale=q_scale * LOG2E, tm=tm, tn=PROJ_TILE)
            wf_pad = jnp.pad(wf, ((0, 0), (0, LANES - n_heads_b))).astype(BF16)
            bf_pad = jnp.pad(hyb_b_f[e], (0, LANES - n_heads_b))[None, :]
            cfeat = _fox_gate(h, g_attn, wf_pad, bf_pad, tc=ts, n_heads=n_heads_b)
            pats = [_dilated_pattern(proj, 1, width=width, per_res=4, tq=ts)]
            pats += [_dilated_pattern(v, dil, width=width, per_res=3, tq=min(512, s // dil))
                     for v, dil in zip(views, dils[1:])]
            ob = _flash_pair(proj_t, proj, (cfeat,), mode="fox", n_groups=n_pairs, gps=gps,
                             q_row=0, v_row=n_pairs // gps, k_col=3 * n_pairs // gps,
                             tq=tq_flash, tk=tk_flash)
            h = _hyb_out(pats, dils, ob, hyb_w_out[e].astype(BF16), h, tm=ts)
        else:
            o = l // 2
            w = diff_w_qkv[o]
            wq, wk, wv = w[:, :dq], w[:, dq:2 * dq], w[:, 2 * dq:]
            k_tiles = dq // PROJ_TILE
            keys = _norm_proj(h, g_attn, wk.astype(BF16), cos, sin_signed,
                              rope_tiles=tuple(range(k_tiles)), scale_tiles=(),
                              scale=1.0, tm=tm, tn=PROJ_TILE)
            proj_t = _norm_proj_t(h, g_attn, jnp.concatenate([wq, wv], axis=1).T.astype(BF16),
                                  cos_t, sin_t, rope_tiles=tuple(range(k_tiles)),
                                  scale_tiles=tuple(range(k_tiles)), scale=q_scale * LOG2E,
                                  tm=tm, tn=PROJ_TILE)
            lam_init = 0.8 - 0.6 * math.exp(-0.3 * l)
            att = _flash_pair(proj_t, keys, (diff_lambda[o], diff_subln[o][:, None]),
                              mode="diff", n_groups=n_diff_heads, gps=gps, q_row=0,
                              v_row=n_diff_heads // gps, k_col=0, tq=tq_flash, tk=tk_flash,
                              lam_init=lam_init)
            h = _proj_res(att, diff_w_out[o].astype(BF16), h, tm=tm)
        h = _ffn(h, ffn_norm[l][None, :], ffn_w_up[l].astype(BF16), ffn_conv_w[l],
                 ffn_conv_b[l][None, :], ffn_w_down[l].astype(BF16),
                 final_norm[None, :] if l == depth - 1 else None, tm=ts, n_chunks=1)
    return h[None]
```

```python
import functools
import math

import jax
import jax.numpy as jnp
from jax import lax
from jax.experimental import pallas as pl
from jax.experimental.pallas import tpu as pltpu

F32 = jnp.float32
BF16 = jnp.bfloat16

HEAD_DIM = 64
LANES = 128
BF16_SUBLANES = 16
ROPE_THETA = 10000.0
DILATED_PATTERNS = ((128, 1), (512, 4), (2048, 16))
BAND = 128
NORM_EPS = 1e-6
SUBLN_EPS = 1e-5
CONV_WIDTH = 3
NEG = -1e30
VMEM_LIMIT = 48 * 1024 * 1024
PROJ_TILE = 512

NT_DIMS = (((1,), (1,)), ((), ()))


def _params(sem):
    return pltpu.CompilerParams(dimension_semantics=sem, vmem_limit_bytes=VMEM_LIMIT)


def _rms(x, g, eps):
    return x * lax.rsqrt(jnp.mean(x * x, axis=-1, keepdims=True) + eps) * g


def _lane_lo(rows):
    return lax.broadcasted_iota(jnp.int32, (rows, LANES), 1) < HEAD_DIM


def _rope_tile(x, cos, sin_signed):
    rows = x.shape[0]
    first_half = (lax.broadcasted_iota(jnp.int32, (rows, LANES), 1) & 32) == 0
    out = []
    for c in range(x.shape[1] // LANES):
        xc = x[:, c * LANES:(c + 1) * LANES]
        ahead = pltpu.roll(xc, LANES - 32, 1)
        behind = pltpu.roll(xc, 32, 1)
        rot = jnp.where(first_half, ahead, behind)
        out.append(xc * cos + rot * sin_signed)
    return jnp.concatenate(out, axis=1)


def _norm_proj_kernel(h_ref, g_ref, w_ref, cos_ref, sin_ref, o_ref, *rest,
                      rope_tiles, scale_tiles, scale, tn, strided_tiles, dilations):
    n = _rms(h_ref[...], g_ref[...], NORM_EPS).astype(BF16)
    tm = h_ref.shape[0]
    for j in range(w_ref.shape[1] // tn):
        cols = slice(j * tn, (j + 1) * tn)
        acc = jnp.dot(n, w_ref[:, cols], preferred_element_type=F32)
        if j in rope_tiles:
            acc = _rope_tile(acc, cos_ref[...], sin_ref[...])
        if j in scale_tiles:
            acc = acc * scale
        o_ref[:, cols] = acc.astype(BF16)
        if j < strided_tiles:
            stage = rest[-1]
            for c in range(tn // LANES):
                stage[c] = acc[:, c * LANES:(c + 1) * LANES]
            for d_ref, dil in zip(rest[:-1], dilations):
                for r in range(dil):
                    for c in range(tn // LANES):
                        c0 = (r * strided_tiles + j) * tn + c * LANES
                        rows = stage[c, pl.ds(r, tm // dil, stride=dil), :]
                        d_ref[:, c0:c0 + LANES] = rows.astype(BF16)


def _norm_proj(h, g, w, cos, sin_signed, *, rope_tiles, scale_tiles, scale, tm, tn,
               strided_tiles=0, dilations=()):
    s, d = h.shape
    n = w.shape[1]
    kern = functools.partial(_norm_proj_kernel, rope_tiles=rope_tiles, scale_tiles=scale_tiles,
                             scale=scale, tn=tn, strided_tiles=strided_tiles,
                             dilations=dilations)
    out_specs = [pl.BlockSpec((tm, n), lambda i: (i, 0))]
    out_shape = [jax.ShapeDtypeStruct((s, n), BF16)]
    for dil in dilations:
        wide = dil * strided_tiles * tn
        out_specs.append(pl.BlockSpec((tm // dil, wide), lambda i: (i, 0)))
        out_shape.append(jax.ShapeDtypeStruct((s // dil, wide), BF16))
    outs = pl.pallas_call(
        kern,
        grid=(s // tm,),
        in_specs=[
            pl.BlockSpec((tm, d), lambda i: (i, 0)),
            pl.BlockSpec((1, d), lambda i: (0, 0)),
            pl.BlockSpec((d, n), lambda i: (0, 0)),
            pl.BlockSpec((tm, LANES), lambda i: (i, 0)),
            pl.BlockSpec((tm, LANES), lambda i: (i, 0)),
        ],
        out_specs=out_specs,
        out_shape=out_shape,
        scratch_shapes=[pltpu.VMEM((tn // LANES, tm, LANES), F32)] if dilations else [],
        compiler_params=_params(("parallel",)),
        name="norm_proj",
    )(h, g, w, cos, sin_signed)
    return outs if dilations else outs[0]


def _norm_proj_t_kernel(h_ref, g_ref, wt_ref, cos_ref, sin_ref, o_ref, *,
                        rope_tiles, scale_tiles, scale, tn):
    n = _rms(h_ref[...], g_ref[...], NORM_EPS).astype(BF16)
    half = HEAD_DIM // 2
    for j in range(wt_ref.shape[0] // tn):
        acc = lax.dot_general(wt_ref[j * tn:(j + 1) * tn, :], n, NT_DIMS,
                              preferred_element_type=F32)
        sc = scale if j in scale_tiles else 1.0
        if j in rope_tiles:
            cos, sin = cos_ref[...], sin_ref[...]
            for hd in range(tn // HEAD_DIM):
                r0 = j * tn + hd * HEAD_DIM
                x1 = acc[hd * HEAD_DIM:hd * HEAD_DIM + half]
                x2 = acc[hd * HEAD_DIM + half:(hd + 1) * HEAD_DIM]
                o_ref[r0:r0 + half] = ((x1 * cos - x2 * sin) * sc).astype(BF16)
                o_ref[r0 + half:r0 + HEAD_DIM] = ((x2 * cos + x1 * sin) * sc).astype(BF16)
        else:
            o_ref[j * tn:(j + 1) * tn] = (acc * sc).astype(BF16)


def _norm_proj_t(h, g, wt, cos_t, sin_t, *, rope_tiles, scale_tiles, scale, tm, tn):
    s, d = h.shape
    n = wt.shape[0]
    kern = functools.partial(_norm_proj_t_kernel, rope_tiles=rope_tiles,
                             scale_tiles=scale_tiles, scale=scale, tn=tn)
    return pl.pallas_call(
        kern,
        grid=(s // tm,),
        in_specs=[
            pl.BlockSpec((tm, d), lambda i: (i, 0)),
            pl.BlockSpec((1, d), lambda i: (0, 0)),
            pl.BlockSpec((n, d), lambda i: (0, 0)),
            pl.BlockSpec((HEAD_DIM // 2, tm), lambda i: (0, i)),
            pl.BlockSpec((HEAD_DIM // 2, tm), lambda i: (0, i)),
        ],
        out_specs=pl.BlockSpec((n, tm), lambda i: (0, i)),
        out_shape=jax.ShapeDtypeStruct((n, s), BF16),
        compiler_params=_params(("parallel",)),
        name="norm_proj_t",
    )(h, g, wt, cos_t, sin_t)


GATE_TERMS = 3
GATE_STRIDE = 8
LOG2E = math.log2(math.e)


def _fox_gate_kernel(h_ref, g_ref, wf_ref, bf_ref, cf_ref, carry, *, tc, n_heads):
    i = pl.program_id(0)

    @pl.when(i == 0)
    def _():
        carry[...] = jnp.zeros_like(carry)

    n = _rms(h_ref[...], g_ref[...], NORM_EPS).astype(BF16)
    z = jnp.dot(n, wf_ref[...], preferred_element_type=F32) + bf_ref[...]
    logf = jnp.minimum(z, 0.0) - jnp.log(1.0 + jnp.exp(-jnp.abs(z)))
    dst = lax.broadcasted_iota(jnp.int32, (tc, tc), 0)
    src = lax.broadcasted_iota(jnp.int32, (tc, tc), 1)
    prefix = jnp.where(src <= dst, 1.0, 0.0).astype(F32)
    cs = jnp.dot(prefix, logf, precision=lax.Precision.HIGHEST,
                 preferred_element_type=F32) + carry[0:1, :]
    carry[...] = jnp.broadcast_to(cs[tc - 1:tc, :], carry.shape)
    lane = lax.broadcasted_iota(jnp.int32, (tc, LANES), 1)
    rem = jnp.where(lane < n_heads, cs * LOG2E, 0.0)
    feat = jnp.zeros_like(rem)
    for term in range(GATE_TERMS):
        part = rem.astype(BF16).astype(F32)
        rem = rem - part
        feat = feat + (pltpu.roll(part, term * GATE_STRIDE, 1) if term else part)
    cf_ref[...] = feat.astype(BF16)


def _fox_gate(h, g, wf, b_f, *, tc, n_heads):
    s, d = h.shape
    return pl.pallas_call(
        functools.partial(_fox_gate_kernel, tc=tc, n_heads=n_heads),
        grid=(s // tc,),
        in_specs=[
            pl.BlockSpec((tc, d), lambda i: (i, 0)),
            pl.BlockSpec((1, d), lambda i: (0, 0)),
            pl.BlockSpec((d, LANES), lambda i: (0, 0)),
            pl.BlockSpec((1, LANES), lambda i: (0, 0)),
        ],
        out_specs=pl.BlockSpec((tc, LANES), lambda i: (i, 0)),
        out_shape=jax.ShapeDtypeStruct((s, LANES), BF16),
        scratch_shapes=[pltpu.VMEM((8, LANES), F32)],
        compiler_params=_params(("arbitrary",)),
        name="fox_gate",
    )(h, g, wf, b_f)


def _dilated_kernel(q_ref, kc_ref, kh_ref, vc_ref, vh_ref, o_ref, l_ref, kbuf, vbuf, *, tq):
    i = pl.program_id(1)
    kbuf[0:BAND] = kh_ref[...]
    kbuf[BAND:] = kc_ref[...]
    vbuf[0:BAND] = vh_ref[...]
    vbuf[BAND:] = vc_ref[...]
    row = lax.broadcasted_iota(jnp.int32, (BAND, 2 * BAND), 0)
    col = lax.broadcasted_iota(jnp.int32, (BAND, 2 * BAND), 1)
    delta = row - col + BAND
    band = (delta >= 0) & (delta <= BAND)
    lo = _lane_lo(BAND)
    for a in range(tq // BAND):
        first_key = i * tq + (a - 1) * BAND
        valid = band & (col + first_key >= 0)
        rows = slice(a * BAND, (a + 1) * BAND)
        for hp in range(q_ref.shape[1] // LANES):
            lanes = slice(hp * LANES, (hp + 1) * LANES)
            q = q_ref[rows, lanes]
            kk = kbuf[a * BAND:(a + 2) * BAND, lanes]
            vv = vbuf[a * BAND:(a + 2) * BAND, lanes]
            outs, lses = [], []
            for qh in (jnp.where(lo, q, jnp.zeros_like(q)), jnp.where(lo, jnp.zeros_like(q), q)):
                s = lax.dot_general(qh, kk, NT_DIMS, preferred_element_type=F32)
                s = jnp.where(valid, s, NEG)
                m = jnp.max(s, axis=1, keepdims=True)
                p = jnp.exp(s - m)
                den = jnp.sum(p, axis=1, keepdims=True)
                pv = jnp.dot(p.astype(BF16), vv, preferred_element_type=F32)
                outs.append(pv / den)
                lses.append(jnp.broadcast_to(m + jnp.log(den), (BAND, LANES)))
            o_ref[rows, lanes] = jnp.where(lo, outs[0], outs[1])
            l_ref[rows, lanes] = jnp.where(lo, lses[0], lses[1])


def _dilated_pattern(view, dil, *, width, per_res, tq):
    L = view.shape[0]
    halo_per_tile = tq // BAND
    cur = lambda part: pl.BlockSpec((tq, width), lambda r, i: (i, r * per_res + part))
    halo = lambda part: pl.BlockSpec(
        (BAND, width), lambda r, i: (jnp.maximum(i * halo_per_tile - 1, 0), r * per_res + part))
    out_spec = pl.BlockSpec((tq, width), lambda r, i: (i, r))
    return pl.pallas_call(
        functools.partial(_dilated_kernel, tq=tq),
        grid=(dil, L // tq),
        in_specs=[cur(0), cur(1), halo(1), cur(2), halo(2)],
        out_specs=[out_spec, out_spec],
        out_shape=[jax.ShapeDtypeStruct((L, dil * width), F32)] * 2,
        scratch_shapes=[pltpu.VMEM((tq + BAND, width), BF16)] * 2,
        compiler_params=_params(("parallel", "arbitrary")),
        name=f"dilated_d{dil}",
    )(view, view, view, view, view)


ONES_ROWS = BF16_SUBLANES


COL_TILE = 256
STRIPS_PER_TRIP = 4


def _flash_pair_kernel(qi_ref, kj_ref, qt_ref, k_ref, *rest, mode, tq, tk, gps, lam_init):
    if mode == "fox":
        cf_ref, vt_ref, o_ref, wq, m_scr, acc, kx_scr, vx_scr, *bufs = rest
    else:
        vt_ref, lam_ref, g_ref, o_ref, wq, m_scr, acc, kx_scr, vx_scr, *bufs = rest
    s_buf, mx_buf, p_buf, al_buf = bufs[0:2], bufs[2:4], bufs[4:6], bufs[6:8]
    U = STRIPS_PER_TRIP
    step = pl.program_id(1)
    qi = qi_ref[step]
    kj = kj_ref[step]
    last_kj = (qi * tq + tq - 1) // tk
    rows = acc.shape[1]
    dv = rows - ONES_ROWS
    n_ct = tq // COL_TILE
    n_strips = 2 * gps * n_ct
    n_trips = n_strips // U
    par = kj % 2

    @pl.when(jnp.logical_and(pl.program_id(0) == 0, step == 0))
    def _():
        for buf in bufs:
            buf[...] = jnp.zeros_like(buf)
        vx_scr[...] = jnp.zeros_like(vx_scr)

    @pl.when(kj == 0)
    def _():
        r = lax.broadcasted_iota(jnp.int32, (LANES, COL_TILE), 0)
        first = r < HEAD_DIM
        for gi in range(gps):
            for a in range(2):
                keep_q = first if a == 0 else jnp.logical_not(first)
                if mode == "fox":
                    head = 2 * (pl.program_id(0) * gps + gi) + a
                    pick = (r - head) == 0
                    for term in range(1, GATE_TERMS):
                        pick = pick | ((r - head) == term * GATE_STRIDE)
                    gate_rows = jnp.where(pick, -1.0, 0.0).astype(BF16)
                for ct in range(n_ct):
                    t = (2 * gi + a) * n_ct + ct
                    qt = qt_ref[gi * LANES:(gi + 1) * LANES, ct * COL_TILE:(ct + 1) * COL_TILE]
                    wq[t, 0:LANES] = jnp.where(keep_q, qt, jnp.zeros_like(qt))
                    if mode == "fox":
                        wq[t, LANES:2 * LANES] = gate_rows
        m_scr[...] = jnp.full_like(m_scr, NEG)
        acc[...] = jnp.zeros_like(acc)

    maps_per_v = 1 if mode == "fox" else 2
    ones = jnp.ones((ONES_ROWS, tk), BF16)
    for gi in range(gps):
        kx_scr[gi, :, 0:LANES] = k_ref[:, gi * LANES:(gi + 1) * LANES]
        if mode == "fox":
            kx_scr[gi, :, LANES:2 * LANES] = cf_ref[...]
            for a in range(2):
                v0 = gi * LANES + a * HEAD_DIM
                vx_scr[par, 2 * gi + a, 0:dv] = vt_ref[v0:v0 + HEAD_DIM, :]
                vx_scr[par, 2 * gi + a, dv:rows] = ones
        else:
            vx_scr[par, gi, 0:dv] = vt_ref[gi * LANES:(gi + 1) * LANES, :]
            vx_scr[par, gi, dv:rows] = ones

    def stage_a(j, slot, u, masked):
        s = jnp.dot(kx_scr[j // (2 * n_ct)], wq[j], preferred_element_type=F32)
        if masked:
            key = kj * tk + lax.broadcasted_iota(jnp.int32, (tk, COL_TILE), 0)
            qry = (qi * tq + (j % n_ct) * COL_TILE
                   + lax.broadcasted_iota(jnp.int32, (tk, COL_TILE), 1))
            s = jnp.where(key <= qry, s, NEG)
        s_buf[slot][u] = s
        mx_buf[slot][u] = jnp.broadcast_to(jnp.max(s, axis=0, keepdims=True), (8, COL_TILE))


    def stage_b(j0, slot, valid, us):
        m_prevs = {u: m_scr[j0 + u] for u in us}
        for u, m_prev in list(m_prevs.items()):
            m_next = jnp.maximum(m_prev, mx_buf[slot][u])
            if valid is not None:
                m_next = jnp.where(valid, m_next, m_prev)
            al_buf[slot][u] = jnp.exp2(m_prev - m_next)
            s3 = s_buf[slot][u].reshape(tk // 8, 8, COL_TILE)
            p_buf[slot][u] = jnp.exp2(s3 - m_next[None]).reshape(tk, COL_TILE).astype(BF16)
            m_prevs[u] = m_next
        for u, m_next in m_prevs.items():
            m_scr[j0 + u] = m_next

    def stage_c(j0, slot, vpar, valid, us):
        pvs = [jnp.dot(vx_scr[vpar, (j0 + u) // (n_ct * maps_per_v)], p_buf[slot][u],
                       preferred_element_type=F32) for u in us]
        olds = [acc[j0 + u] for u in us]
        for u, old, pv in zip(us, olds, pvs):
            new = ((old.reshape(rows // 8, 8, COL_TILE) * al_buf[slot][u][None])
                   .reshape(rows, COL_TILE) + pv)
            acc[j0 + u] = new if valid is None else jnp.where(valid, new, old)

    def sweep(masked):
        def trip(k, slot):
            later_step = kj > 0
            kc = (k + n_trips - 2) % n_trips
            kb = (k + n_trips - 1) % n_trips
            valid_c = jnp.logical_or(later_step, k >= 2)
            valid_b = jnp.logical_or(later_step, k >= 1)
            vpar = jnp.where(k >= 2, par, 1 - par)
            for u in range(U):
                stage_a(k * U + u, slot, u, masked)
            stage_b(kb * U, 1 - slot, valid_b, range(U))
            stage_c(kc * U, slot, vpar, valid_c, range(U))

        def trip_pair(half, carry):
            trip(2 * half, 0)
            trip(2 * half + 1, 1)
            return carry

        if n_trips == 2:
            trip_pair(0, 0)
        else:
            lax.fori_loop(0, n_trips // 2, trip_pair, 0)

    needs_mask = kj * tk + tk - 1 > qi * tq
    pl.when(needs_mask)(lambda: sweep(True))
    pl.when(jnp.logical_not(needs_mask))(lambda: sweep(False))

    @pl.when(kj == last_kj)
    def _():
        last = n_trips - 1
        stage_c((last - 1) * U, (last - 1) % 2, par, None, range(U))
        stage_b(last * U, last % 2, None, range(U))
        stage_c(last * U, last % 2, par, None, range(U))

        def map_out(mi):
            parts = [acc[mi * n_ct + ct] for ct in range(n_ct)]
            full = jnp.concatenate(parts, axis=1)
            return full[0:dv] / full[dv:dv + 1]

        for gi in range(gps):
            o0, o1 = map_out(2 * gi), map_out(2 * gi + 1)
            if mode == "fox":
                o = jnp.concatenate([o0, o1], axis=0)
            else:
                lp = lam_ref[...]
                t1 = jnp.sum(lp[0:1] * lp[1:2], axis=1, keepdims=True)
                t2 = jnp.sum(lp[2:3] * lp[3:4], axis=1, keepdims=True)
                lam = jnp.exp(t1) - jnp.exp(t2) + lam_init
                o = o0 - lam * o1
                ms = jnp.mean(o * o, axis=0, keepdims=True)
                o = o * lax.rsqrt(ms + SUBLN_EPS) * g_ref[...] * (1.0 - lam_init)
            o_ref[:, gi * LANES:(gi + 1) * LANES] = o.T.astype(BF16)


def _causal_steps(s, tq, tk):
    qi, kj = [], []
    for i in range(s // tq):
        for j in range((i * tq + tq - 1) // tk + 1):
            qi.append(i)
            kj.append(j)
    return jnp.asarray(qi, jnp.int32), jnp.asarray(kj, jnp.int32)


def _flash_pair(proj_t, keys, extras, *, mode, n_groups, gps, q_row, v_row, k_col, tq, tk,
                lam_init=0.0):
    s = keys.shape[0]
    qi, kj = _causal_steps(s, tq, tk)
    gw = gps * LANES
    qt_spec = pl.BlockSpec((gw, tq), lambda g, t, qi, kj: (q_row + g, qi[t]))
    k_spec = pl.BlockSpec((tk, gw), lambda g, t, qi, kj: (kj[t], k_col + g))
    vt_spec = pl.BlockSpec((gw, tk), lambda g, t, qi, kj: (v_row + g, kj[t]))
    const = lambda x: pl.BlockSpec(x.shape, lambda g, t, qi, kj: (0, 0))
    if mode == "fox":
        (cfeat,) = extras
        in_specs = [qt_spec, k_spec,
                    pl.BlockSpec((tk, LANES), lambda g, t, qi, kj: (kj[t], 0)), vt_spec]
        args = (proj_t, keys, cfeat, proj_t)
        kd, dv, n_values = 2 * LANES, HEAD_DIM, 2 * gps
    else:
        lam_params, subln_g = extras
        in_specs = [qt_spec, k_spec, vt_spec, const(lam_params), const(subln_g)]
        args = (proj_t, keys, proj_t, lam_params, subln_g)
        kd, dv, n_values = LANES, LANES, gps
    rows = dv + ONES_ROWS
    n_strips = 2 * gps * (tq // COL_TILE)
    u = STRIPS_PER_TRIP
    assert n_strips % (2 * u) == 0
    kern = functools.partial(_flash_pair_kernel, mode=mode, tq=tq, tk=tk, gps=gps,
                             lam_init=lam_init)
    return pl.pallas_call(
        kern,
        grid_spec=pltpu.PrefetchScalarGridSpec(
            num_scalar_prefetch=2,
            grid=(n_groups // gps, qi.shape[0]),
            in_specs=in_specs,
            out_specs=pl.BlockSpec((tq, gw), lambda g, t, qi, kj: (qi[t], g)),
            scratch_shapes=[
                pltpu.VMEM((n_strips, kd, COL_TILE), BF16),
                pltpu.VMEM((n_strips, 8, COL_TILE), F32),
                pltpu.VMEM((n_strips, rows, COL_TILE), F32),
                pltpu.VMEM((gps, tk, kd), BF16),
                pltpu.VMEM((2, n_values, rows, tk), BF16),
            ] + [pltpu.VMEM((u, tk, COL_TILE), F32)] * 2
              + [pltpu.VMEM((u, 8, COL_TILE), F32)] * 2
              + [pltpu.VMEM((u, tk, COL_TILE), BF16)] * 2
              + [pltpu.VMEM((u, 8, COL_TILE), F32)] * 2,
        ),
        out_shape=jax.ShapeDtypeStruct((s, n_groups * LANES), BF16),
        compiler_params=_params(("arbitrary", "arbitrary")),
        name=f"flash_{mode}",
    )(qi, kj, *args)


def _hyb_out_kernel(*refs, dilations):
    n_pat = len(dilations)
    pat = refs[:2 * n_pat]
    ob_ref, w_ref, h_ref, out_ref = refs[2 * n_pat:2 * n_pat + 4]
    scratch = list(refs[2 * n_pat + 4:])
    tm = h_ref.shape[0]
    vals = []
    for idx, ref in enumerate(pat):
        dil = dilations[idx // 2]
        if dil == 1:
            vals.append(ref[...])
            continue
        buf = scratch.pop(0)
        n_chunks = buf.shape[0]
        for r in range(dil):
            for c in range(n_chunks):
                c0 = (r * n_chunks + c) * LANES
                buf[c, pl.ds(r, tm // dil, stride=dil), :] = ref[:, c0:c0 + LANES]
        vals.append(jnp.concatenate([buf[c] for c in range(n_chunks)], axis=1))
    os_, ls_ = vals[0::2], vals[1::2]
    m = functools.reduce(jnp.maximum, ls_)
    es = [jnp.exp(l - m) for l in ls_]
    oa = sum(e * o for e, o in zip(es, os_)) / sum(es)
    wa = oa.shape[1]
    acc = jnp.dot(oa.astype(BF16), w_ref[0:wa, :], preferred_element_type=F32)
    acc = acc + jnp.dot(ob_ref[...], w_ref[wa:, :], preferred_element_type=F32)
    out_ref[...] = h_ref[...] + acc


def _hyb_out(pattern_outs, dilations, ob, w, h, *, tm):
    s, d = h.shape
    wa = pattern_outs[0][0].shape[1] // dilations[0]
    in_specs, args, scratch = [], [], []
    for (o, lse), dil in zip(pattern_outs, dilations):
        for arr in (o, lse):
            in_specs.append(pl.BlockSpec((tm // dil, dil * wa), lambda i: (i, 0)))
            args.append(arr)
            if dil > 1:
                scratch.append(pltpu.VMEM((wa // LANES, tm, LANES), F32))
    return pl.pallas_call(
        functools.partial(_hyb_out_kernel, dilations=tuple(dilations)),
        grid=(s // tm,),
        in_specs=in_specs + [
            pl.BlockSpec((tm, ob.shape[1]), lambda i: (i, 0)),
            pl.BlockSpec(w.shape, lambda i: (0, 0)),
            pl.BlockSpec((tm, d), lambda i: (i, 0)),
        ],
        out_specs=pl.BlockSpec((tm, d), lambda i: (i, 0)),
        out_shape=jax.ShapeDtypeStruct((s, d), F32),
        scratch_shapes=scratch,
        compiler_params=_params(("parallel",)),
        name="hyb_out",
    )(*args, ob, w, h)


def _proj_res_kernel(a_ref, w_ref, h_ref, out_ref):
    out_ref[...] = h_ref[...] + jnp.dot(a_ref[...], w_ref[...], preferred_element_type=F32)


def _proj_res(a, w, h, *, tm):
    s, d = h.shape
    return pl.pallas_call(
        _proj_res_kernel,
        grid=(s // tm,),
        in_specs=[
            pl.BlockSpec((tm, a.shape[1]), lambda i: (i, 0)),
            pl.BlockSpec(w.shape, lambda i: (0, 0)),
            pl.BlockSpec((tm, d), lambda i: (i, 0)),
        ],
        out_specs=pl.BlockSpec((tm, d), lambda i: (i, 0)),
        out_shape=jax.ShapeDtypeStruct((s, d), F32),
        compiler_params=_params(("parallel",)),
        name="proj_res",
    )(a, w, h)


HALO = BF16_SUBLANES


FFN_SUB = 256


def _ffn_kernel(h_ref, halo_ref, g_ref, wg_ref, wu_ref, cw_ref, cb_ref, wd_ref, *rest,
                tm, final):
    if final:
        fg_ref, out_ref, n_scr, gate_scr, act_scr = rest
    else:
        out_ref, n_scr, gate_scr, act_scr = rest
    i = pl.program_id(0)
    j = pl.program_id(1)
    n_chunks, _, chunk = act_scr.shape

    @pl.when(j == 0)
    def _():
        g = g_ref[...]
        prev = jnp.where(i > 0, halo_ref[...], 0.0)
        n_scr[0:HALO] = _rms(prev, g, NORM_EPS).astype(BF16)
        n_scr[HALO:] = _rms(h_ref[...], g, NORM_EPS).astype(BF16)

    for t, c0 in enumerate(range(0, chunk, FFN_SUB)):
        w = min(FFN_SUB, chunk - c0)
        cols = slice(c0, c0 + w)
        gate_scr[t, :, 0:w] = jnp.dot(n_scr[...], wg_ref[:, cols], preferred_element_type=F32)
        up = jnp.dot(n_scr[HALO:], wu_ref[:, cols], preferred_element_type=F32)
        conv = cb_ref[:, cols]
        for k in range(CONV_WIDTH):
            start = HALO - (CONV_WIDTH - 1) + k
            conv = conv + gate_scr[t, start:start + tm, 0:w] * cw_ref[k:k + 1, cols]
        act_scr[j, :, cols] = (conv * (1.0 / (1.0 + jnp.exp(-conv))) * up).astype(BF16)

    @pl.when(j == n_chunks - 1)
    def _():
        act = jnp.concatenate([act_scr[t] for t in range(n_chunks)], axis=1)
        y = h_ref[...] + jnp.dot(act, wd_ref[...], preferred_element_type=F32)
        if final:
            y = _rms(y, fg_ref[...], NORM_EPS)
        out_ref[...] = y


def _ffn(h, g, w_up, conv_w, conv_b, w_down, final_g, *, tm, n_chunks):
    s, d = h.shape
    d_ff = w_down.shape[0]
    chunk = d_ff // n_chunks
    assert chunk * n_chunks == d_ff and chunk % LANES == 0
    n_sub = -(-chunk // FFN_SUB)
    final = final_g is not None
    once = pl.Buffered(1)
    up_mode = once if n_chunks == 1 else None
    in_specs = [
        pl.BlockSpec((tm, d), lambda i, j: (i, 0)),
        pl.BlockSpec((HALO, d), lambda i, j: (jnp.maximum(i * (tm // HALO) - 1, 0), 0)),
        pl.BlockSpec((1, d), lambda i, j: (0, 0)),
        pl.BlockSpec((d, chunk), lambda i, j: (0, j), pipeline_mode=up_mode),
        pl.BlockSpec((d, chunk), lambda i, j: (0, n_chunks + j), pipeline_mode=up_mode),
        pl.BlockSpec((CONV_WIDTH, chunk), lambda i, j: (0, j)),
        pl.BlockSpec((1, chunk), lambda i, j: (0, j)),
        pl.BlockSpec((d_ff, d), lambda i, j: (0, 0), pipeline_mode=once),
    ]
    args = [h, h, g, w_up, w_up, conv_w, conv_b, w_down]
    if final:
        in_specs.append(pl.BlockSpec((1, d), lambda i, j: (0, 0)))
        args.append(final_g)
    return pl.pallas_call(
        functools.partial(_ffn_kernel, tm=tm, final=final),
        grid=(s // tm, n_chunks),
        in_specs=in_specs,
        out_specs=pl.BlockSpec((tm, d), lambda i, j: (i, 0)),
        out_shape=jax.ShapeDtypeStruct((s, d), F32),
        scratch_shapes=[
            pltpu.VMEM((tm + HALO, d), BF16),
            pltpu.VMEM((n_sub, tm + HALO, FFN_SUB), F32),
            pltpu.VMEM((n_chunks, tm, chunk), BF16),
        ],
        compiler_params=_params(("parallel", "arbitrary")),
        name="ffn_final" if final else "ffn",
    )(*args)


def _rope_tables(s):
    inv = 1.0 / (ROPE_THETA ** (jnp.arange(0, HEAD_DIM, 2, dtype=F32) / HEAD_DIM))
    ang = jnp.arange(s, dtype=F32)[:, None] * inv[None, :]
    cos, sin = jnp.cos(ang), jnp.sin(ang)
    sign = jnp.where((jnp.arange(LANES) & 32) == 0, -1.0, 1.0).astype(F32)
    cos_l = jnp.tile(cos, (1, LANES // 32))
    sin_signed = jnp.tile(sin, (1, LANES // 32)) * sign[None, :]
    return cos_l, sin_signed, cos.T, sin.T


def kernel(x, attn_norm, ffn_norm, final_norm, hyb_w_in, hyb_b_f, hyb_w_out, diff_w_qkv,
           diff_lambda, diff_subln, diff_w_out, ffn_w_up, ffn_conv_w, ffn_conv_b, ffn_w_down):
    b, s, d = x.shape
    assert b == 1
    depth = attn_norm.shape[0]
    width = hyb_w_out.shape[1] // 2
    n_pairs = width // LANES
    n_heads_b = width // HEAD_DIM
    n_diff_heads = diff_w_out.shape[1] // LANES
    dq = n_diff_heads * LANES
    tm = min(1024, s)
    ts = min(512, s)
    tq_flash, tk_flash = min(1024, s), min(512, s)
    gps = 4
    q_scale = HEAD_DIM ** -0.5
    cos, sin_signed, cos_t, sin_t = _rope_tables(s)
    h = x[0]

    for l in range(depth):
        g_attn = attn_norm[l][None, :]
        if l % 2 == 0:
            e = l // 2
            w_in = hyb_w_in[e]
            qa_ka_va, qb, kb, vb, wf = (w_in[:, :3 * width], w_in[:, 3 * width:4 * width],
                                        w_in[:, 4 * width:5 * width], w_in[:, 5 * width:6 * width],
                                        w_in[:, 6 * width:])
            dils = tuple(dil for _, dil in DILATED_PATTERNS)
            proj, *views = _norm_proj(
                h, g_attn, jnp.concatenate([qa_ka_va, kb], axis=1).astype(BF16), cos, sin_signed,
                rope_tiles=(0, 1), scale_tiles=(0,), scale=q_scale, tm=tm, tn=PROJ_TILE,
                strided_tiles=3, dilations=dils[1:])
            proj_t = _norm_proj_t(h, g_attn, jnp.concatenate([qb, vb], axis=1).T.astype(BF16),
                                  cos_t, sin_t, rope_tiles=(), scale_tiles=(0,),
                                  scale=q_scale * LOG2E, tm=tm, tn=PROJ_TILE)
            wf_pad = jnp.pad(wf, ((0, 0), (0, LANES - n_heads_b))).astype(BF16)
            bf_pad = jnp.pad(hyb_b_f[e], (0, LANES - n_heads_b))[None, :]
            cfeat = _fox_gate(h, g_attn, wf_pad, bf_pad, tc=ts, n_heads=n_heads_b)
            pats = [_dilated_pattern(proj, 1, width=width, per_res=4, tq=ts)]
            pats += [_dilated_pattern(v, dil, width=width, per_res=3, tq=min(512, s // dil))
                     for v, dil in zip(views, dils[1:])]
            ob = _flash_pair(proj_t, proj, (cfeat,), mode="fox", n_groups=n_pairs, gps=gps,
                             q_row=0, v_row=n_pairs // gps, k_col=3 * n_pairs // gps,
                             tq=tq_flash, tk=tk_flash)
            h = _hyb_out(pats, dils, ob, hyb_w_out[e].astype(BF16), h, tm=ts)
        else:
            o = l // 2
            w = diff_w_qkv[o]
            wq, wk, wv = w[:, :dq], w[:, dq:2 * dq], w[:, 2 * dq:]
            k_tiles = dq // PROJ_TILE
            keys = _norm_proj(h, g_attn, wk.astype(BF16), cos, sin_signed,
                              rope_tiles=tuple(range(k_tiles)), scale_tiles=(),
                              scale=1.0, tm=tm, tn=PROJ_TILE)
            proj_t = _norm_proj_t(h, g_attn, jnp.concatenate([wq, wv], axis=1).T.astype(BF16),
                                  cos_t, sin_t, rope_tiles=tuple(range(k_tiles)),
                                  scale_tiles=tuple(range(k_tiles)), scale=q_scale * LOG2E,
                                  tm=tm, tn=PROJ_TILE)
            lam_init = 0.8 - 0.6 * math.exp(-0.3 * l)
            att = _flash_pair(proj_t, keys, (diff_lambda[o], diff_subln[o][:, None]),
                              mode="diff", n_groups=n_diff_heads, gps=n_diff_heads, q_row=0,
                              v_row=1, k_col=0, tq=tq_flash, tk=tk_flash,
                              lam_init=lam_init)
            h = _proj_res(att, diff_w_out[o].astype(BF16), h, tm=tm)
        h = _ffn(h, ffn_norm[l][None, :], ffn_w_up[l].astype(BF16), ffn_conv_w[l],
                 ffn_conv_b[l][None, :], ffn_w_down[l].astype(BF16),
                 final_norm[None, :] if l == depth - 1 else None, tm=ts, n_chunks=1)
    return h[None]
```

```python
import functools
import math

import jax
import jax.numpy as jnp
from jax import lax
from jax.experimental import pallas as pl
from jax.experimental.pallas import tpu as pltpu

F32 = jnp.float32
BF16 = jnp.bfloat16

HEAD_DIM = 64
LANES = 128
BF16_SUBLANES = 16
ROPE_THETA = 10000.0
DILATED_PATTERNS = ((128, 1), (512, 4), (2048, 16))
BAND = 128
NORM_EPS = 1e-6
SUBLN_EPS = 1e-5
CONV_WIDTH = 3
NEG = -1e30
VMEM_LIMIT = 48 * 1024 * 1024
PROJ_TILE = 512

NT_DIMS = (((1,), (1,)), ((), ()))


def _params(sem):
    return pltpu.CompilerParams(dimension_semantics=sem, vmem_limit_bytes=VMEM_LIMIT)


def _rms(x, g, eps):
    return x * lax.rsqrt(jnp.mean(x * x, axis=-1, keepdims=True) + eps) * g


def _lane_lo(rows):
    return lax.broadcasted_iota(jnp.int32, (rows, LANES), 1) < HEAD_DIM


def _rope_tile(x, cos, sin_signed):
    rows = x.shape[0]
    first_half = (lax.broadcasted_iota(jnp.int32, (rows, LANES), 1) & 32) == 0
    out = []
    for c in range(x.shape[1] // LANES):
        xc = x[:, c * LANES:(c + 1) * LANES]
        ahead = pltpu.roll(xc, LANES - 32, 1)
        behind = pltpu.roll(xc, 32, 1)
        rot = jnp.where(first_half, ahead, behind)
        out.append(xc * cos + rot * sin_signed)
    return jnp.concatenate(out, axis=1)


def _norm_proj_kernel(h_ref, g_ref, w_ref, cos_ref, sin_ref, o_ref, *rest,
                      rope_tiles, scale_tiles, scale, tn, strided_tiles, dilations):
    n = _rms(h_ref[...], g_ref[...], NORM_EPS).astype(BF16)
    tm = h_ref.shape[0]
    for j in range(w_ref.shape[1] // tn):
        cols = slice(j * tn, (j + 1) * tn)
        acc = jnp.dot(n, w_ref[:, cols], preferred_element_type=F32)
        if j in rope_tiles:
            acc = _rope_tile(acc, cos_ref[...], sin_ref[...])
        if j in scale_tiles:
            acc = acc * scale
        o_ref[:, cols] = acc.astype(BF16)
        if j < strided_tiles:
            stage = rest[-1]
            for c in range(tn // LANES):
                stage[c] = acc[:, c * LANES:(c + 1) * LANES]
            for d_ref, dil in zip(rest[:-1], dilations):
                for r in range(dil):
                    for c in range(tn // LANES):
                        c0 = (r * strided_tiles + j) * tn + c * LANES
                        rows = stage[c, pl.ds(r, tm // dil, stride=dil), :]
                        d_ref[:, c0:c0 + LANES] = rows.astype(BF16)


def _norm_proj(h, g, w, cos, sin_signed, *, rope_tiles, scale_tiles, scale, tm, tn,
               strided_tiles=0, dilations=()):
    s, d = h.shape
    n = w.shape[1]
    kern = functools.partial(_norm_proj_kernel, rope_tiles=rope_tiles, scale_tiles=scale_tiles,
                             scale=scale, tn=tn, strided_tiles=strided_tiles,
                             dilations=dilations)
    out_specs = [pl.BlockSpec((tm, n), lambda i: (i, 0))]
    out_shape = [jax.ShapeDtypeStruct((s, n), BF16)]
    for dil in dilations:
        wide = dil * strided_tiles * tn
        out_specs.append(pl.BlockSpec((tm // dil, wide), lambda i: (i, 0)))
        out_shape.append(jax.ShapeDtypeStruct((s // dil, wide), BF16))
    outs = pl.pallas_call(
        kern,
        grid=(s // tm,),
        in_specs=[
            pl.BlockSpec((tm, d), lambda i: (i, 0)),
            pl.BlockSpec((1, d), lambda i: (0, 0)),
            pl.BlockSpec((d, n), lambda i: (0, 0)),
            pl.BlockSpec((tm, LANES), lambda i: (i, 0)),
            pl.BlockSpec((tm, LANES), lambda i: (i, 0)),
        ],
        out_specs=out_specs,
        out_shape=out_shape,
        scratch_shapes=[pltpu.VMEM((tn // LANES, tm, LANES), F32)] if dilations else [],
        compiler_params=_params(("parallel",)),
        name="norm_proj",
    )(h, g, w, cos, sin_signed)
    return outs if dilations else outs[0]


def _norm_proj_t_kernel(h_ref, g_ref, wt_ref, cos_ref, sin_ref, o_ref, *,
                        rope_tiles, scale_tiles, scale, tn):
    n = _rms(h_ref[...], g_ref[...], NORM_EPS).astype(BF16)
    half = HEAD_DIM // 2
    for j in range(wt_ref.shape[0] // tn):
        acc = lax.dot_general(wt_ref[j * tn:(j + 1) * tn, :], n, NT_DIMS,
                              preferred_element_type=F32)
        sc = scale if j in scale_tiles else 1.0
        if j in rope_tiles:
            cos, sin = cos_ref[...], sin_ref[...]
            for hd in range(tn // HEAD_DIM):
                r0 = j * tn + hd * HEAD_DIM
                x1 = acc[hd * HEAD_DIM:hd * HEAD_DIM + half]
                x2 = acc[hd * HEAD_DIM + half:(hd + 1) * HEAD_DIM]
                o_ref[r0:r0 + half] = ((x1 * cos - x2 * sin) * sc).astype(BF16)
                o_ref[r0 + half:r0 + HEAD_DIM] = ((x2 * cos + x1 * sin) * sc).astype(BF16)
        else:
            o_ref[j * tn:(j + 1) * tn] = (acc * sc).astype(BF16)


def _norm_proj_t(h, g, wt, cos_t, sin_t, *, rope_tiles, scale_tiles, scale, tm, tn):
    s, d = h.shape
    n = wt.shape[0]
    kern = functools.partial(_norm_proj_t_kernel, rope_tiles=rope_tiles,
                             scale_tiles=scale_tiles, scale=scale, tn=tn)
    return pl.pallas_call(
        kern,
        grid=(s // tm,),
        in_specs=[
            pl.BlockSpec((tm, d), lambda i: (i, 0)),
            pl.BlockSpec((1, d), lambda i: (0, 0)),
            pl.BlockSpec((n, d), lambda i: (0, 0)),
            pl.BlockSpec((HEAD_DIM // 2, tm), lambda i: (0, i)),
            pl.BlockSpec((HEAD_DIM // 2, tm), lambda i: (0, i)),
        ],
        out_specs=pl.BlockSpec((n, tm), lambda i: (0, i)),
        out_shape=jax.ShapeDtypeStruct((n, s), BF16),
        compiler_params=_params(("parallel",)),
        name="norm_proj_t",
    )(h, g, wt, cos_t, sin_t)


GATE_TERMS = 3
GATE_STRIDE = 8
LOG2E = math.log2(math.e)


def _fox_gate_kernel(h_ref, g_ref, wf_ref, bf_ref, cf_ref, carry, *, tc, n_heads):
    i = pl.program_id(0)

    @pl.when(i == 0)
    def _():
        carry[...] = jnp.zeros_like(carry)

    n = _rms(h_ref[...], g_ref[...], NORM_EPS).astype(BF16)
    z = jnp.dot(n, wf_ref[...], preferred_element_type=F32) + bf_ref[...]
    logf = jnp.minimum(z, 0.0) - jnp.log(1.0 + jnp.exp(-jnp.abs(z)))
    dst = lax.broadcasted_iota(jnp.int32, (tc, tc), 0)
    src = lax.broadcasted_iota(jnp.int32, (tc, tc), 1)
    prefix = jnp.where(src <= dst, 1.0, 0.0).astype(F32)
    cs = jnp.dot(prefix, logf, precision=lax.Precision.HIGHEST,
                 preferred_element_type=F32) + carry[0:1, :]
    carry[...] = jnp.broadcast_to(cs[tc - 1:tc, :], carry.shape)
    lane = lax.broadcasted_iota(jnp.int32, (tc, LANES), 1)
    rem = jnp.where(lane < n_heads, cs * LOG2E, 0.0)
    feat = jnp.zeros_like(rem)
    for term in range(GATE_TERMS):
        part = rem.astype(BF16).astype(F32)
        rem = rem - part
        feat = feat + (pltpu.roll(part, term * GATE_STRIDE, 1) if term else part)
    cf_ref[...] = feat.astype(BF16)


def _fox_gate(h, g, wf, b_f, *, tc, n_heads):
    s, d = h.shape
    return pl.pallas_call(
        functools.partial(_fox_gate_kernel, tc=tc, n_heads=n_heads),
        grid=(s // tc,),
        in_specs=[
            pl.BlockSpec((tc, d), lambda i: (i, 0)),
            pl.BlockSpec((1, d), lambda i: (0, 0)),
            pl.BlockSpec((d, LANES), lambda i: (0, 0)),
            pl.BlockSpec((1, LANES), lambda i: (0, 0)),
        ],
        out_specs=pl.BlockSpec((tc, LANES), lambda i: (i, 0)),
        out_shape=jax.ShapeDtypeStruct((s, LANES), BF16),
        scratch_shapes=[pltpu.VMEM((8, LANES), F32)],
        compiler_params=_params(("arbitrary",)),
        name="fox_gate",
    )(h, g, wf, b_f)


def _dilated_kernel(q_ref, kc_ref, kh_ref, vc_ref, vh_ref, o_ref, l_ref, kbuf, vbuf, *, tq):
    i = pl.program_id(1)
    kbuf[0:BAND] = kh_ref[...]
    kbuf[BAND:] = kc_ref[...]
    vbuf[0:BAND] = vh_ref[...]
    vbuf[BAND:] = vc_ref[...]
    row = lax.broadcasted_iota(jnp.int32, (BAND, 2 * BAND), 0)
    col = lax.broadcasted_iota(jnp.int32, (BAND, 2 * BAND), 1)
    delta = row - col + BAND
    band = (delta >= 0) & (delta <= BAND)
    lo = _lane_lo(BAND)
    for a in range(tq // BAND):
        first_key = i * tq + (a - 1) * BAND
        valid = band & (col + first_key >= 0)
        rows = slice(a * BAND, (a + 1) * BAND)
        for hp in range(q_ref.shape[1] // LANES):
            lanes = slice(hp * LANES, (hp + 1) * LANES)
            q = q_ref[rows, lanes]
            kk = kbuf[a * BAND:(a + 2) * BAND, lanes]
            vv = vbuf[a * BAND:(a + 2) * BAND, lanes]
            outs, lses = [], []
            for qh in (jnp.where(lo, q, jnp.zeros_like(q)), jnp.where(lo, jnp.zeros_like(q), q)):
                s = lax.dot_general(qh, kk, NT_DIMS, preferred_element_type=F32)
                s = jnp.where(valid, s, NEG)
                m = jnp.max(s, axis=1, keepdims=True)
                p = jnp.exp(s - m)
                den = jnp.sum(p, axis=1, keepdims=True)
                pv = jnp.dot(p.astype(BF16), vv, preferred_element_type=F32)
                outs.append(pv / den)
                lses.append(jnp.broadcast_to(m + jnp.log(den), (BAND, LANES)))
            o_ref[rows, lanes] = jnp.where(lo, outs[0], outs[1])
            l_ref[rows, lanes] = jnp.where(lo, lses[0], lses[1])


def _dilated_pattern(view, dil, *, width, per_res, tq):
    L = view.shape[0]
    halo_per_tile = tq // BAND
    cur = lambda part: pl.BlockSpec((tq, width), lambda r, i: (i, r * per_res + part))
    halo = lambda part: pl.BlockSpec(
        (BAND, width), lambda r, i: (jnp.maximum(i * halo_per_tile - 1, 0), r * per_res + part))
    out_spec = pl.BlockSpec((tq, width), lambda r, i: (i, r))
    return pl.pallas_call(
        functools.partial(_dilated_kernel, tq=tq),
        grid=(dil, L // tq),
        in_specs=[cur(0), cur(1), halo(1), cur(2), halo(2)],
        out_specs=[out_spec, out_spec],
        out_shape=[jax.ShapeDtypeStruct((L, dil * width), F32)] * 2,
        scratch_shapes=[pltpu.VMEM((tq + BAND, width), BF16)] * 2,
        compiler_params=_params(("parallel", "arbitrary")),
        name=f"dilated_d{dil}",
    )(view, view, view, view, view)


ONES_ROWS = BF16_SUBLANES


COL_TILE = 256
STRIPS_PER_TRIP = 4


def _flash_pair_kernel(qi_ref, kj_ref, qt_ref, k_ref, *rest, mode, tq, tk, gps, lam_init):
    if mode == "fox":
        cf_ref, vt_ref, o_ref, wq, m_scr, acc, kx_scr, vx_scr, *bufs = rest
    else:
        vt_ref, lam_ref, g_ref, o_ref, wq, m_scr, acc, kx_scr, vx_scr, *bufs = rest
    s_buf, mx_buf, p_buf, al_buf = bufs[0:2], bufs[2:4], bufs[4:6], bufs[6:8]
    U = STRIPS_PER_TRIP
    step = pl.program_id(1)
    qi = qi_ref[step]
    kj = kj_ref[step]
    last_kj = (qi * tq + tq - 1) // tk
    rows = acc.shape[1]
    dv = rows - ONES_ROWS
    n_ct = tq // COL_TILE
    n_strips = 2 * gps * n_ct
    n_trips = n_strips // U
    par = kj % 2

    @pl.when(jnp.logical_and(pl.program_id(0) == 0, step == 0))
    def _():
        for buf in bufs:
            buf[...] = jnp.zeros_like(buf)
        vx_scr[...] = jnp.zeros_like(vx_scr)

    @pl.when(kj == 0)
    def _():
        r = lax.broadcasted_iota(jnp.int32, (LANES, COL_TILE), 0)
        first = r < HEAD_DIM
        for gi in range(gps):
            for a in range(2):
                keep_q = first if a == 0 else jnp.logical_not(first)
                if mode == "fox":
                    head = 2 * (pl.program_id(0) * gps + gi) + a
                    pick = (r - head) == 0
                    for term in range(1, GATE_TERMS):
                        pick = pick | ((r - head) == term * GATE_STRIDE)
                    gate_rows = jnp.where(pick, -1.0, 0.0).astype(BF16)
                for ct in range(n_ct):
                    t = (2 * gi + a) * n_ct + ct
                    qt = qt_ref[gi * LANES:(gi + 1) * LANES, ct * COL_TILE:(ct + 1) * COL_TILE]
                    wq[t, 0:LANES] = jnp.where(keep_q, qt, jnp.zeros_like(qt))
                    if mode == "fox":
                        wq[t, LANES:2 * LANES] = gate_rows
        m_scr[...] = jnp.full_like(m_scr, NEG)
        acc[...] = jnp.zeros_like(acc)

    maps_per_v = 1 if mode == "fox" else 2
    ones = jnp.ones((ONES_ROWS, tk), BF16)
    for gi in range(gps):
        kx_scr[gi, :, 0:LANES] = k_ref[:, gi * LANES:(gi + 1) * LANES]
        if mode == "fox":
            kx_scr[gi, :, LANES:2 * LANES] = cf_ref[...]
            for a in range(2):
                v0 = gi * LANES + a * HEAD_DIM
                vx_scr[par, 2 * gi + a, 0:dv] = vt_ref[v0:v0 + HEAD_DIM, :]
                vx_scr[par, 2 * gi + a, dv:rows] = ones
        else:
            vx_scr[par, gi, 0:dv] = vt_ref[gi * LANES:(gi + 1) * LANES, :]
            vx_scr[par, gi, dv:rows] = ones

    def stage_a(j, slot, u, masked):
        s = jnp.dot(kx_scr[j // (2 * n_ct)], wq[j], preferred_element_type=F32)
        if masked:
            key = kj * tk + lax.broadcasted_iota(jnp.int32, (tk, COL_TILE), 0)
            qry = (qi * tq + (j % n_ct) * COL_TILE
                   + lax.broadcasted_iota(jnp.int32, (tk, COL_TILE), 1))
            s = jnp.where(key <= qry, s, NEG)
        s_buf[slot][u] = s
        mx_buf[slot][u] = jnp.broadcast_to(jnp.max(s, axis=0, keepdims=True), (8, COL_TILE))


    def stage_b(j0, slot, valid, us):
        m_prevs = {u: m_scr[j0 + u] for u in us}
        for u, m_prev in list(m_prevs.items()):
            m_next = jnp.maximum(m_prev, mx_buf[slot][u])
            if valid is not None:
                m_next = jnp.where(valid, m_next, m_prev)
            al_buf[slot][u] = jnp.exp2(m_prev - m_next)
            s3 = s_buf[slot][u].reshape(tk // 8, 8, COL_TILE)
            p_buf[slot][u] = jnp.exp2(s3 - m_next[None]).reshape(tk, COL_TILE).astype(BF16)
            m_prevs[u] = m_next
        for u, m_next in m_prevs.items():
            m_scr[j0 + u] = m_next

    def stage_c(j0, slot, vpar, valid, us):
        pvs = [jnp.dot(vx_scr[vpar, (j0 + u) // (n_ct * maps_per_v)], p_buf[slot][u],
                       preferred_element_type=F32) for u in us]
        olds = [acc[j0 + u] for u in us]
        for u, old, pv in zip(us, olds, pvs):
            new = ((old.reshape(rows // 8, 8, COL_TILE) * al_buf[slot][u][None])
                   .reshape(rows, COL_TILE) + pv)
            acc[j0 + u] = new if valid is None else jnp.where(valid, new, old)

    def sweep(masked):
        def trip(k, slot):
            later_step = kj > 0
            kc = (k + n_trips - 2) % n_trips
            kb = (k + n_trips - 1) % n_trips
            valid_c = jnp.logical_or(later_step, k >= 2)
            valid_b = jnp.logical_or(later_step, k >= 1)
            vpar = jnp.where(k >= 2, par, 1 - par)
            for u in range(U):
                stage_a(k * U + u, slot, u, masked)
            stage_b(kb * U, 1 - slot, valid_b, range(U))
            stage_c(kc * U, slot, vpar, valid_c, range(U))

        def trip_pair(half, carry):
            trip(2 * half, 0)
            trip(2 * half + 1, 1)
            return carry

        if n_trips == 2:
            trip_pair(0, 0)
        else:
            lax.fori_loop(0, n_trips // 2, trip_pair, 0)

    assert U % n_ct == 0
    live = tuple(u for u in range(U) if (u % n_ct) * COL_TILE >= tq - tk)

    def last_sweep():
        def trip(k, slot, b_from_prev_step, c_from_prev_step):
            kc = (k + n_trips - 2) % n_trips
            kb = (k + n_trips - 1) % n_trips
            for u in live:
                stage_a(k * U + u, slot, u, True)
            stage_b(kb * U, 1 - slot, None, range(U) if b_from_prev_step else live)
            stage_c(kc * U, slot, 1 - par if c_from_prev_step else par, None,
                    range(U) if c_from_prev_step else live)

        def trip_pair(half, carry):
            trip(2 * half, 0, False, False)
            trip(2 * half + 1, 1, False, False)
            return carry

        trip(0, 0, True, True)
        trip(1, 1, False, True)
        lax.fori_loop(1, n_trips // 2, trip_pair, 0)

    needs_mask = kj * tk + tk - 1 > qi * tq
    is_last = kj == last_kj
    pl.when(jnp.logical_and(needs_mask, jnp.logical_not(is_last)))(lambda: sweep(True))
    pl.when(jnp.logical_not(needs_mask))(lambda: sweep(False))
    pl.when(is_last)(last_sweep)

    @pl.when(is_last)
    def _():
        last = n_trips - 1
        stage_c((last - 1) * U, (last - 1) % 2, par, None, live)
        stage_b(last * U, last % 2, None, live)
        stage_c(last * U, last % 2, par, None, live)

        def map_out(mi):
            parts = [acc[mi * n_ct + ct] for ct in range(n_ct)]
            full = jnp.concatenate(parts, axis=1)
            return full[0:dv] / full[dv:dv + 1]

        for gi in range(gps):
            o0, o1 = map_out(2 * gi), map_out(2 * gi + 1)
            if mode == "fox":
                o = jnp.concatenate([o0, o1], axis=0)
            else:
                lp = lam_ref[...]
                t1 = jnp.sum(lp[0:1] * lp[1:2], axis=1, keepdims=True)
                t2 = jnp.sum(lp[2:3] * lp[3:4], axis=1, keepdims=True)
                lam = jnp.exp(t1) - jnp.exp(t2) + lam_init
                o = o0 - lam * o1
                ms = jnp.mean(o * o, axis=0, keepdims=True)
                o = o * lax.rsqrt(ms + SUBLN_EPS) * g_ref[...] * (1.0 - lam_init)
            o_ref[:, gi * LANES:(gi + 1) * LANES] = o.T.astype(BF16)


def _causal_steps(s, tq, tk):
    qi, kj = [], []
    for i in range(s // tq):
        for j in range((i * tq + tq - 1) // tk + 1):
            qi.append(i)
            kj.append(j)
    return jnp.asarray(qi, jnp.int32), jnp.asarray(kj, jnp.int32)


def _flash_pair(proj_t, keys, extras, *, mode, n_groups, gps, q_row, v_row, k_col, tq, tk,
                lam_init=0.0):
    s = keys.shape[0]
    qi, kj = _causal_steps(s, tq, tk)
    gw = gps * LANES
    qt_spec = pl.BlockSpec((gw, tq), lambda g, t, qi, kj: (q_row + g, qi[t]))
    k_spec = pl.BlockSpec((tk, gw), lambda g, t, qi, kj: (kj[t], k_col + g))
    vt_spec = pl.BlockSpec((gw, tk), lambda g, t, qi, kj: (v_row + g, kj[t]))
    const = lambda x: pl.BlockSpec(x.shape, lambda g, t, qi, kj: (0, 0))
    if mode == "fox":
        (cfeat,) = extras
        in_specs = [qt_spec, k_spec,
                    pl.BlockSpec((tk, LANES), lambda g, t, qi, kj: (kj[t], 0)), vt_spec]
        args = (proj_t, keys, cfeat, proj_t)
        kd, dv, n_values = 2 * LANES, HEAD_DIM, 2 * gps
    else:
        lam_params, subln_g = extras
        in_specs = [qt_spec, k_spec, vt_spec, const(lam_params), const(subln_g)]
        args = (proj_t, keys, proj_t, lam_params, subln_g)
        kd, dv, n_values = LANES, LANES, gps
    rows = dv + ONES_ROWS
    n_strips = 2 * gps * (tq // COL_TILE)
    u = STRIPS_PER_TRIP
    assert n_strips % (2 * u) == 0
    kern = functools.partial(_flash_pair_kernel, mode=mode, tq=tq, tk=tk, gps=gps,
                             lam_init=lam_init)
    return pl.pallas_call(
        kern,
        grid_spec=pltpu.PrefetchScalarGridSpec(
            num_scalar_prefetch=2,
            grid=(n_groups // gps, qi.shape[0]),
            in_specs=in_specs,
            out_specs=pl.BlockSpec((tq, gw), lambda g, t, qi, kj: (qi[t], g)),
            scratch_shapes=[
                pltpu.VMEM((n_strips, kd, COL_TILE), BF16),
                pltpu.VMEM((n_strips, 8, COL_TILE), F32),
                pltpu.VMEM((n_strips, rows, COL_TILE), F32),
                pltpu.VMEM((gps, tk, kd), BF16),
                pltpu.VMEM((2, n_values, rows, tk), BF16),
            ] + [pltpu.VMEM((u, tk, COL_TILE), F32)] * 2
              + [pltpu.VMEM((u, 8, COL_TILE), F32)] * 2
              + [pltpu.VMEM((u, tk, COL_TILE), BF16)] * 2
              + [pltpu.VMEM((u, 8, COL_TILE), F32)] * 2,
        ),
        out_shape=jax.ShapeDtypeStruct((s, n_groups * LANES), BF16),
        compiler_params=_params(("arbitrary", "arbitrary")),
        name=f"flash_{mode}",
    )(qi, kj, *args)


def _hyb_out_kernel(*refs, dilations):
    n_pat = len(dilations)
    pat = refs[:2 * n_pat]
    ob_ref, w_ref, h_ref, out_ref = refs[2 * n_pat:2 * n_pat + 4]
    scratch = list(refs[2 * n_pat + 4:])
    tm = h_ref.shape[0]
    vals = []
    for idx, ref in enumerate(pat):
        dil = dilations[idx // 2]
        if dil == 1:
            vals.append(ref[...])
            continue
        buf = scratch.pop(0)
        n_chunks = buf.shape[0]
        for r in range(dil):
            for c in range(n_chunks):
                c0 = (r * n_chunks + c) * LANES
                buf[c, pl.ds(r, tm // dil, stride=dil), :] = ref[:, c0:c0 + LANES]
        vals.append(jnp.concatenate([buf[c] for c in range(n_chunks)], axis=1))
    os_, ls_ = vals[0::2], vals[1::2]
    m = functools.reduce(jnp.maximum, ls_)
    es = [jnp.exp(l - m) for l in ls_]
    oa = sum(e * o for e, o in zip(es, os_)) / sum(es)
    wa = oa.shape[1]
    acc = jnp.dot(oa.astype(BF16), w_ref[0:wa, :], preferred_element_type=F32)
    acc = acc + jnp.dot(ob_ref[...], w_ref[wa:, :], preferred_element_type=F32)
    out_ref[...] = h_ref[...] + acc


def _hyb_out(pattern_outs, dilations, ob, w, h, *, tm):
    s, d = h.shape
    wa = pattern_outs[0][0].shape[1] // dilations[0]
    in_specs, args, scratch = [], [], []
    for (o, lse), dil in zip(pattern_outs, dilations):
        for arr in (o, lse):
            in_specs.append(pl.BlockSpec((tm // dil, dil * wa), lambda i: (i, 0)))
            args.append(arr)
            if dil > 1:
                scratch.append(pltpu.VMEM((wa // LANES, tm, LANES), F32))
    return pl.pallas_call(
        functools.partial(_hyb_out_kernel, dilations=tuple(dilations)),
        grid=(s // tm,),
        in_specs=in_specs + [
            pl.BlockSpec((tm, ob.shape[1]), lambda i: (i, 0)),
            pl.BlockSpec(w.shape, lambda i: (0, 0)),
            pl.BlockSpec((tm, d), lambda i: (i, 0)),
        ],
        out_specs=pl.BlockSpec((tm, d), lambda i: (i, 0)),
        out_shape=jax.ShapeDtypeStruct((s, d), F32),
        scratch_shapes=scratch,
        compiler_params=_params(("parallel",)),
        name="hyb_out",
    )(*args, ob, w, h)


def _proj_res_kernel(a_ref, w_ref, h_ref, out_ref):
    out_ref[...] = h_ref[...] + jnp.dot(a_ref[...], w_ref[...], preferred_element_type=F32)


def _proj_res(a, w, h, *, tm):
    s, d = h.shape
    return pl.pallas_call(
        _proj_res_kernel,
        grid=(s // tm,),
        in_specs=[
            pl.BlockSpec((tm, a.shape[1]), lambda i: (i, 0)),
            pl.BlockSpec(w.shape, lambda i: (0, 0)),
            pl.BlockSpec((tm, d), lambda i: (i, 0)),
        ],
        out_specs=pl.BlockSpec((tm, d), lambda i: (i, 0)),
        out_shape=jax.ShapeDtypeStruct((s, d), F32),
        compiler_params=_params(("parallel",)),
        name="proj_res",
    )(a, w, h)


HALO = BF16_SUBLANES


FFN_SUB = 256


def _ffn_kernel(h_ref, halo_ref, g_ref, wg_ref, wu_ref, cw_ref, cb_ref, wd_ref, *rest,
                tm, final):
    if final:
        fg_ref, out_ref, n_scr, gate_scr, act_scr = rest
    else:
        out_ref, n_scr, gate_scr, act_scr = rest
    i = pl.program_id(0)
    j = pl.program_id(1)
    n_chunks, _, chunk = act_scr.shape

    @pl.when(j == 0)
    def _():
        g = g_ref[...]
        prev = jnp.where(i > 0, halo_ref[...], 0.0)
        n_scr[0:HALO] = _rms(prev, g, NORM_EPS).astype(BF16)
        n_scr[HALO:] = _rms(h_ref[...], g, NORM_EPS).astype(BF16)

    for t, c0 in enumerate(range(0, chunk, FFN_SUB)):
        w = min(FFN_SUB, chunk - c0)
        cols = slice(c0, c0 + w)
        gate_scr[t, :, 0:w] = jnp.dot(n_scr[...], wg_ref[:, cols], preferred_element_type=F32)
        up = jnp.dot(n_scr[HALO:], wu_ref[:, cols], preferred_element_type=F32)
        conv = cb_ref[:, cols]
        for k in range(CONV_WIDTH):
            start = HALO - (CONV_WIDTH - 1) + k
            conv = conv + gate_scr[t, start:start + tm, 0:w] * cw_ref[k:k + 1, cols]
        act_scr[j, :, cols] = (conv * (1.0 / (1.0 + jnp.exp(-conv))) * up).astype(BF16)

    @pl.when(j == n_chunks - 1)
    def _():
        act = jnp.concatenate([act_scr[t] for t in range(n_chunks)], axis=1)
        y = h_ref[...] + jnp.dot(act, wd_ref[...], preferred_element_type=F32)
        if final:
            y = _rms(y, fg_ref[...], NORM_EPS)
        out_ref[...] = y


def _ffn(h, g, w_up, conv_w, conv_b, w_down, final_g, *, tm, n_chunks):
    s, d = h.shape
    d_ff = w_down.shape[0]
    chunk = d_ff // n_chunks
    assert chunk * n_chunks == d_ff and chunk % LANES == 0
    n_sub = -(-chunk // FFN_SUB)
    final = final_g is not None
    once = pl.Buffered(1)
    up_mode = once if n_chunks == 1 else None
    in_specs = [
        pl.BlockSpec((tm, d), lambda i, j: (i, 0)),
        pl.BlockSpec((HALO, d), lambda i, j: (jnp.maximum(i * (tm // HALO) - 1, 0), 0)),
        pl.BlockSpec((1, d), lambda i, j: (0, 0)),
        pl.BlockSpec((d, chunk), lambda i, j: (0, j), pipeline_mode=up_mode),
        pl.BlockSpec((d, chunk), lambda i, j: (0, n_chunks + j), pipeline_mode=up_mode),
        pl.BlockSpec((CONV_WIDTH, chunk), lambda i, j: (0, j)),
        pl.BlockSpec((1, chunk), lambda i, j: (0, j)),
        pl.BlockSpec((d_ff, d), lambda i, j: (0, 0), pipeline_mode=once),
    ]
    args = [h, h, g, w_up, w_up, conv_w, conv_b, w_down]
    if final:
        in_specs.append(pl.BlockSpec((1, d), lambda i, j: (0, 0)))
        args.append(final_g)
    return pl.pallas_call(
        functools.partial(_ffn_kernel, tm=tm, final=final),
        grid=(s // tm, n_chunks),
        in_specs=in_specs,
        out_specs=pl.BlockSpec((tm, d), lambda i, j: (i, 0)),
        out_shape=jax.ShapeDtypeStruct((s, d), F32),
        scratch_shapes=[
            pltpu.VMEM((tm + HALO, d), BF16),
            pltpu.VMEM((n_sub, tm + HALO, FFN_SUB), F32),
            pltpu.VMEM((n_chunks, tm, chunk), BF16),
        ],
        compiler_params=_params(("parallel", "arbitrary")),
        name="ffn_final" if final else "ffn",
    )(*args)


def _rope_tables(s):
    inv = 1.0 / (ROPE_THETA ** (jnp.arange(0, HEAD_DIM, 2, dtype=F32) / HEAD_DIM))
    ang = jnp.arange(s, dtype=F32)[:, None] * inv[None, :]
    cos, sin = jnp.cos(ang), jnp.sin(ang)
    sign = jnp.where((jnp.arange(LANES) & 32) == 0, -1.0, 1.0).astype(F32)
    cos_l = jnp.tile(cos, (1, LANES // 32))
    sin_signed = jnp.tile(sin, (1, LANES // 32)) * sign[None, :]
    return cos_l, sin_signed, cos.T, sin.T


def kernel(x, attn_norm, ffn_norm, final_norm, hyb_w_in, hyb_b_f, hyb_w_out, diff_w_qkv,
           diff_lambda, diff_subln, diff_w_out, ffn_w_up, ffn_conv_w, ffn_conv_b, ffn_w_down):
    b, s, d = x.shape
    assert b == 1
    depth = attn_norm.shape[0]
    width = hyb_w_out.shape[1] // 2
    n_pairs = width // LANES
    n_heads_b = width // HEAD_DIM
    n_diff_heads = diff_w_out.shape[1] // LANES
    dq = n_diff_heads * LANES
    tm = min(1024, s)
    ts = min(512, s)
    tq_flash, tk_flash = min(1024, s), min(512, s)
    gps = 4
    q_scale = HEAD_DIM ** -0.5
    cos, sin_signed, cos_t, sin_t = _rope_tables(s)
    h = x[0]

    for l in range(depth):
        g_attn = attn_norm[l][None, :]
        if l % 2 == 0:
            e = l // 2
            w_in = hyb_w_in[e]
            qa_ka_va, qb, kb, vb, wf = (w_in[:, :3 * width], w_in[:, 3 * width:4 * width],
                                        w_in[:, 4 * width:5 * width], w_in[:, 5 * width:6 * width],
                                        w_in[:, 6 * width:])
            dils = tuple(dil for _, dil in DILATED_PATTERNS)
            proj, *views = _norm_proj(
                h, g_attn, jnp.concatenate([qa_ka_va, kb], axis=1).astype(BF16), cos, sin_signed,
                rope_tiles=(0, 1), scale_tiles=(0,), scale=q_scale, tm=tm, tn=PROJ_TILE,
                strided_tiles=3, dilations=dils[1:])
            proj_t = _norm_proj_t(h, g_attn, jnp.concatenate([qb, vb], axis=1).T.astype(BF16),
                                  cos_t, sin_t, rope_tiles=(), scale_tiles=(0,),
                                  scale=q_scale * LOG2E, tm=tm, tn=PROJ_TILE)
            wf_pad = jnp.pad(wf, ((0, 0), (0, LANES - n_heads_b))).astype(BF16)
            bf_pad = jnp.pad(hyb_b_f[e], (0, LANES - n_heads_b))[None, :]
            cfeat = _fox_gate(h, g_attn, wf_pad, bf_pad, tc=ts, n_heads=n_heads_b)
            pats = [_dilated_pattern(proj, 1, width=width, per_res=4, tq=min(1024, s))]
            pats += [_dilated_pattern(v, dil, width=width, per_res=3, tq=min(1024, s // dil))
                     for v, dil in zip(views, dils[1:])]
            ob = _flash_pair(proj_t, proj, (cfeat,), mode="fox", n_groups=n_pairs, gps=gps,
                             q_row=0, v_row=n_pairs // gps, k_col=3 * n_pairs // gps,
                             tq=tq_flash, tk=tk_flash)
            h = _hyb_out(pats, dils, ob, hyb_w_out[e].astype(BF16), h, tm=ts)
        else:
            o = l // 2
            w = diff_w_qkv[o]
            wq, wk, wv = w[:, :dq], w[:, dq:2 * dq], w[:, 2 * dq:]
            k_tiles = dq // PROJ_TILE
            keys = _norm_proj(h, g_attn, wk.astype(BF16), cos, sin_signed,
                              rope_tiles=tuple(range(k_tiles)), scale_tiles=(),
                              scale=1.0, tm=tm, tn=PROJ_TILE)
            proj_t = _norm_proj_t(h, g_attn, jnp.concatenate([wq, wv], axis=1).T.astype(BF16),
                                  cos_t, sin_t, rope_tiles=tuple(range(k_tiles)),
                                  scale_tiles=tuple(range(k_tiles)), scale=q_scale * LOG2E,
                                  tm=tm, tn=PROJ_TILE)
            lam_init = 0.8 - 0.6 * math.exp(-0.3 * l)
            att = _flash_pair(proj_t, keys, (diff_lambda[o], diff_subln[o][:, None]),
                              mode="diff", n_groups=n_diff_heads, gps=n_diff_heads, q_row=0,
                              v_row=1, k_col=0, tq=tq_flash, tk=tk_flash,
                              lam_init=lam_init)
            h = _proj_res(att, diff_w_out[o].astype(BF16), h, tm=tm)
        h = _ffn(h, ffn_norm[l][None, :], ffn_w_up[l].astype(BF16), ffn_conv_w[l],
                 ffn_conv_b[l][None, :], ffn_w_down[l].astype(BF16),
                 final_norm[None, :] if l == depth - 1 else None, tm=ts, n_chunks=1)
    return h[None]
```

```python
import functools
import math

import jax
import jax.numpy as jnp
from jax import lax
from jax.experimental import pallas as pl
from jax.experimental.pallas import tpu as pltpu

F32 = jnp.float32
BF16 = jnp.bfloat16

HEAD_DIM = 64
LANES = 128
BF16_SUBLANES = 16
ROPE_THETA = 10000.0
DILATED_PATTERNS = ((128, 1), (512, 4), (2048, 16))
BAND = 128
NORM_EPS = 1e-6
SUBLN_EPS = 1e-5
CONV_WIDTH = 3
NEG = -1e30
VMEM_LIMIT = 48 * 1024 * 1024
PROJ_TILE = 512

NT_DIMS = (((1,), (1,)), ((), ()))


def _params(sem):
    return pltpu.CompilerParams(dimension_semantics=sem, vmem_limit_bytes=VMEM_LIMIT)


def _rms(x, g, eps):
    return x * lax.rsqrt(jnp.mean(x * x, axis=-1, keepdims=True) + eps) * g


def _lane_lo(rows):
    return lax.broadcasted_iota(jnp.int32, (rows, LANES), 1) < HEAD_DIM


def _rope_tile(x, cos, sin_signed):
    rows = x.shape[0]
    first_half = (lax.broadcasted_iota(jnp.int32, (rows, LANES), 1) & 32) == 0
    out = []
    for c in range(x.shape[1] // LANES):
        xc = x[:, c * LANES:(c + 1) * LANES]
        ahead = pltpu.roll(xc, LANES - 32, 1)
        behind = pltpu.roll(xc, 32, 1)
        rot = jnp.where(first_half, ahead, behind)
        out.append(xc * cos + rot * sin_signed)
    return jnp.concatenate(out, axis=1)


def _norm_proj_kernel(h_ref, g_ref, w_ref, cos_ref, sin_ref, o_ref, *rest,
                      rope_tiles, scale_tiles, scale, tn, strided_tiles, dilations):
    n = _rms(h_ref[...], g_ref[...], NORM_EPS).astype(BF16)
    tm = h_ref.shape[0]
    for j in range(w_ref.shape[1] // tn):
        cols = slice(j * tn, (j + 1) * tn)
        acc = jnp.dot(n, w_ref[:, cols], preferred_element_type=F32)
        if j in rope_tiles:
            acc = _rope_tile(acc, cos_ref[...], sin_ref[...])
        if j in scale_tiles:
            acc = acc * scale
        o_ref[:, cols] = acc.astype(BF16)
        if j < strided_tiles:
            stage = rest[-1]
            for c in range(tn // LANES):
                stage[c] = acc[:, c * LANES:(c + 1) * LANES]
            for d_ref, dil in zip(rest[:-1], dilations):
                for r in range(dil):
                    for c in range(tn // LANES):
                        c0 = (r * strided_tiles + j) * tn + c * LANES
                        rows = stage[c, pl.ds(r, tm // dil, stride=dil), :]
                        d_ref[:, c0:c0 + LANES] = rows.astype(BF16)


def _norm_proj(h, g, w, cos, sin_signed, *, rope_tiles, scale_tiles, scale, tm, tn,
               strided_tiles=0, dilations=()):
    s, d = h.shape
    n = w.shape[1]
    kern = functools.partial(_norm_proj_kernel, rope_tiles=rope_tiles, scale_tiles=scale_tiles,
                             scale=scale, tn=tn, strided_tiles=strided_tiles,
                             dilations=dilations)
    out_specs = [pl.BlockSpec((tm, n), lambda i: (i, 0))]
    out_shape = [jax.ShapeDtypeStruct((s, n), BF16)]
    for dil in dilations:
        wide = dil * strided_tiles * tn
        out_specs.append(pl.BlockSpec((tm // dil, wide), lambda i: (i, 0)))
        out_shape.append(jax.ShapeDtypeStruct((s // dil, wide), BF16))
    outs = pl.pallas_call(
        kern,
        grid=(s // tm,),
        in_specs=[
            pl.BlockSpec((tm, d), lambda i: (i, 0)),
            pl.BlockSpec((1, d), lambda i: (0, 0)),
            pl.BlockSpec((d, n), lambda i: (0, 0)),
            pl.BlockSpec((tm, LANES), lambda i: (i, 0)),
            pl.BlockSpec((tm, LANES), lambda i: (i, 0)),
        ],
        out_specs=out_specs,
        out_shape=out_shape,
        scratch_shapes=[pltpu.VMEM((tn // LANES, tm, LANES), F32)] if dilations else [],
        compiler_params=_params(("parallel",)),
        name="norm_proj",
    )(h, g, w, cos, sin_signed)
    return outs if dilations else outs[0]


def _norm_proj_t_kernel(h_ref, g_ref, wt_ref, cos_ref, sin_ref, o_ref, *,
                        rope_tiles, scale_tiles, scale, tn):
    n = _rms(h_ref[...], g_ref[...], NORM_EPS).astype(BF16)
    half = HEAD_DIM // 2
    for j in range(wt_ref.shape[0] // tn):
        acc = lax.dot_general(wt_ref[j * tn:(j + 1) * tn, :], n, NT_DIMS,
                              preferred_element_type=F32)
        sc = scale if j in scale_tiles else 1.0
        if j in rope_tiles:
            cos, sin = cos_ref[...], sin_ref[...]
            for hd in range(tn // HEAD_DIM):
                r0 = j * tn + hd * HEAD_DIM
                x1 = acc[hd * HEAD_DIM:hd * HEAD_DIM + half]
                x2 = acc[hd * HEAD_DIM + half:(hd + 1) * HEAD_DIM]
                o_ref[r0:r0 + half] = ((x1 * cos - x2 * sin) * sc).astype(BF16)
                o_ref[r0 + half:r0 + HEAD_DIM] = ((x2 * cos + x1 * sin) * sc).astype(BF16)
        else:
            o_ref[j * tn:(j + 1) * tn] = (acc * sc).astype(BF16)


def _norm_proj_t(h, g, wt, cos_t, sin_t, *, rope_tiles, scale_tiles, scale, tm, tn):
    s, d = h.shape
    n = wt.shape[0]
    kern = functools.partial(_norm_proj_t_kernel, rope_tiles=rope_tiles,
                             scale_tiles=scale_tiles, scale=scale, tn=tn)
    return pl.pallas_call(
        kern,
        grid=(s // tm,),
        in_specs=[
            pl.BlockSpec((tm, d), lambda i: (i, 0)),
            pl.BlockSpec((1, d), lambda i: (0, 0)),
            pl.BlockSpec((n, d), lambda i: (0, 0)),
            pl.BlockSpec((HEAD_DIM // 2, tm), lambda i: (0, i)),
            pl.BlockSpec((HEAD_DIM // 2, tm), lambda i: (0, i)),
        ],
        out_specs=pl.BlockSpec((n, tm), lambda i: (0, i)),
        out_shape=jax.ShapeDtypeStruct((n, s), BF16),
        compiler_params=_params(("parallel",)),
        name="norm_proj_t",
    )(h, g, wt, cos_t, sin_t)


GATE_TERMS = 3
GATE_STRIDE = 8
LOG2E = math.log2(math.e)


def _fox_gate_kernel(h_ref, g_ref, wf_ref, bf_ref, cf_ref, carry, *, tc, n_heads):
    i = pl.program_id(0)

    @pl.when(i == 0)
    def _():
        carry[...] = jnp.zeros_like(carry)

    n = _rms(h_ref[...], g_ref[...], NORM_EPS).astype(BF16)
    z = jnp.dot(n, wf_ref[...], preferred_element_type=F32) + bf_ref[...]
    logf = jnp.minimum(z, 0.0) - jnp.log(1.0 + jnp.exp(-jnp.abs(z)))
    dst = lax.broadcasted_iota(jnp.int32, (tc, tc), 0)
    src = lax.broadcasted_iota(jnp.int32, (tc, tc), 1)
    prefix = jnp.where(src <= dst, 1.0, 0.0).astype(F32)
    cs = jnp.dot(prefix, logf, precision=lax.Precision.HIGHEST,
                 preferred_element_type=F32) + carry[0:1, :]
    carry[...] = jnp.broadcast_to(cs[tc - 1:tc, :], carry.shape)
    lane = lax.broadcasted_iota(jnp.int32, (tc, LANES), 1)
    rem = jnp.where(lane < n_heads, cs * LOG2E, 0.0)
    feat = jnp.zeros_like(rem)
    for term in range(GATE_TERMS):
        part = rem.astype(BF16).astype(F32)
        rem = rem - part
        feat = feat + (pltpu.roll(part, term * GATE_STRIDE, 1) if term else part)
    cf_ref[...] = feat.astype(BF16)


def _fox_gate(h, g, wf, b_f, *, tc, n_heads):
    s, d = h.shape
    return pl.pallas_call(
        functools.partial(_fox_gate_kernel, tc=tc, n_heads=n_heads),
        grid=(s // tc,),
        in_specs=[
            pl.BlockSpec((tc, d), lambda i: (i, 0)),
            pl.BlockSpec((1, d), lambda i: (0, 0)),
            pl.BlockSpec((d, LANES), lambda i: (0, 0)),
            pl.BlockSpec((1, LANES), lambda i: (0, 0)),
        ],
        out_specs=pl.BlockSpec((tc, LANES), lambda i: (i, 0)),
        out_shape=jax.ShapeDtypeStruct((s, LANES), BF16),
        scratch_shapes=[pltpu.VMEM((8, LANES), F32)],
        compiler_params=_params(("arbitrary",)),
        name="fox_gate",
    )(h, g, wf, b_f)


def _dilated_kernel(q_ref, kc_ref, kh_ref, vc_ref, vh_ref, o_ref, l_ref, kbuf, vbuf, *, tq):
    i = pl.program_id(1)
    kbuf[0:BAND] = kh_ref[...]
    kbuf[BAND:] = kc_ref[...]
    vbuf[0:BAND] = vh_ref[...]
    vbuf[BAND:] = vc_ref[...]
    row = lax.broadcasted_iota(jnp.int32, (BAND, 2 * BAND), 0)
    col = lax.broadcasted_iota(jnp.int32, (BAND, 2 * BAND), 1)
    delta = row - col + BAND
    band = (delta >= 0) & (delta <= BAND)
    lo = _lane_lo(BAND)
    for a in range(tq // BAND):
        first_key = i * tq + (a - 1) * BAND
        valid = band & (col + first_key >= 0)
        rows = slice(a * BAND, (a + 1) * BAND)
        for hp in range(q_ref.shape[1] // LANES):
            lanes = slice(hp * LANES, (hp + 1) * LANES)
            q = q_ref[rows, lanes]
            kk = kbuf[a * BAND:(a + 2) * BAND, lanes]
            vv = vbuf[a * BAND:(a + 2) * BAND, lanes]
            outs, lses = [], []
            for qh in (jnp.where(lo, q, jnp.zeros_like(q)), jnp.where(lo, jnp.zeros_like(q), q)):
                s = lax.dot_general(qh, kk, NT_DIMS, preferred_element_type=F32)
                s = jnp.where(valid, s, NEG)
                m = jnp.max(s, axis=1, keepdims=True)
                p = jnp.exp(s - m)
                den = jnp.sum(p, axis=1, keepdims=True)
                pv = jnp.dot(p.astype(BF16), vv, preferred_element_type=F32)
                outs.append(pv / den)
                lses.append(jnp.broadcast_to(m + jnp.log(den), (BAND, LANES)))
            o_ref[rows, lanes] = jnp.where(lo, outs[0], outs[1])
            l_ref[rows, lanes] = jnp.where(lo, lses[0], lses[1])


def _dilated_pattern(view, dil, *, width, per_res, tq):
    L = view.shape[0]
    halo_per_tile = tq // BAND
    cur = lambda part: pl.BlockSpec((tq, width), lambda r, i: (i, r * per_res + part))
    halo = lambda part: pl.BlockSpec(
        (BAND, width), lambda r, i: (jnp.maximum(i * halo_per_tile - 1, 0), r * per_res + part))
    out_spec = pl.BlockSpec((tq, width), lambda r, i: (i, r))
    return pl.pallas_call(
        functools.partial(_dilated_kernel, tq=tq),
        grid=(dil, L // tq),
        in_specs=[cur(0), cur(1), halo(1), cur(2), halo(2)],
        out_specs=[out_spec, out_spec],
        out_shape=[jax.ShapeDtypeStruct((L, dil * width), F32)] * 2,
        scratch_shapes=[pltpu.VMEM((tq + BAND, width), BF16)] * 2,
        compiler_params=_params(("parallel", "arbitrary")),
        name=f"dilated_d{dil}",
    )(view, view, view, view, view)


ONES_ROWS = BF16_SUBLANES


COL_TILE = 256
STRIPS_PER_TRIP = 4


def _flash_pair_kernel(qi_ref, kj_ref, qt_ref, k_ref, *rest, mode, tq, tk, gps, lam_init):
    if mode == "fox":
        cf_ref, vt_ref, o_ref, wq, m_scr, acc, kx_scr, vx_scr, *bufs = rest
    else:
        vt_ref, lam_ref, g_ref, o_ref, wq, m_scr, acc, kx_scr, vx_scr, *bufs = rest
    s_buf, mx_buf, p_buf, al_buf = bufs[0:2], bufs[2:4], bufs[4:6], bufs[6:8]
    U = STRIPS_PER_TRIP
    step = pl.program_id(1)
    qi = qi_ref[step]
    kj = kj_ref[step]
    last_kj = (qi * tq + tq - 1) // tk
    rows = acc.shape[1]
    dv = rows - ONES_ROWS
    n_ct = tq // COL_TILE
    n_strips = 2 * gps * n_ct
    n_trips = n_strips // U
    par = kj % 2

    @pl.when(kj == 0)
    def _():
        r = lax.broadcasted_iota(jnp.int32, (LANES, COL_TILE), 0)
        first = r < HEAD_DIM
        for gi in range(gps):
            for a in range(2):
                keep_q = first if a == 0 else jnp.logical_not(first)
                if mode == "fox":
                    head = 2 * (pl.program_id(0) * gps + gi) + a
                    pick = (r - head) == 0
                    for term in range(1, GATE_TERMS):
                        pick = pick | ((r - head) == term * GATE_STRIDE)
                    gate_rows = jnp.where(pick, -1.0, 0.0).astype(BF16)
                for ct in range(n_ct):
                    t = (2 * gi + a) * n_ct + ct
                    qt = qt_ref[gi * LANES:(gi + 1) * LANES, ct * COL_TILE:(ct + 1) * COL_TILE]
                    wq[t, 0:LANES] = jnp.where(keep_q, qt, jnp.zeros_like(qt))
                    if mode == "fox":
                        wq[t, LANES:2 * LANES] = gate_rows
        m_scr[...] = jnp.full_like(m_scr, NEG)
        acc[...] = jnp.zeros_like(acc)

    maps_per_v = 1 if mode == "fox" else 2
    ones = jnp.ones((ONES_ROWS, tk), BF16)
    for gi in range(gps):
        kx_scr[gi, :, 0:LANES] = k_ref[:, gi * LANES:(gi + 1) * LANES]
        if mode == "fox":
            kx_scr[gi, :, LANES:2 * LANES] = cf_ref[...]
            for a in range(2):
                v0 = gi * LANES + a * HEAD_DIM
                vx_scr[par, 2 * gi + a, 0:dv] = vt_ref[v0:v0 + HEAD_DIM, :]
                vx_scr[par, 2 * gi + a, dv:rows] = ones
        else:
            vx_scr[par, gi, 0:dv] = vt_ref[gi * LANES:(gi + 1) * LANES, :]
            vx_scr[par, gi, dv:rows] = ones

    def stage_a(j, slot, u, masked):
        s = jnp.dot(kx_scr[j // (2 * n_ct)], wq[j], preferred_element_type=F32)
        if masked:
            key = kj * tk + lax.broadcasted_iota(jnp.int32, (tk, COL_TILE), 0)
            qry = (qi * tq + (j % n_ct) * COL_TILE
                   + lax.broadcasted_iota(jnp.int32, (tk, COL_TILE), 1))
            s = jnp.where(key <= qry, s, NEG)
        s_buf[slot][u] = s
        mx_buf[slot][u] = jnp.broadcast_to(jnp.max(s, axis=0, keepdims=True), (8, COL_TILE))


    def stage_b(j0, slot, us):
        m_prevs = {u: m_scr[j0 + u] for u in us}
        for u, m_prev in list(m_prevs.items()):
            m_next = jnp.maximum(m_prev, mx_buf[slot][u])
            al_buf[slot][u] = jnp.exp2(m_prev - m_next)
            s3 = s_buf[slot][u].reshape(tk // 8, 8, COL_TILE)
            p_buf[slot][u] = jnp.exp2(s3 - m_next[None]).reshape(tk, COL_TILE).astype(BF16)
            m_prevs[u] = m_next
        for u, m_next in m_prevs.items():
            m_scr[j0 + u] = m_next

    def stage_c(j0, slot, vpar, us):
        pvs = [jnp.dot(vx_scr[vpar, (j0 + u) // (n_ct * maps_per_v)], p_buf[slot][u],
                       preferred_element_type=F32) for u in us]
        olds = [acc[j0 + u] for u in us]
        for u, old, pv in zip(us, olds, pvs):
            acc[j0 + u] = ((old.reshape(rows // 8, 8, COL_TILE) * al_buf[slot][u][None])
                           .reshape(rows, COL_TILE) + pv)

    def sweep(masked, first_step):
        def trip(k, slot, run_b=True, run_c=True):
            kc = (k + n_trips - 2) % n_trips
            kb = (k + n_trips - 1) % n_trips
            for u in range(U):
                stage_a(k * U + u, slot, u, masked)
            if run_b:
                stage_b(kb * U, 1 - slot, range(U))
            if run_c:
                stage_c(kc * U, slot, jnp.where(k >= 2, par, 1 - par), range(U))

        def trip_pair(half, carry):
            trip(2 * half, 0)
            trip(2 * half + 1, 1)
            return carry

        if first_step:
            trip(0, 0, run_b=False, run_c=False)
            trip(1, 1, run_c=False)
            lax.fori_loop(1, n_trips // 2, trip_pair, 0)
        else:
            lax.fori_loop(0, n_trips // 2, trip_pair, 0)

    assert U % n_ct == 0
    live = tuple(u for u in range(U) if (u % n_ct) * COL_TILE >= tq - tk)

    def last_sweep():
        def trip(k, slot, b_from_prev_step, c_from_prev_step):
            kc = (k + n_trips - 2) % n_trips
            kb = (k + n_trips - 1) % n_trips
            for u in live:
                stage_a(k * U + u, slot, u, True)
            stage_b(kb * U, 1 - slot, range(U) if b_from_prev_step else live)
            stage_c(kc * U, slot, 1 - par if c_from_prev_step else par,
                    range(U) if c_from_prev_step else live)

        def trip_pair(half, carry):
            trip(2 * half, 0, False, False)
            trip(2 * half + 1, 1, False, False)
            return carry

        trip(0, 0, True, True)
        trip(1, 1, False, True)
        lax.fori_loop(1, n_trips // 2, trip_pair, 0)

    needs_mask = kj * tk + tk - 1 > qi * tq
    is_last = kj == last_kj
    is_first = kj == 0
    for masked in (True, False):
        on_mask = jnp.logical_and(needs_mask, jnp.logical_not(is_last)) if masked \
            else jnp.logical_not(needs_mask)
        for first_step in (True, False):
            on_step = is_first if first_step else jnp.logical_not(is_first)
            pl.when(jnp.logical_and(on_mask, on_step))(
                functools.partial(sweep, masked, first_step))
    pl.when(is_last)(last_sweep)

    @pl.when(is_last)
    def _():
        last = n_trips - 1
        stage_c((last - 1) * U, (last - 1) % 2, par, live)
        stage_b(last * U, last % 2, live)
        stage_c(last * U, last % 2, par, live)

        def map_out(mi):
            parts = [acc[mi * n_ct + ct] for ct in range(n_ct)]
            full = jnp.concatenate(parts, axis=1)
            return full[0:dv] / full[dv:dv + 1]

        for gi in range(gps):
            o0, o1 = map_out(2 * gi), map_out(2 * gi + 1)
            if mode == "fox":
                o = jnp.concatenate([o0, o1], axis=0)
            else:
                lp = lam_ref[...]
                t1 = jnp.sum(lp[0:1] * lp[1:2], axis=1, keepdims=True)
                t2 = jnp.sum(lp[2:3] * lp[3:4], axis=1, keepdims=True)
                lam = jnp.exp(t1) - jnp.exp(t2) + lam_init
                o = o0 - lam * o1
                ms = jnp.mean(o * o, axis=0, keepdims=True)
                o = o * lax.rsqrt(ms + SUBLN_EPS) * g_ref[...] * (1.0 - lam_init)
            o_ref[:, gi * LANES:(gi + 1) * LANES] = o.T.astype(BF16)


def _causal_steps(s, tq, tk):
    qi, kj = [], []
    for i in range(s // tq):
        for j in range((i * tq + tq - 1) // tk + 1):
            qi.append(i)
            kj.append(j)
    return jnp.asarray(qi, jnp.int32), jnp.asarray(kj, jnp.int32)


def _flash_pair(proj_t, keys, extras, *, mode, n_groups, gps, q_row, v_row, k_col, tq, tk,
                lam_init=0.0):
    s = keys.shape[0]
    qi, kj = _causal_steps(s, tq, tk)
    gw = gps * LANES
    qt_spec = pl.BlockSpec((gw, tq), lambda g, t, qi, kj: (q_row + g, qi[t]))
    k_spec = pl.BlockSpec((tk, gw), lambda g, t, qi, kj: (kj[t], k_col + g))
    vt_spec = pl.BlockSpec((gw, tk), lambda g, t, qi, kj: (v_row + g, kj[t]))
    const = lambda x: pl.BlockSpec(x.shape, lambda g, t, qi, kj: (0, 0))
    if mode == "fox":
        (cfeat,) = extras
        in_specs = [qt_spec, k_spec,
                    pl.BlockSpec((tk, LANES), lambda g, t, qi, kj: (kj[t], 0)), vt_spec]
        args = (proj_t, keys, cfeat, proj_t)
        kd, dv, n_values = 2 * LANES, HEAD_DIM, 2 * gps
    else:
        lam_params, subln_g = extras
        in_specs = [qt_spec, k_spec, vt_spec, const(lam_params), const(subln_g)]
        args = (proj_t, keys, proj_t, lam_params, subln_g)
        kd, dv, n_values = LANES, LANES, gps
    rows = dv + ONES_ROWS
    n_strips = 2 * gps * (tq // COL_TILE)
    u = STRIPS_PER_TRIP
    assert n_strips % (2 * u) == 0
    assert tq >= 2 * tk
    kern = functools.partial(_flash_pair_kernel, mode=mode, tq=tq, tk=tk, gps=gps,
                             lam_init=lam_init)
    return pl.pallas_call(
        kern,
        grid_spec=pltpu.PrefetchScalarGridSpec(
            num_scalar_prefetch=2,
            grid=(n_groups // gps, qi.shape[0]),
            in_specs=in_specs,
            out_specs=pl.BlockSpec((tq, gw), lambda g, t, qi, kj: (qi[t], g)),
            scratch_shapes=[
                pltpu.VMEM((n_strips, kd, COL_TILE), BF16),
                pltpu.VMEM((n_strips, 8, COL_TILE), F32),
                pltpu.VMEM((n_strips, rows, COL_TILE), F32),
                pltpu.VMEM((gps, tk, kd), BF16),
                pltpu.VMEM((2, n_values, rows, tk), BF16),
            ] + [pltpu.VMEM((u, tk, COL_TILE), F32)] * 2
              + [pltpu.VMEM((u, 8, COL_TILE), F32)] * 2
              + [pltpu.VMEM((u, tk, COL_TILE), BF16)] * 2
              + [pltpu.VMEM((u, 8, COL_TILE), F32)] * 2,
        ),
        out_shape=jax.ShapeDtypeStruct((s, n_groups * LANES), BF16),
        compiler_params=_params(("arbitrary", "arbitrary")),
        name=f"flash_{mode}",
    )(qi, kj, *args)


def _hyb_out_kernel(*refs, dilations):
    n_pat = len(dilations)
    pat = refs[:2 * n_pat]
    ob_ref, w_ref, h_ref, out_ref = refs[2 * n_pat:2 * n_pat + 4]
    scratch = list(refs[2 * n_pat + 4:])
    tm = h_ref.shape[0]
    vals = []
    for idx, ref in enumerate(pat):
        dil = dilations[idx // 2]
        if dil == 1:
            vals.append(ref[...])
            continue
        buf = scratch.pop(0)
        n_chunks = buf.shape[0]
        for r in range(dil):
            for c in range(n_chunks):
                c0 = (r * n_chunks + c) * LANES
                buf[c, pl.ds(r, tm // dil, stride=dil), :] = ref[:, c0:c0 + LANES]
        vals.append(jnp.concatenate([buf[c] for c in range(n_chunks)], axis=1))
    os_, ls_ = vals[0::2], vals[1::2]
    m = functools.reduce(jnp.maximum, ls_)
    es = [jnp.exp(l - m) for l in ls_]
    oa = sum(e * o for e, o in zip(es, os_)) / sum(es)
    wa = oa.shape[1]
    acc = jnp.dot(oa.astype(BF16), w_ref[0:wa, :], preferred_element_type=F32)
    acc = acc + jnp.dot(ob_ref[...], w_ref[wa:, :], preferred_element_type=F32)
    out_ref[...] = h_ref[...] + acc


def _hyb_out(pattern_outs, dilations, ob, w, h, *, tm):
    s, d = h.shape
    wa = pattern_outs[0][0].shape[1] // dilations[0]
    in_specs, args, scratch = [], [], []
    for (o, lse), dil in zip(pattern_outs, dilations):
        for arr in (o, lse):
            in_specs.append(pl.BlockSpec((tm // dil, dil * wa), lambda i: (i, 0)))
            args.append(arr)
            if dil > 1:
                scratch.append(pltpu.VMEM((wa // LANES, tm, LANES), F32))
    return pl.pallas_call(
        functools.partial(_hyb_out_kernel, dilations=tuple(dilations)),
        grid=(s // tm,),
        in_specs=in_specs + [
            pl.BlockSpec((tm, ob.shape[1]), lambda i: (i, 0)),
            pl.BlockSpec(w.shape, lambda i: (0, 0)),
            pl.BlockSpec((tm, d), lambda i: (i, 0)),
        ],
        out_specs=pl.BlockSpec((tm, d), lambda i: (i, 0)),
        out_shape=jax.ShapeDtypeStruct((s, d), F32),
        scratch_shapes=scratch,
        compiler_params=_params(("parallel",)),
        name="hyb_out",
    )(*args, ob, w, h)


def _proj_res_kernel(a_ref, w_ref, h_ref, out_ref):
    out_ref[...] = h_ref[...] + jnp.dot(a_ref[...], w_ref[...], preferred_element_type=F32)


def _proj_res(a, w, h, *, tm):
    s, d = h.shape
    return pl.pallas_call(
        _proj_res_kernel,
        grid=(s // tm,),
        in_specs=[
            pl.BlockSpec((tm, a.shape[1]), lambda i: (i, 0)),
            pl.BlockSpec(w.shape, lambda i: (0, 0)),
            pl.BlockSpec((tm, d), lambda i: (i, 0)),
        ],
        out_specs=pl.BlockSpec((tm, d), lambda i: (i, 0)),
        out_shape=jax.ShapeDtypeStruct((s, d), F32),
        compiler_params=_params(("parallel",)),
        name="proj_res",
    )(a, w, h)


HALO = BF16_SUBLANES


FFN_SUB = 256


def _ffn_kernel(h_ref, halo_ref, g_ref, wg_ref, wu_ref, cw_ref, cb_ref, wd_ref, *rest,
                tm, final):
    if final:
        fg_ref, out_ref, n_scr, gate_scr, act_scr = rest
    else:
        out_ref, n_scr, gate_scr, act_scr = rest
    i = pl.program_id(0)
    j = pl.program_id(1)
    n_chunks, _, chunk = act_scr.shape

    @pl.when(j == 0)
    def _():
        g = g_ref[...]
        prev = jnp.where(i > 0, halo_ref[...], 0.0)
        n_scr[0:HALO] = _rms(prev, g, NORM_EPS).astype(BF16)
        n_scr[HALO:] = _rms(h_ref[...], g, NORM_EPS).astype(BF16)

    for t, c0 in enumerate(range(0, chunk, FFN_SUB)):
        w = min(FFN_SUB, chunk - c0)
        cols = slice(c0, c0 + w)
        gate_scr[t, :, 0:w] = jnp.dot(n_scr[...], wg_ref[:, cols], preferred_element_type=F32)
        up = jnp.dot(n_scr[HALO:], wu_ref[:, cols], preferred_element_type=F32)
        conv = cb_ref[:, cols]
        for k in range(CONV_WIDTH):
            start = HALO - (CONV_WIDTH - 1) + k
            conv = conv + gate_scr[t, start:start + tm, 0:w] * cw_ref[k:k + 1, cols]
        act_scr[j, :, cols] = (conv * (1.0 / (1.0 + jnp.exp(-conv))) * up).astype(BF16)

    @pl.when(j == n_chunks - 1)
    def _():
        act = jnp.concatenate([act_scr[t] for t in range(n_chunks)], axis=1)
        y = h_ref[...] + jnp.dot(act, wd_ref[...], preferred_element_type=F32)
        if final:
            y = _rms(y, fg_ref[...], NORM_EPS)
        out_ref[...] = y


def _ffn(h, g, w_up, conv_w, conv_b, w_down, final_g, *, tm, n_chunks):
    s, d = h.shape
    d_ff = w_down.shape[0]
    chunk = d_ff // n_chunks
    assert chunk * n_chunks == d_ff and chunk % LANES == 0
    n_sub = -(-chunk // FFN_SUB)
    final = final_g is not None
    once = pl.Buffered(1)
    up_mode = once if n_chunks == 1 else None
    in_specs = [
        pl.BlockSpec((tm, d), lambda i, j: (i, 0)),
        pl.BlockSpec((HALO, d), lambda i, j: (jnp.maximum(i * (tm // HALO) - 1, 0), 0)),
        pl.BlockSpec((1, d), lambda i, j: (0, 0)),
        pl.BlockSpec((d, chunk), lambda i, j: (0, j), pipeline_mode=up_mode),
        pl.BlockSpec((d, chunk), lambda i, j: (0, n_chunks + j), pipeline_mode=up_mode),
        pl.BlockSpec((CONV_WIDTH, chunk), lambda i, j: (0, j)),
        pl.BlockSpec((1, chunk), lambda i, j: (0, j)),
        pl.BlockSpec((d_ff, d), lambda i, j: (0, 0), pipeline_mode=once),
    ]
    args = [h, h, g, w_up, w_up, conv_w, conv_b, w_down]
    if final:
        in_specs.append(pl.BlockSpec((1, d), lambda i, j: (0, 0)))
        args.append(final_g)
    return pl.pallas_call(
        functools.partial(_ffn_kernel, tm=tm, final=final),
        grid=(s // tm, n_chunks),
        in_specs=in_specs,
        out_specs=pl.BlockSpec((tm, d), lambda i, j: (i, 0)),
        out_shape=jax.ShapeDtypeStruct((s, d), F32),
        scratch_shapes=[
            pltpu.VMEM((tm + HALO, d), BF16),
            pltpu.VMEM((n_sub, tm + HALO, FFN_SUB), F32),
            pltpu.VMEM((n_chunks, tm, chunk), BF16),
        ],
        compiler_params=_params(("parallel", "arbitrary")),
        name="ffn_final" if final else "ffn",
    )(*args)


def _rope_tables(s):
    inv = 1.0 / (ROPE_THETA ** (jnp.arange(0, HEAD_DIM, 2, dtype=F32) / HEAD_DIM))
    ang = jnp.arange(s, dtype=F32)[:, None] * inv[None, :]
    cos, sin = jnp.cos(ang), jnp.sin(ang)
    sign = jnp.where((jnp.arange(LANES) & 32) == 0, -1.0, 1.0).astype(F32)
    cos_l = jnp.tile(cos, (1, LANES // 32))
    sin_signed = jnp.tile(sin, (1, LANES // 32)) * sign[None, :]
    return cos_l, sin_signed, cos.T, sin.T


def kernel(x, attn_norm, ffn_norm, final_norm, hyb_w_in, hyb_b_f, hyb_w_out, diff_w_qkv,
           diff_lambda, diff_subln, diff_w_out, ffn_w_up, ffn_conv_w, ffn_conv_b, ffn_w_down):
    b, s, d = x.shape
    assert b == 1
    depth = attn_norm.shape[0]
    width = hyb_w_out.shape[1] // 2
    n_pairs = width // LANES
    n_heads_b = width // HEAD_DIM
    n_diff_heads = diff_w_out.shape[1] // LANES
    dq = n_diff_heads * LANES
    tm = min(1024, s)
    ts = min(512, s)
    tq_flash, tk_flash = min(1024, s), min(512, s)
    gps = 4
    q_scale = HEAD_DIM ** -0.5
    cos, sin_signed, cos_t, sin_t = _rope_tables(s)
    h = x[0]

    for l in range(depth):
        g_attn = attn_norm[l][None, :]
        if l % 2 == 0:
            e = l // 2
            w_in = hyb_w_in[e]
            qa_ka_va, qb, kb, vb, wf = (w_in[:, :3 * width], w_in[:, 3 * width:4 * width],
                                        w_in[:, 4 * width:5 * width], w_in[:, 5 * width:6 * width],
                                        w_in[:, 6 * width:])
            dils = tuple(dil for _, dil in DILATED_PATTERNS)
            proj, *views = _norm_proj(
                h, g_attn, jnp.concatenate([qa_ka_va, kb], axis=1).astype(BF16), cos, sin_signed,
                rope_tiles=(0, 1), scale_tiles=(0,), scale=q_scale, tm=tm, tn=PROJ_TILE,
                strided_tiles=3, dilations=dils[1:])
            proj_t = _norm_proj_t(h, g_attn, jnp.concatenate([qb, vb], axis=1).T.astype(BF16),
                                  cos_t, sin_t, rope_tiles=(), scale_tiles=(0,),
                                  scale=q_scale * LOG2E, tm=tm, tn=PROJ_TILE)
            wf_pad = jnp.pad(wf, ((0, 0), (0, LANES - n_heads_b))).astype(BF16)
            bf_pad = jnp.pad(hyb_b_f[e], (0, LANES - n_heads_b))[None, :]
            cfeat = _fox_gate(h, g_attn, wf_pad, bf_pad, tc=ts, n_heads=n_heads_b)
            pats = [_dilated_pattern(proj, 1, width=width, per_res=4, tq=min(1024, s))]
            pats += [_dilated_pattern(v, dil, width=width, per_res=3, tq=min(1024, s // dil))
                     for v, dil in zip(views, dils[1:])]
            ob = _flash_pair(proj_t, proj, (cfeat,), mode="fox", n_groups=n_pairs, gps=gps,
                             q_row=0, v_row=n_pairs // gps, k_col=3 * n_pairs // gps,
                             tq=tq_flash, tk=tk_flash)
            h = _hyb_out(pats, dils, ob, hyb_w_out[e].astype(BF16), h, tm=ts)
        else:
            o = l // 2
            w = diff_w_qkv[o]
            wq, wk, wv = w[:, :dq], w[:, dq:2 * dq], w[:, 2 * dq:]
            k_tiles = dq // PROJ_TILE
            keys = _norm_proj(h, g_attn, wk.astype(BF16), cos, sin_signed,
                              rope_tiles=tuple(range(k_tiles)), scale_tiles=(),
                              scale=1.0, tm=tm, tn=PROJ_TILE)
            proj_t = _norm_proj_t(h, g_attn, jnp.concatenate([wq, wv], axis=1).T.astype(BF16),
                                  cos_t, sin_t, rope_tiles=tuple(range(k_tiles)),
                                  scale_tiles=tuple(range(k_tiles)), scale=q_scale * LOG2E,
                                  tm=tm, tn=PROJ_TILE)
            lam_init = 0.8 - 0.6 * math.exp(-0.3 * l)
            att = _flash_pair(proj_t, keys, (diff_lambda[o], diff_subln[o][:, None]),
                              mode="diff", n_groups=n_diff_heads, gps=n_diff_heads, q_row=0,
                              v_row=1, k_col=0, tq=tq_flash, tk=tk_flash,
                              lam_init=lam_init)
            h = _proj_res(att, diff_w_out[o].astype(BF16), h, tm=tm)
        h = _ffn(h, ffn_norm[l][None, :], ffn_w_up[l].astype(BF16), ffn_conv_w[l],
                 ffn_conv_b[l][None, :], ffn_w_down[l].astype(BF16),
                 final_norm[None, :] if l == depth - 1 else None, tm=ts, n_chunks=1)
    return h[None]
```

```python
import functools
import math

import jax
import jax.numpy as jnp
from jax import lax
from jax.experimental import pallas as pl
from jax.experimental.pallas import tpu as pltpu

F32 = jnp.float32
BF16 = jnp.bfloat16

HEAD_DIM = 64
LANES = 128
BF16_SUBLANES = 16
ROPE_THETA = 10000.0
DILATED_PATTERNS = ((128, 1), (512, 4), (2048, 16))
BAND = 128
NORM_EPS = 1e-6
SUBLN_EPS = 1e-5
CONV_WIDTH = 3
NEG = -1e30
VMEM_LIMIT = 48 * 1024 * 1024
PROJ_TILE = 512

NT_DIMS = (((1,), (1,)), ((), ()))


def _params(sem):
    return pltpu.CompilerParams(dimension_semantics=sem, vmem_limit_bytes=VMEM_LIMIT)


def _rms(x, g, eps):
    return x * lax.rsqrt(jnp.mean(x * x, axis=-1, keepdims=True) + eps) * g


def _lane_lo(rows):
    return lax.broadcasted_iota(jnp.int32, (rows, LANES), 1) < HEAD_DIM


def _rope_tile(x, cos, sin_signed):
    rows = x.shape[0]
    first_half = (lax.broadcasted_iota(jnp.int32, (rows, LANES), 1) & 32) == 0
    out = []
    for c in range(x.shape[1] // LANES):
        xc = x[:, c * LANES:(c + 1) * LANES]
        ahead = pltpu.roll(xc, LANES - 32, 1)
        behind = pltpu.roll(xc, 32, 1)
        rot = jnp.where(first_half, ahead, behind)
        out.append(xc * cos + rot * sin_signed)
    return jnp.concatenate(out, axis=1)


def _norm_proj_kernel(h_ref, g_ref, w_ref, cos_ref, sin_ref, o_ref, *rest,
                      rope_tiles, scale_tiles, scale, tn, strided_tiles, dilations):
    n = _rms(h_ref[...], g_ref[...], NORM_EPS).astype(BF16)
    tm = h_ref.shape[0]
    for j in range(w_ref.shape[1] // tn):
        cols = slice(j * tn, (j + 1) * tn)
        acc = jnp.dot(n, w_ref[:, cols], preferred_element_type=F32)
        if j in rope_tiles:
            acc = _rope_tile(acc, cos_ref[...], sin_ref[...])
        if j in scale_tiles:
            acc = acc * scale
        o_ref[:, cols] = acc.astype(BF16)
        if j < strided_tiles:
            stage = rest[-1]
            for c in range(tn // LANES):
                stage[c] = acc[:, c * LANES:(c + 1) * LANES]
            for d_ref, dil in zip(rest[:-1], dilations):
                for r in range(dil):
                    for c in range(tn // LANES):
                        c0 = (r * strided_tiles + j) * tn + c * LANES
                        rows = stage[c, pl.ds(r, tm // dil, stride=dil), :]
                        d_ref[:, c0:c0 + LANES] = rows.astype(BF16)


def _norm_proj(h, g, w, cos, sin_signed, *, rope_tiles, scale_tiles, scale, tm, tn,
               strided_tiles=0, dilations=()):
    s, d = h.shape
    n = w.shape[1]
    kern = functools.partial(_norm_proj_kernel, rope_tiles=rope_tiles, scale_tiles=scale_tiles,
                             scale=scale, tn=tn, strided_tiles=strided_tiles,
                             dilations=dilations)
    out_specs = [pl.BlockSpec((tm, n), lambda i: (i, 0))]
    out_shape = [jax.ShapeDtypeStruct((s, n), BF16)]
    for dil in dilations:
        wide = dil * strided_tiles * tn
        out_specs.append(pl.BlockSpec((tm // dil, wide), lambda i: (i, 0)))
        out_shape.append(jax.ShapeDtypeStruct((s // dil, wide), BF16))
    outs = pl.pallas_call(
        kern,
        grid=(s // tm,),
        in_specs=[
            pl.BlockSpec((tm, d), lambda i: (i, 0)),
            pl.BlockSpec((1, d), lambda i: (0, 0)),
            pl.BlockSpec((d, n), lambda i: (0, 0)),
            pl.BlockSpec((tm, LANES), lambda i: (i, 0)),
            pl.BlockSpec((tm, LANES), lambda i: (i, 0)),
        ],
        out_specs=out_specs,
        out_shape=out_shape,
        scratch_shapes=[pltpu.VMEM((tn // LANES, tm, LANES), F32)] if dilations else [],
        compiler_params=_params(("parallel",)),
        name="norm_proj",
    )(h, g, w, cos, sin_signed)
    return outs if dilations else outs[0]


def _norm_proj_t_kernel(h_ref, g_ref, wt_ref, cos_ref, sin_ref, o_ref, *,
                        rope_tiles, scale_tiles, scale, tn):
    n = _rms(h_ref[...], g_ref[...], NORM_EPS).astype(BF16)
    half = HEAD_DIM // 2
    for j in range(wt_ref.shape[0] // tn):
        acc = lax.dot_general(wt_ref[j * tn:(j + 1) * tn, :], n, NT_DIMS,
                              preferred_element_type=F32)
        sc = scale if j in scale_tiles else 1.0
        if j in rope_tiles:
            cos, sin = cos_ref[...], sin_ref[...]
            for hd in range(tn // HEAD_DIM):
                r0 = j * tn + hd * HEAD_DIM
                x1 = acc[hd * HEAD_DIM:hd * HEAD_DIM + half]
                x2 = acc[hd * HEAD_DIM + half:(hd + 1) * HEAD_DIM]
                o_ref[r0:r0 + half] = ((x1 * cos - x2 * sin) * sc).astype(BF16)
                o_ref[r0 + half:r0 + HEAD_DIM] = ((x2 * cos + x1 * sin) * sc).astype(BF16)
        else:
            o_ref[j * tn:(j + 1) * tn] = (acc * sc).astype(BF16)


def _norm_proj_t(h, g, wt, cos_t, sin_t, *, rope_tiles, scale_tiles, scale, tm, tn):
    s, d = h.shape
    n = wt.shape[0]
    kern = functools.partial(_norm_proj_t_kernel, rope_tiles=rope_tiles,
                             scale_tiles=scale_tiles, scale=scale, tn=tn)
    return pl.pallas_call(
        kern,
        grid=(s // tm,),
        in_specs=[
            pl.BlockSpec((tm, d), lambda i: (i, 0)),
            pl.BlockSpec((1, d), lambda i: (0, 0)),
            pl.BlockSpec((n, d), lambda i: (0, 0)),
            pl.BlockSpec((HEAD_DIM // 2, tm), lambda i: (0, i)),
            pl.BlockSpec((HEAD_DIM // 2, tm), lambda i: (0, i)),
        ],
        out_specs=pl.BlockSpec((n, tm), lambda i: (0, i)),
        out_shape=jax.ShapeDtypeStruct((n, s), BF16),
        compiler_params=_params(("parallel",)),
        name="norm_proj_t",
    )(h, g, wt, cos_t, sin_t)


GATE_TERMS = 3
GATE_STRIDE = 8
LOG2E = math.log2(math.e)


def _fox_gate_kernel(h_ref, g_ref, wf_ref, bf_ref, cf_ref, carry, *, tc, n_heads):
    i = pl.program_id(0)

    @pl.when(i == 0)
    def _():
        carry[...] = jnp.zeros_like(carry)

    n = _rms(h_ref[...], g_ref[...], NORM_EPS).astype(BF16)
    z = jnp.dot(n, wf_ref[...], preferred_element_type=F32) + bf_ref[...]
    logf = jnp.minimum(z, 0.0) - jnp.log(1.0 + jnp.exp(-jnp.abs(z)))
    dst = lax.broadcasted_iota(jnp.int32, (tc, tc), 0)
    src = lax.broadcasted_iota(jnp.int32, (tc, tc), 1)
    prefix = jnp.where(src <= dst, 1.0, 0.0).astype(F32)
    cs = jnp.dot(prefix, logf, precision=lax.Precision.HIGHEST,
                 preferred_element_type=F32) + carry[0:1, :]
    carry[...] = jnp.broadcast_to(cs[tc - 1:tc, :], carry.shape)
    lane = lax.broadcasted_iota(jnp.int32, (tc, LANES), 1)
    rem = jnp.where(lane < n_heads, cs * LOG2E, 0.0)
    feat = jnp.zeros_like(rem)
    for term in range(GATE_TERMS):
        part = rem.astype(BF16).astype(F32)
        rem = rem - part
        feat = feat + (pltpu.roll(part, term * GATE_STRIDE, 1) if term else part)
    cf_ref[...] = feat.astype(BF16)


def _fox_gate(h, g, wf, b_f, *, tc, n_heads):
    s, d = h.shape
    return pl.pallas_call(
        functools.partial(_fox_gate_kernel, tc=tc, n_heads=n_heads),
        grid=(s // tc,),
        in_specs=[
            pl.BlockSpec((tc, d), lambda i: (i, 0)),
            pl.BlockSpec((1, d), lambda i: (0, 0)),
            pl.BlockSpec((d, LANES), lambda i: (0, 0)),
            pl.BlockSpec((1, LANES), lambda i: (0, 0)),
        ],
        out_specs=pl.BlockSpec((tc, LANES), lambda i: (i, 0)),
        out_shape=jax.ShapeDtypeStruct((s, LANES), BF16),
        scratch_shapes=[pltpu.VMEM((8, LANES), F32)],
        compiler_params=_params(("arbitrary",)),
        name="fox_gate",
    )(h, g, wf, b_f)


def _dilated_kernel(q_ref, kc_ref, kh_ref, vc_ref, vh_ref, o_ref, l_ref, kbuf, vbuf, *, tq):
    i = pl.program_id(1)
    kbuf[0:BAND] = kh_ref[...]
    kbuf[BAND:] = kc_ref[...]
    vbuf[0:BAND] = vh_ref[...]
    vbuf[BAND:] = vc_ref[...]
    row = lax.broadcasted_iota(jnp.int32, (BAND, 2 * BAND), 0)
    col = lax.broadcasted_iota(jnp.int32, (BAND, 2 * BAND), 1)
    delta = row - col + BAND
    band = (delta >= 0) & (delta <= BAND)
    lo = _lane_lo(BAND)
    for a in range(tq // BAND):
        first_key = i * tq + (a - 1) * BAND
        valid = band & (col + first_key >= 0)
        rows = slice(a * BAND, (a + 1) * BAND)
        for hp in range(q_ref.shape[1] // LANES):
            lanes = slice(hp * LANES, (hp + 1) * LANES)
            q = q_ref[rows, lanes]
            kk = kbuf[a * BAND:(a + 2) * BAND, lanes]
            vv = vbuf[a * BAND:(a + 2) * BAND, lanes]
            outs, lses = [], []
            for qh in (jnp.where(lo, q, jnp.zeros_like(q)), jnp.where(lo, jnp.zeros_like(q), q)):
                s = lax.dot_general(qh, kk, NT_DIMS, preferred_element_type=F32)
                s = jnp.where(valid, s, NEG)
                m = jnp.max(s, axis=1, keepdims=True)
                p = jnp.exp(s - m)
                den = jnp.sum(p, axis=1, keepdims=True)
                pv = jnp.dot(p.astype(BF16), vv, preferred_element_type=F32)
                outs.append(pv / den)
                lses.append(jnp.broadcast_to(m + jnp.log(den), (BAND, LANES)))
            o_ref[rows, lanes] = jnp.where(lo, outs[0], outs[1])
            l_ref[rows, lanes] = jnp.where(lo, lses[0], lses[1])


def _dilated_pattern(view, dil, *, width, per_res, tq):
    L = view.shape[0]
    halo_per_tile = tq // BAND
    cur = lambda part: pl.BlockSpec((tq, width), lambda r, i: (i, r * per_res + part))
    halo = lambda part: pl.BlockSpec(
        (BAND, width), lambda r, i: (jnp.maximum(i * halo_per_tile - 1, 0), r * per_res + part))
    out_spec = pl.BlockSpec((tq, width), lambda r, i: (i, r))
    return pl.pallas_call(
        functools.partial(_dilated_kernel, tq=tq),
        grid=(dil, L // tq),
        in_specs=[cur(0), cur(1), halo(1), cur(2), halo(2)],
        out_specs=[out_spec, out_spec],
        out_shape=[jax.ShapeDtypeStruct((L, dil * width), F32)] * 2,
        scratch_shapes=[pltpu.VMEM((tq + BAND, width), BF16)] * 2,
        compiler_params=_params(("parallel", "arbitrary")),
        name=f"dilated_d{dil}",
    )(view, view, view, view, view)


ONES_ROWS = BF16_SUBLANES


COL_TILE = 256
STRIPS_PER_TRIP = 4


def _flash_pair_kernel(qi_ref, kj_ref, qt_ref, k_ref, *rest, mode, tq, tk, gps, lam_init):
    if mode == "fox":
        cf_ref, vt_ref, o_ref, wq, m_scr, acc, kx_scr, vx_scr, *bufs = rest
    else:
        vt_ref, lam_ref, g_ref, o_ref, wq, m_scr, acc, kx_scr, vx_scr, *bufs = rest
    s_buf, mx_buf, p_buf, al_buf = bufs[0:2], bufs[2:4], bufs[4:6], bufs[6:8]
    U = STRIPS_PER_TRIP
    step = pl.program_id(1)
    qi = qi_ref[step]
    kj = kj_ref[step]
    last_kj = (qi * tq + tq - 1) // tk
    rows = acc.shape[1]
    dv = rows - ONES_ROWS
    n_ct = tq // COL_TILE
    n_strips = 2 * gps * n_ct
    n_trips = n_strips // U
    par = kj % 2

    @pl.when(kj == 0)
    def _():
        r = lax.broadcasted_iota(jnp.int32, (LANES, COL_TILE), 0)
        first = r < HEAD_DIM
        for gi in range(gps):
            for a in range(2):
                keep_q = first if a == 0 else jnp.logical_not(first)
                if mode == "fox":
                    head = 2 * (pl.program_id(0) * gps + gi) + a
                    pick = (r - head) == 0
                    for term in range(1, GATE_TERMS):
                        pick = pick | ((r - head) == term * GATE_STRIDE)
                    gate_rows = jnp.where(pick, -1.0, 0.0).astype(BF16)
                for ct in range(n_ct):
                    t = (2 * gi + a) * n_ct + ct
                    qt = qt_ref[gi * LANES:(gi + 1) * LANES, ct * COL_TILE:(ct + 1) * COL_TILE]
                    wq[t, 0:LANES] = jnp.where(keep_q, qt, jnp.zeros_like(qt))
                    if mode == "fox":
                        wq[t, LANES:2 * LANES] = gate_rows
        m_scr[...] = jnp.full_like(m_scr, NEG)
        acc[...] = jnp.zeros_like(acc)

    maps_per_v = 1 if mode == "fox" else 2
    ones = jnp.ones((ONES_ROWS, tk), BF16)
    for gi in range(gps):
        kx_scr[gi, :, 0:LANES] = k_ref[:, gi * LANES:(gi + 1) * LANES]
        if mode == "fox":
            kx_scr[gi, :, LANES:2 * LANES] = cf_ref[...]
            for a in range(2):
                v0 = gi * LANES + a * HEAD_DIM
                vx_scr[par, 2 * gi + a, 0:dv] = vt_ref[v0:v0 + HEAD_DIM, :]
                vx_scr[par, 2 * gi + a, dv:rows] = ones
        else:
            vx_scr[par, gi, 0:dv] = vt_ref[gi * LANES:(gi + 1) * LANES, :]
            vx_scr[par, gi, dv:rows] = ones

    def stage_a(j, slot, u, masked):
        s = jnp.dot(kx_scr[j // (2 * n_ct)], wq[j], preferred_element_type=F32)
        if masked:
            key = kj * tk + lax.broadcasted_iota(jnp.int32, (tk, COL_TILE), 0)
            qry = (qi * tq + (j % n_ct) * COL_TILE
                   + lax.broadcasted_iota(jnp.int32, (tk, COL_TILE), 1))
            s = jnp.where(key <= qry, s, NEG)
        s_buf[slot][u] = s
        mx_buf[slot][u] = jnp.broadcast_to(jnp.max(s, axis=0, keepdims=True), (8, COL_TILE))


    def stage_b(j0, slot, us):
        m_prevs = {u: m_scr[j0 + u] for u in us}
        for u, m_prev in list(m_prevs.items()):
            m_next = jnp.maximum(m_prev, mx_buf[slot][u])
            al_buf[slot][u] = jnp.exp2(m_prev - m_next)
            s3 = s_buf[slot][u].reshape(tk // 8, 8, COL_TILE)
            p_buf[slot][u] = jnp.exp2(s3 - m_next[None]).reshape(tk, COL_TILE).astype(BF16)
            m_prevs[u] = m_next
        for u, m_next in m_prevs.items():
            m_scr[j0 + u] = m_next

    def stage_c(j0, slot, vpar, us):
        pvs = [jnp.dot(vx_scr[vpar, (j0 + u) // (n_ct * maps_per_v)], p_buf[slot][u],
                       preferred_element_type=F32) for u in us]
        olds = [acc[j0 + u] for u in us]
        for u, old, pv in zip(us, olds, pvs):
            acc[j0 + u] = ((old.reshape(rows // 8, 8, COL_TILE) * al_buf[slot][u][None])
                           .reshape(rows, COL_TILE) + pv)

    def sweep(masked, first_step):
        def trip(k, slot, run_b=True, run_c=True):
            kc = (k + n_trips - 2) % n_trips
            kb = (k + n_trips - 1) % n_trips
            for u in range(U):
                sees_all = tq == 2 * tk and (u % n_ct) * COL_TILE >= tk - 1
                stage_a(k * U + u, slot, u, masked and not sees_all)
            if run_b:
                stage_b(kb * U, 1 - slot, range(U))
            if run_c:
                stage_c(kc * U, slot, jnp.where(k >= 2, par, 1 - par), range(U))

        def trip_pair(half, carry):
            trip(2 * half, 0)
            trip(2 * half + 1, 1)
            return carry

        if first_step:
            trip(0, 0, run_b=False, run_c=False)
            trip(1, 1, run_c=False)
            lax.fori_loop(1, n_trips // 2, trip_pair, 0)
        else:
            lax.fori_loop(0, n_trips // 2, trip_pair, 0)

    assert U % n_ct == 0
    live = tuple(u for u in range(U) if (u % n_ct) * COL_TILE >= tq - tk)

    def last_sweep():
        def trip(k, slot, b_from_prev_step, c_from_prev_step):
            kc = (k + n_trips - 2) % n_trips
            kb = (k + n_trips - 1) % n_trips
            for u in live:
                stage_a(k * U + u, slot, u, True)
            stage_b(kb * U, 1 - slot, range(U) if b_from_prev_step else live)
            stage_c(kc * U, slot, 1 - par if c_from_prev_step else par,
                    range(U) if c_from_prev_step else live)

        def trip_pair(half, carry):
            trip(2 * half, 0, False, False)
            trip(2 * half + 1, 1, False, False)
            return carry

        trip(0, 0, True, True)
        trip(1, 1, False, True)
        lax.fori_loop(1, n_trips // 2, trip_pair, 0)

    needs_mask = kj * tk + tk - 1 > qi * tq
    is_last = kj == last_kj
    is_first = kj == 0
    for masked in (True, False):
        on_mask = jnp.logical_and(needs_mask, jnp.logical_not(is_last)) if masked \
            else jnp.logical_not(needs_mask)
        for first_step in (True, False):
            on_step = is_first if first_step else jnp.logical_not(is_first)
            pl.when(jnp.logical_and(on_mask, on_step))(
                functools.partial(sweep, masked, first_step))
    pl.when(is_last)(last_sweep)

    @pl.when(is_last)
    def _():
        last = n_trips - 1
        stage_c((last - 1) * U, (last - 1) % 2, par, live)
        stage_b(last * U, last % 2, live)
        stage_c(last * U, last % 2, par, live)

        def map_out(mi):
            parts = [acc[mi * n_ct + ct] for ct in range(n_ct)]
            full = jnp.concatenate(parts, axis=1)
            return full[0:dv] / full[dv:dv + 1]

        for gi in range(gps):
            o0, o1 = map_out(2 * gi), map_out(2 * gi + 1)
            if mode == "fox":
                o = jnp.concatenate([o0, o1], axis=0)
            else:
                lp = lam_ref[...]
                t1 = jnp.sum(lp[0:1] * lp[1:2], axis=1, keepdims=True)
                t2 = jnp.sum(lp[2:3] * lp[3:4], axis=1, keepdims=True)
                lam = jnp.exp(t1) - jnp.exp(t2) + lam_init
                o = o0 - lam * o1
                ms = jnp.mean(o * o, axis=0, keepdims=True)
                o = o * lax.rsqrt(ms + SUBLN_EPS) * g_ref[...] * (1.0 - lam_init)
            o_ref[:, gi * LANES:(gi + 1) * LANES] = o.T.astype(BF16)


def _causal_steps(s, tq, tk):
    qi, kj = [], []
    for i in range(s // tq):
        for j in range((i * tq + tq - 1) // tk + 1):
            qi.append(i)
            kj.append(j)
    return jnp.asarray(qi, jnp.int32), jnp.asarray(kj, jnp.int32)


def _flash_pair(proj_t, keys, extras, *, mode, n_groups, gps, q_row, v_row, k_col, tq, tk,
                lam_init=0.0):
    s = keys.shape[0]
    qi, kj = _causal_steps(s, tq, tk)
    gw = gps * LANES
    qt_spec = pl.BlockSpec((gw, tq), lambda g, t, qi, kj: (q_row + g, qi[t]))
    k_spec = pl.BlockSpec((tk, gw), lambda g, t, qi, kj: (kj[t], k_col + g))
    vt_spec = pl.BlockSpec((gw, tk), lambda g, t, qi, kj: (v_row + g, kj[t]))
    const = lambda x: pl.BlockSpec(x.shape, lambda g, t, qi, kj: (0, 0))
    if mode == "fox":
        (cfeat,) = extras
        in_specs = [qt_spec, k_spec,
                    pl.BlockSpec((tk, LANES), lambda g, t, qi, kj: (kj[t], 0)), vt_spec]
        args = (proj_t, keys, cfeat, proj_t)
        kd, dv, n_values = 2 * LANES, HEAD_DIM, 2 * gps
    else:
        lam_params, subln_g = extras
        in_specs = [qt_spec, k_spec, vt_spec, const(lam_params), const(subln_g)]
        args = (proj_t, keys, proj_t, lam_params, subln_g)
        kd, dv, n_values = LANES, LANES, gps
    rows = dv + ONES_ROWS
    n_strips = 2 * gps * (tq // COL_TILE)
    u = STRIPS_PER_TRIP
    assert n_strips % (2 * u) == 0
    assert tq >= 2 * tk
    kern = functools.partial(_flash_pair_kernel, mode=mode, tq=tq, tk=tk, gps=gps,
                             lam_init=lam_init)
    return pl.pallas_call(
        kern,
        grid_spec=pltpu.PrefetchScalarGridSpec(
            num_scalar_prefetch=2,
            grid=(n_groups // gps, qi.shape[0]),
            in_specs=in_specs,
            out_specs=pl.BlockSpec((tq, gw), lambda g, t, qi, kj: (qi[t], g)),
            scratch_shapes=[
                pltpu.VMEM((n_strips, kd, COL_TILE), BF16),
                pltpu.VMEM((n_strips, 8, COL_TILE), F32),
                pltpu.VMEM((n_strips, rows, COL_TILE), F32),
                pltpu.VMEM((gps, tk, kd), BF16),
                pltpu.VMEM((2, n_values, rows, tk), BF16),
            ] + [pltpu.VMEM((u, tk, COL_TILE), F32)] * 2
              + [pltpu.VMEM((u, 8, COL_TILE), F32)] * 2
              + [pltpu.VMEM((u, tk, COL_TILE), BF16)] * 2
              + [pltpu.VMEM((u, 8, COL_TILE), F32)] * 2,
        ),
        out_shape=jax.ShapeDtypeStruct((s, n_groups * LANES), BF16),
        compiler_params=_params(("arbitrary", "arbitrary")),
        name=f"flash_{mode}",
    )(qi, kj, *args)


def _hyb_out_kernel(*refs, dilations):
    n_pat = len(dilations)
    pat = refs[:2 * n_pat]
    ob_ref, w_ref, h_ref, out_ref = refs[2 * n_pat:2 * n_pat + 4]
    scratch = list(refs[2 * n_pat + 4:])
    tm = h_ref.shape[0]
    vals = []
    for idx, ref in enumerate(pat):
        dil = dilations[idx // 2]
        if dil == 1:
            vals.append(ref[...])
            continue
        buf = scratch.pop(0)
        n_chunks = buf.shape[0]
        for r in range(dil):
            for c in range(n_chunks):
                c0 = (r * n_chunks + c) * LANES
                buf[c, pl.ds(r, tm // dil, stride=dil), :] = ref[:, c0:c0 + LANES]
        vals.append(jnp.concatenate([buf[c] for c in range(n_chunks)], axis=1))
    os_, ls_ = vals[0::2], vals[1::2]
    m = functools.reduce(jnp.maximum, ls_)
    es = [jnp.exp(l - m) for l in ls_]
    oa = sum(e * o for e, o in zip(es, os_)) / sum(es)
    wa = oa.shape[1]
    acc = jnp.dot(oa.astype(BF16), w_ref[0:wa, :], preferred_element_type=F32)
    acc = acc + jnp.dot(ob_ref[...], w_ref[wa:, :], preferred_element_type=F32)
    out_ref[...] = h_ref[...] + acc


def _hyb_out(pattern_outs, dilations, ob, w, h, *, tm):
    s, d = h.shape
    wa = pattern_outs[0][0].shape[1] // dilations[0]
    in_specs, args, scratch = [], [], []
    for (o, lse), dil in zip(pattern_outs, dilations):
        for arr in (o, lse):
            in_specs.append(pl.BlockSpec((tm // dil, dil * wa), lambda i: (i, 0)))
            args.append(arr)
            if dil > 1:
                scratch.append(pltpu.VMEM((wa // LANES, tm, LANES), F32))
    return pl.pallas_call(
        functools.partial(_hyb_out_kernel, dilations=tuple(dilations)),
        grid=(s // tm,),
        in_specs=in_specs + [
            pl.BlockSpec((tm, ob.shape[1]), lambda i: (i, 0)),
            pl.BlockSpec(w.shape, lambda i: (0, 0)),
            pl.BlockSpec((tm, d), lambda i: (i, 0)),
        ],
        out_specs=pl.BlockSpec((tm, d), lambda i: (i, 0)),
        out_shape=jax.ShapeDtypeStruct((s, d), F32),
        scratch_shapes=scratch,
        compiler_params=_params(("parallel",)),
        name="hyb_out",
    )(*args, ob, w, h)


def _proj_res_kernel(a_ref, w_ref, h_ref, out_ref):
    out_ref[...] = h_ref[...] + jnp.dot(a_ref[...], w_ref[...], preferred_element_type=F32)


def _proj_res(a, w, h, *, tm):
    s, d = h.shape
    return pl.pallas_call(
        _proj_res_kernel,
        grid=(s // tm,),
        in_specs=[
            pl.BlockSpec((tm, a.shape[1]), lambda i: (i, 0)),
            pl.BlockSpec(w.shape, lambda i: (0, 0)),
            pl.BlockSpec((tm, d), lambda i: (i, 0)),
        ],
        out_specs=pl.BlockSpec((tm, d), lambda i: (i, 0)),
        out_shape=jax.ShapeDtypeStruct((s, d), F32),
        compiler_params=_params(("parallel",)),
        name="proj_res",
    )(a, w, h)


HALO = BF16_SUBLANES


FFN_SUB = 256


def _ffn_kernel(h_ref, halo_ref, g_ref, wg_ref, wu_ref, cw_ref, cb_ref, wd_ref, *rest,
                tm, final):
    if final:
        fg_ref, out_ref, n_scr, gate_scr, act_scr = rest
    else:
        out_ref, n_scr, gate_scr, act_scr = rest
    i = pl.program_id(0)
    j = pl.program_id(1)
    n_chunks, _, chunk = act_scr.shape

    @pl.when(j == 0)
    def _():
        g = g_ref[...]
        prev = jnp.where(i > 0, halo_ref[...], 0.0)
        n_scr[0:HALO] = _rms(prev, g, NORM_EPS).astype(BF16)
        n_scr[HALO:] = _rms(h_ref[...], g, NORM_EPS).astype(BF16)

    for t, c0 in enumerate(range(0, chunk, FFN_SUB)):
        w = min(FFN_SUB, chunk - c0)
        cols = slice(c0, c0 + w)
        gate_scr[t, :, 0:w] = jnp.dot(n_scr[...], wg_ref[:, cols], preferred_element_type=F32)
        up = jnp.dot(n_scr[HALO:], wu_ref[:, cols], preferred_element_type=F32)
        conv = cb_ref[:, cols]
        for k in range(CONV_WIDTH):
            start = HALO - (CONV_WIDTH - 1) + k
            conv = conv + gate_scr[t, start:start + tm, 0:w] * cw_ref[k:k + 1, cols]
        act_scr[j, :, cols] = (conv * (1.0 / (1.0 + jnp.exp(-conv))) * up).astype(BF16)

    @pl.when(j == n_chunks - 1)
    def _():
        act = jnp.concatenate([act_scr[t] for t in range(n_chunks)], axis=1)
        y = h_ref[...] + jnp.dot(act, wd_ref[...], preferred_element_type=F32)
        if final:
            y = _rms(y, fg_ref[...], NORM_EPS)
        out_ref[...] = y


def _ffn(h, g, w_up, conv_w, conv_b, w_down, final_g, *, tm, n_chunks):
    s, d = h.shape
    d_ff = w_down.shape[0]
    chunk = d_ff // n_chunks
    assert chunk * n_chunks == d_ff and chunk % LANES == 0
    n_sub = -(-chunk // FFN_SUB)
    final = final_g is not None
    once = pl.Buffered(1)
    up_mode = once if n_chunks == 1 else None
    in_specs = [
        pl.BlockSpec((tm, d), lambda i, j: (i, 0)),
        pl.BlockSpec((HALO, d), lambda i, j: (jnp.maximum(i * (tm // HALO) - 1, 0), 0)),
        pl.BlockSpec((1, d), lambda i, j: (0, 0)),
        pl.BlockSpec((d, chunk), lambda i, j: (0, j), pipeline_mode=up_mode),
        pl.BlockSpec((d, chunk), lambda i, j: (0, n_chunks + j), pipeline_mode=up_mode),
        pl.BlockSpec((CONV_WIDTH, chunk), lambda i, j: (0, j)),
        pl.BlockSpec((1, chunk), lambda i, j: (0, j)),
        pl.BlockSpec((d_ff, d), lambda i, j: (0, 0), pipeline_mode=once),
    ]
    args = [h, h, g, w_up, w_up, conv_w, conv_b, w_down]
    if final:
        in_specs.append(pl.BlockSpec((1, d), lambda i, j: (0, 0)))
        args.append(final_g)
    return pl.pallas_call(
        functools.partial(_ffn_kernel, tm=tm, final=final),
        grid=(s // tm, n_chunks),
        in_specs=in_specs,
        out_specs=pl.BlockSpec((tm, d), lambda i, j: (i, 0)),
        out_shape=jax.ShapeDtypeStruct((s, d), F32),
        scratch_shapes=[
            pltpu.VMEM((tm + HALO, d), BF16),
            pltpu.VMEM((n_sub, tm + HALO, FFN_SUB), F32),
            pltpu.VMEM((n_chunks, tm, chunk), BF16),
        ],
        compiler_params=_params(("parallel", "arbitrary")),
        name="ffn_final" if final else "ffn",
    )(*args)


def _rope_tables(s):
    inv = 1.0 / (ROPE_THETA ** (jnp.arange(0, HEAD_DIM, 2, dtype=F32) / HEAD_DIM))
    ang = jnp.arange(s, dtype=F32)[:, None] * inv[None, :]
    cos, sin = jnp.cos(ang), jnp.sin(ang)
    sign = jnp.where((jnp.arange(LANES) & 32) == 0, -1.0, 1.0).astype(F32)
    cos_l = jnp.tile(cos, (1, LANES // 32))
    sin_signed = jnp.tile(sin, (1, LANES // 32)) * sign[None, :]
    return cos_l, sin_signed, cos.T, sin.T


def kernel(x, attn_norm, ffn_norm, final_norm, hyb_w_in, hyb_b_f, hyb_w_out, diff_w_qkv,
           diff_lambda, diff_subln, diff_w_out, ffn_w_up, ffn_conv_w, ffn_conv_b, ffn_w_down):
    b, s, d = x.shape
    assert b == 1
    depth = attn_norm.shape[0]
    width = hyb_w_out.shape[1] // 2
    n_pairs = width // LANES
    n_heads_b = width // HEAD_DIM
    n_diff_heads = diff_w_out.shape[1] // LANES
    dq = n_diff_heads * LANES
    tm = min(1024, s)
    ts = min(512, s)
    tq_flash, tk_flash = min(1024, s), min(512, s)
    gps = 4
    q_scale = HEAD_DIM ** -0.5
    cos, sin_signed, cos_t, sin_t = _rope_tables(s)
    h = x[0]

    for l in range(depth):
        g_attn = attn_norm[l][None, :]
        if l % 2 == 0:
            e = l // 2
            w_in = hyb_w_in[e]
            qa_ka_va, qb, kb, vb, wf = (w_in[:, :3 * width], w_in[:, 3 * width:4 * width],
                                        w_in[:, 4 * width:5 * width], w_in[:, 5 * width:6 * width],
                                        w_in[:, 6 * width:])
            dils = tuple(dil for _, dil in DILATED_PATTERNS)
            proj, *views = _norm_proj(
                h, g_attn, jnp.concatenate([qa_ka_va, kb], axis=1).astype(BF16), cos, sin_signed,
                rope_tiles=(0, 1), scale_tiles=(0,), scale=q_scale, tm=tm, tn=PROJ_TILE,
                strided_tiles=3, dilations=dils[1:])
            proj_t = _norm_proj_t(h, g_attn, jnp.concatenate([qb, vb], axis=1).T.astype(BF16),
                                  cos_t, sin_t, rope_tiles=(), scale_tiles=(0,),
                                  scale=q_scale * LOG2E, tm=tm, tn=PROJ_TILE)
            wf_pad = jnp.pad(wf, ((0, 0), (0, LANES - n_heads_b))).astype(BF16)
            bf_pad = jnp.pad(hyb_b_f[e], (0, LANES - n_heads_b))[None, :]
            cfeat = _fox_gate(h, g_attn, wf_pad, bf_pad, tc=ts, n_heads=n_heads_b)
            pats = [_dilated_pattern(proj, 1, width=width, per_res=4, tq=min(1024, s))]
            pats += [_dilated_pattern(v, dil, width=width, per_res=3, tq=min(1024, s // dil))
                     for v, dil in zip(views, dils[1:])]
            ob = _flash_pair(proj_t, proj, (cfeat,), mode="fox", n_groups=n_pairs, gps=gps,
                             q_row=0, v_row=n_pairs // gps, k_col=3 * n_pairs // gps,
                             tq=tq_flash, tk=tk_flash)
            h = _hyb_out(pats, dils, ob, hyb_w_out[e].astype(BF16), h, tm=ts)
        else:
            o = l // 2
            w = diff_w_qkv[o]
            wq, wk, wv = w[:, :dq], w[:, dq:2 * dq], w[:, 2 * dq:]
            k_tiles = dq // PROJ_TILE
            keys = _norm_proj(h, g_attn, wk.astype(BF16), cos, sin_signed,
                              rope_tiles=tuple(range(k_tiles)), scale_tiles=(),
                              scale=1.0, tm=tm, tn=PROJ_TILE)
            proj_t = _norm_proj_t(h, g_attn, jnp.concatenate([wq, wv], axis=1).T.astype(BF16),
                                  cos_t, sin_t, rope_tiles=tuple(range(k_tiles)),
                                  scale_tiles=tuple(range(k_tiles)), scale=q_scale * LOG2E,
                                  tm=tm, tn=PROJ_TILE)
            lam_init = 0.8 - 0.6 * math.exp(-0.3 * l)
            att = _flash_pair(proj_t, keys, (diff_lambda[o], diff_subln[o][:, None]),
                              mode="diff", n_groups=n_diff_heads, gps=n_diff_heads, q_row=0,
                              v_row=1, k_col=0, tq=tq_flash, tk=tk_flash,
                              lam_init=lam_init)
            h = _proj_res(att, diff_w_out[o].astype(BF16), h, tm=tm)
        h = _ffn(h, ffn_norm[l][None, :], ffn_w_up[l].astype(BF16), ffn_conv_w[l],
                 ffn_conv_b[l][None, :], ffn_w_down[l].astype(BF16),
                 final_norm[None, :] if l == depth - 1 else None, tm=ts, n_chunks=1)
    return h[None]
```

```python
import functools
import math

import jax
import jax.numpy as jnp
from jax import lax
from jax.experimental import pallas as pl
from jax.experimental.pallas import tpu as pltpu

F32 = jnp.float32
BF16 = jnp.bfloat16

HEAD_DIM = 64
LANES = 128
BF16_SUBLANES = 16
ROPE_THETA = 10000.0
DILATED_PATTERNS = ((128, 1), (512, 4), (2048, 16))
BAND = 128
NORM_EPS = 1e-6
SUBLN_EPS = 1e-5
CONV_WIDTH = 3
NEG = -1e30
VMEM_LIMIT = 48 * 1024 * 1024
PROJ_TILE = 512

NT_DIMS = (((1,), (1,)), ((), ()))


def _params(sem):
    return pltpu.CompilerParams(dimension_semantics=sem, vmem_limit_bytes=VMEM_LIMIT)


def _rms(x, g, eps):
    return x * lax.rsqrt(jnp.mean(x * x, axis=-1, keepdims=True) + eps) * g


def _lane_lo(rows):
    return lax.broadcasted_iota(jnp.int32, (rows, LANES), 1) < HEAD_DIM


def _rope_tile(x, cos, sin_signed):
    rows = x.shape[0]
    first_half = (lax.broadcasted_iota(jnp.int32, (rows, LANES), 1) & 32) == 0
    out = []
    for c in range(x.shape[1] // LANES):
        xc = x[:, c * LANES:(c + 1) * LANES]
        ahead = pltpu.roll(xc, LANES - 32, 1)
        behind = pltpu.roll(xc, 32, 1)
        rot = jnp.where(first_half, ahead, behind)
        out.append(xc * cos + rot * sin_signed)
    return jnp.concatenate(out, axis=1)


def _norm_proj_kernel(h_ref, g_ref, w_ref, cos_ref, sin_ref, o_ref, *rest,
                      rope_tiles, scale_tiles, scale, tn, strided_tiles, dilations):
    n = _rms(h_ref[...], g_ref[...], NORM_EPS).astype(BF16)
    tm = h_ref.shape[0]
    for j in range(w_ref.shape[1] // tn):
        cols = slice(j * tn, (j + 1) * tn)
        acc = jnp.dot(n, w_ref[:, cols], preferred_element_type=F32)
        if j in rope_tiles:
            acc = _rope_tile(acc, cos_ref[...], sin_ref[...])
        if j in scale_tiles:
            acc = acc * scale
        o_ref[:, cols] = acc.astype(BF16)
        if j < strided_tiles:
            stage = rest[-1]
            for c in range(tn // LANES):
                stage[c] = acc[:, c * LANES:(c + 1) * LANES]
            for d_ref, dil in zip(rest[:-1], dilations):
                for r in range(dil):
                    for c in range(tn // LANES):
                        c0 = (r * strided_tiles + j) * tn + c * LANES
                        rows = stage[c, pl.ds(r, tm // dil, stride=dil), :]
                        d_ref[:, c0:c0 + LANES] = rows.astype(BF16)


def _norm_proj(h, g, w, cos, sin_signed, *, rope_tiles, scale_tiles, scale, tm, tn,
               strided_tiles=0, dilations=()):
    s, d = h.shape
    n = w.shape[1]
    kern = functools.partial(_norm_proj_kernel, rope_tiles=rope_tiles, scale_tiles=scale_tiles,
                             scale=scale, tn=tn, strided_tiles=strided_tiles,
                             dilations=dilations)
    out_specs = [pl.BlockSpec((tm, n), lambda i: (i, 0))]
    out_shape = [jax.ShapeDtypeStruct((s, n), BF16)]
    for dil in dilations:
        wide = dil * strided_tiles * tn
        out_specs.append(pl.BlockSpec((tm // dil, wide), lambda i: (i, 0)))
        out_shape.append(jax.ShapeDtypeStruct((s // dil, wide), BF16))
    outs = pl.pallas_call(
        kern,
        grid=(s // tm,),
        in_specs=[
            pl.BlockSpec((tm, d), lambda i: (i, 0)),
            pl.BlockSpec((1, d), lambda i: (0, 0)),
            pl.BlockSpec((d, n), lambda i: (0, 0)),
            pl.BlockSpec((tm, LANES), lambda i: (i, 0)),
            pl.BlockSpec((tm, LANES), lambda i: (i, 0)),
        ],
        out_specs=out_specs,
        out_shape=out_shape,
        scratch_shapes=[pltpu.VMEM((tn // LANES, tm, LANES), F32)] if dilations else [],
        compiler_params=_params(("parallel",)),
        name="norm_proj",
    )(h, g, w, cos, sin_signed)
    return outs if dilations else outs[0]


def _norm_proj_t_kernel(h_ref, g_ref, wt_ref, cos_ref, sin_ref, o_ref, *,
                        rope_tiles, scale_tiles, scale, tn):
    n = _rms(h_ref[...], g_ref[...], NORM_EPS).astype(BF16)
    half = HEAD_DIM // 2
    for j in range(wt_ref.shape[0] // tn):
        acc = lax.dot_general(wt_ref[j * tn:(j + 1) * tn, :], n, NT_DIMS,
                              preferred_element_type=F32)
        sc = scale if j in scale_tiles else 1.0
        if j in rope_tiles:
            cos, sin = cos_ref[...], sin_ref[...]
            for hd in range(tn // HEAD_DIM):
                r0 = j * tn + hd * HEAD_DIM
                x1 = acc[hd * HEAD_DIM:hd * HEAD_DIM + half]
                x2 = acc[hd * HEAD_DIM + half:(hd + 1) * HEAD_DIM]
                o_ref[r0:r0 + half] = ((x1 * cos - x2 * sin) * sc).astype(BF16)
                o_ref[r0 + half:r0 + HEAD_DIM] = ((x2 * cos + x1 * sin) * sc).astype(BF16)
        else:
            o_ref[j * tn:(j + 1) * tn] = (acc * sc).astype(BF16)


def _norm_proj_t(h, g, wt, cos_t, sin_t, *, rope_tiles, scale_tiles, scale, tm, tn):
    s, d = h.shape
    n = wt.shape[0]
    kern = functools.partial(_norm_proj_t_kernel, rope_tiles=rope_tiles,
                             scale_tiles=scale_tiles, scale=scale, tn=tn)
    return pl.pallas_call(
        kern,
        grid=(s // tm,),
        in_specs=[
            pl.BlockSpec((tm, d), lambda i: (i, 0)),
            pl.BlockSpec((1, d), lambda i: (0, 0)),
            pl.BlockSpec((n, d), lambda i: (0, 0)),
            pl.BlockSpec((HEAD_DIM // 2, tm), lambda i: (0, i)),
            pl.BlockSpec((HEAD_DIM // 2, tm), lambda i: (0, i)),
        ],
        out_specs=pl.BlockSpec((n, tm), lambda i: (0, i)),
        out_shape=jax.ShapeDtypeStruct((n, s), BF16),
        compiler_params=_params(("parallel",)),
        name="norm_proj_t",
    )(h, g, wt, cos_t, sin_t)


GATE_TERMS = 3
GATE_STRIDE = 8
LOG2E = math.log2(math.e)


def _fox_gate_kernel(h_ref, g_ref, wf_ref, bf_ref, cf_ref, carry, *, tc, n_heads):
    i = pl.program_id(0)

    @pl.when(i == 0)
    def _():
        carry[...] = jnp.zeros_like(carry)

    n = _rms(h_ref[...], g_ref[...], NORM_EPS).astype(BF16)
    z = jnp.dot(n, wf_ref[...], preferred_element_type=F32) + bf_ref[...]
    logf = jnp.minimum(z, 0.0) - jnp.log(1.0 + jnp.exp(-jnp.abs(z)))
    dst = lax.broadcasted_iota(jnp.int32, (tc, tc), 0)
    src = lax.broadcasted_iota(jnp.int32, (tc, tc), 1)
    prefix = jnp.where(src <= dst, 1.0, 0.0).astype(BF16)
    terms, rem = [], logf
    for _ in range(GATE_TERMS):
        part = rem.astype(BF16)
        terms.append(part)
        rem = rem - part.astype(F32)
    sums = jnp.dot(prefix, jnp.concatenate(terms, axis=1), preferred_element_type=F32)
    cs = sum(sums[:, t * LANES:(t + 1) * LANES] for t in range(GATE_TERMS)) + carry[0:1, :]
    carry[...] = jnp.broadcast_to(cs[tc - 1:tc, :], carry.shape)
    lane = lax.broadcasted_iota(jnp.int32, (tc, LANES), 1)
    rem = jnp.where(lane < n_heads, cs * LOG2E, 0.0)
    feat = jnp.zeros_like(rem)
    for term in range(GATE_TERMS):
        part = rem.astype(BF16).astype(F32)
        rem = rem - part
        feat = feat + (pltpu.roll(part, term * GATE_STRIDE, 1) if term else part)
    cf_ref[...] = feat.astype(BF16)


def _fox_gate(h, g, wf, b_f, *, tc, n_heads):
    s, d = h.shape
    return pl.pallas_call(
        functools.partial(_fox_gate_kernel, tc=tc, n_heads=n_heads),
        grid=(s // tc,),
        in_specs=[
            pl.BlockSpec((tc, d), lambda i: (i, 0)),
            pl.BlockSpec((1, d), lambda i: (0, 0)),
            pl.BlockSpec((d, LANES), lambda i: (0, 0)),
            pl.BlockSpec((1, LANES), lambda i: (0, 0)),
        ],
        out_specs=pl.BlockSpec((tc, LANES), lambda i: (i, 0)),
        out_shape=jax.ShapeDtypeStruct((s, LANES), BF16),
        scratch_shapes=[pltpu.VMEM((8, LANES), F32)],
        compiler_params=_params(("arbitrary",)),
        name="fox_gate",
    )(h, g, wf, b_f)


def _dilated_kernel(q_ref, kc_ref, kh_ref, vc_ref, vh_ref, o_ref, l_ref, kbuf, vbuf, *, tq):
    i = pl.program_id(1)
    kbuf[0:BAND] = kh_ref[...]
    kbuf[BAND:] = kc_ref[...]
    vbuf[0:BAND] = vh_ref[...]
    vbuf[BAND:] = vc_ref[...]
    row = lax.broadcasted_iota(jnp.int32, (BAND, 2 * BAND), 0)
    col = lax.broadcasted_iota(jnp.int32, (BAND, 2 * BAND), 1)
    delta = row - col + BAND
    band = (delta >= 0) & (delta <= BAND)
    lo = _lane_lo(BAND)
    for a in range(tq // BAND):
        first_key = i * tq + (a - 1) * BAND
        valid = band & (col + first_key >= 0)
        rows = slice(a * BAND, (a + 1) * BAND)
        for hp in range(q_ref.shape[1] // LANES):
            lanes = slice(hp * LANES, (hp + 1) * LANES)
            q = q_ref[rows, lanes]
            kk = kbuf[a * BAND:(a + 2) * BAND, lanes]
            vv = vbuf[a * BAND:(a + 2) * BAND, lanes]
            outs, lses = [], []
            for qh in (jnp.where(lo, q, jnp.zeros_like(q)), jnp.where(lo, jnp.zeros_like(q), q)):
                s = lax.dot_general(qh, kk, NT_DIMS, preferred_element_type=F32)
                s = jnp.where(valid, s, NEG)
                m = jnp.max(s, axis=1, keepdims=True)
                p = jnp.exp(s - m)
                den = jnp.sum(p, axis=1, keepdims=True)
                pv = jnp.dot(p.astype(BF16), vv, preferred_element_type=F32)
                outs.append(pv / den)
                lses.append(jnp.broadcast_to(m + jnp.log(den), (BAND, LANES)))
            o_ref[rows, lanes] = jnp.where(lo, outs[0], outs[1])
            l_ref[rows, lanes] = jnp.where(lo, lses[0], lses[1])


def _dilated_pattern(view, dil, *, width, per_res, tq):
    L = view.shape[0]
    halo_per_tile = tq // BAND
    cur = lambda part: pl.BlockSpec((tq, width), lambda r, i: (i, r * per_res + part))
    halo = lambda part: pl.BlockSpec(
        (BAND, width), lambda r, i: (jnp.maximum(i * halo_per_tile - 1, 0), r * per_res + part))
    out_spec = pl.BlockSpec((tq, width), lambda r, i: (i, r))
    return pl.pallas_call(
        functools.partial(_dilated_kernel, tq=tq),
        grid=(dil, L // tq),
        in_specs=[cur(0), cur(1), halo(1), cur(2), halo(2)],
        out_specs=[out_spec, out_spec],
        out_shape=[jax.ShapeDtypeStruct((L, dil * width), F32)] * 2,
        scratch_shapes=[pltpu.VMEM((tq + BAND, width), BF16)] * 2,
        compiler_params=_params(("parallel", "arbitrary")),
        name=f"dilated_d{dil}",
    )(view, view, view, view, view)


ONES_ROWS = BF16_SUBLANES


COL_TILE = 256
STRIPS_PER_TRIP = 4


def _flash_pair_kernel(qi_ref, kj_ref, qt_ref, k_ref, *rest, mode, tq, tk, gps, lam_init):
    if mode == "fox":
        cf_ref, vt_ref, o_ref, wq, m_scr, acc, kx_scr, vx_scr, *bufs = rest
    else:
        vt_ref, lam_ref, g_ref, o_ref, wq, m_scr, acc, kx_scr, vx_scr, *bufs = rest
    s_buf, mx_buf, p_buf, al_buf = bufs[0:2], bufs[2:4], bufs[4:6], bufs[6:8]
    U = STRIPS_PER_TRIP
    step = pl.program_id(1)
    qi = qi_ref[step]
    kj = kj_ref[step]
    last_kj = (qi * tq + tq - 1) // tk
    rows = acc.shape[1]
    dv = rows - ONES_ROWS
    n_ct = tq // COL_TILE
    n_strips = 2 * gps * n_ct
    n_trips = n_strips // U
    par = kj % 2

    @pl.when(kj == 0)
    def _():
        r = lax.broadcasted_iota(jnp.int32, (LANES, COL_TILE), 0)
        first = r < HEAD_DIM
        for gi in range(gps):
            for a in range(2):
                keep_q = first if a == 0 else jnp.logical_not(first)
                if mode == "fox":
                    head = 2 * (pl.program_id(0) * gps + gi) + a
                    pick = (r - head) == 0
                    for term in range(1, GATE_TERMS):
                        pick = pick | ((r - head) == term * GATE_STRIDE)
                    gate_rows = jnp.where(pick, -1.0, 0.0).astype(BF16)
                for ct in range(n_ct):
                    t = (2 * gi + a) * n_ct + ct
                    qt = qt_ref[gi * LANES:(gi + 1) * LANES, ct * COL_TILE:(ct + 1) * COL_TILE]
                    wq[t, 0:LANES] = jnp.where(keep_q, qt, jnp.zeros_like(qt))
                    if mode == "fox":
                        wq[t, LANES:2 * LANES] = gate_rows
        m_scr[...] = jnp.full_like(m_scr, NEG)
        acc[...] = jnp.zeros_like(acc)

    maps_per_v = 1 if mode == "fox" else 2
    ones = jnp.ones((ONES_ROWS, tk), BF16)
    for gi in range(gps):
        kx_scr[gi, :, 0:LANES] = k_ref[:, gi * LANES:(gi + 1) * LANES]
        if mode == "fox":
            kx_scr[gi, :, LANES:2 * LANES] = cf_ref[...]
            for a in range(2):
                v0 = gi * LANES + a * HEAD_DIM
                vx_scr[par, 2 * gi + a, 0:dv] = vt_ref[v0:v0 + HEAD_DIM, :]
                vx_scr[par, 2 * gi + a, dv:rows] = ones
        else:
            vx_scr[par, gi, 0:dv] = vt_ref[gi * LANES:(gi + 1) * LANES, :]
            vx_scr[par, gi, dv:rows] = ones

    def stage_a(j, slot, u, masked):
        s = jnp.dot(kx_scr[j // (2 * n_ct)], wq[j], preferred_element_type=F32)
        if masked:
            key = kj * tk + lax.broadcasted_iota(jnp.int32, (tk, COL_TILE), 0)
            qry = (qi * tq + (j % n_ct) * COL_TILE
                   + lax.broadcasted_iota(jnp.int32, (tk, COL_TILE), 1))
            s = jnp.where(key <= qry, s, NEG)
        s_buf[slot][u] = s
        mx_buf[slot][u] = jnp.broadcast_to(jnp.max(s, axis=0, keepdims=True), (8, COL_TILE))


    def stage_b(j0, slot, us):
        m_prevs = {u: m_scr[j0 + u] for u in us}
        for u, m_prev in list(m_prevs.items()):
            m_next = jnp.maximum(m_prev, mx_buf[slot][u])
            al_buf[slot][u] = jnp.exp2(m_prev - m_next)
            s3 = s_buf[slot][u].reshape(tk // 8, 8, COL_TILE)
            p_buf[slot][u] = jnp.exp2(s3 - m_next[None]).reshape(tk, COL_TILE).astype(BF16)
            m_prevs[u] = m_next
        for u, m_next in m_prevs.items():
            m_scr[j0 + u] = m_next

    def stage_c(j0, slot, vpar, us):
        pvs = [jnp.dot(vx_scr[vpar, (j0 + u) // (n_ct * maps_per_v)], p_buf[slot][u],
                       preferred_element_type=F32) for u in us]
        olds = [acc[j0 + u] for u in us]
        for u, old, pv in zip(us, olds, pvs):
            acc[j0 + u] = ((old.reshape(rows // 8, 8, COL_TILE) * al_buf[slot][u][None])
                           .reshape(rows, COL_TILE) + pv)

    def sweep(masked, first_step):
        def trip(k, slot, run_b=True, run_c=True):
            kc = (k + n_trips - 2) % n_trips
            kb = (k + n_trips - 1) % n_trips
            for u in range(U):
                sees_all = tq == 2 * tk and (u % n_ct) * COL_TILE >= tk - 1
                stage_a(k * U + u, slot, u, masked and not sees_all)
            if run_b:
                stage_b(kb * U, 1 - slot, range(U))
            if run_c:
                stage_c(kc * U, slot, jnp.where(k >= 2, par, 1 - par), range(U))

        def trip_pair(half, carry):
            trip(2 * half, 0)
            trip(2 * half + 1, 1)
            return carry

        if first_step:
            trip(0, 0, run_b=False, run_c=False)
            trip(1, 1, run_c=False)
            lax.fori_loop(1, n_trips // 2, trip_pair, 0)
        else:
            lax.fori_loop(0, n_trips // 2, trip_pair, 0)

    assert U % n_ct == 0
    live = tuple(u for u in range(U) if (u % n_ct) * COL_TILE >= tq - tk)

    def last_sweep():
        def trip(k, slot, b_from_prev_step, c_from_prev_step):
            kc = (k + n_trips - 2) % n_trips
            kb = (k + n_trips - 1) % n_trips
            for u in live:
                stage_a(k * U + u, slot, u, True)
            stage_b(kb * U, 1 - slot, range(U) if b_from_prev_step else live)
            stage_c(kc * U, slot, 1 - par if c_from_prev_step else par,
                    range(U) if c_from_prev_step else live)

        def trip_pair(half, carry):
            trip(2 * half, 0, False, False)
            trip(2 * half + 1, 1, False, False)
            return carry

        trip(0, 0, True, True)
        trip(1, 1, False, True)
        lax.fori_loop(1, n_trips // 2, trip_pair, 0)

    needs_mask = kj * tk + tk - 1 > qi * tq
    is_last = kj == last_kj
    is_first = kj == 0
    for masked in (True, False):
        on_mask = jnp.logical_and(needs_mask, jnp.logical_not(is_last)) if masked \
            else jnp.logical_not(needs_mask)
        for first_step in (True, False):
            on_step = is_first if first_step else jnp.logical_not(is_first)
            pl.when(jnp.logical_and(on_mask, on_step))(
                functools.partial(sweep, masked, first_step))
    pl.when(is_last)(last_sweep)

    @pl.when(is_last)
    def _():
        last = n_trips - 1
        stage_c((last - 1) * U, (last - 1) % 2, par, live)
        stage_b(last * U, last % 2, live)
        stage_c(last * U, last % 2, par, live)

        def map_out(mi):
            parts = [acc[mi * n_ct + ct] for ct in range(n_ct)]
            full = jnp.concatenate(parts, axis=1)
            return full[0:dv] / full[dv:dv + 1]

        for gi in range(gps):
            o0, o1 = map_out(2 * gi), map_out(2 * gi + 1)
            if mode == "fox":
                o = jnp.concatenate([o0, o1], axis=0)
            else:
                lp = lam_ref[...]
                t1 = jnp.sum(lp[0:1] * lp[1:2], axis=1, keepdims=True)
                t2 = jnp.sum(lp[2:3] * lp[3:4], axis=1, keepdims=True)
                lam = jnp.exp(t1) - jnp.exp(t2) + lam_init
                o = o0 - lam * o1
                ms = jnp.mean(o * o, axis=0, keepdims=True)
                o = o * lax.rsqrt(ms + SUBLN_EPS) * g_ref[...] * (1.0 - lam_init)
            o_ref[:, gi * LANES:(gi + 1) * LANES] = o.T.astype(BF16)


def _causal_steps(s, tq, tk):
    qi, kj = [], []
    for i in range(s // tq):
        for j in range((i * tq + tq - 1) // tk + 1):
            qi.append(i)
            kj.append(j)
    return jnp.asarray(qi, jnp.int32), jnp.asarray(kj, jnp.int32)


def _flash_pair(proj_t, keys, extras, *, mode, n_groups, gps, q_row, v_row, k_col, tq, tk,
                lam_init=0.0):
    s = keys.shape[0]
    qi, kj = _causal_steps(s, tq, tk)
    gw = gps * LANES
    qt_spec = pl.BlockSpec((gw, tq), lambda g, t, qi, kj: (q_row + g, qi[t]))
    k_spec = pl.BlockSpec((tk, gw), lambda g, t, qi, kj: (kj[t], k_col + g))
    vt_spec = pl.BlockSpec((gw, tk), lambda g, t, qi, kj: (v_row + g, kj[t]))
    const = lambda x: pl.BlockSpec(x.shape, lambda g, t, qi, kj: (0, 0))
    if mode == "fox":
        (cfeat,) = extras
        in_specs = [qt_spec, k_spec,
                    pl.BlockSpec((tk, LANES), lambda g, t, qi, kj: (kj[t], 0)), vt_spec]
        args = (proj_t, keys, cfeat, proj_t)
        kd, dv, n_values = 2 * LANES, HEAD_DIM, 2 * gps
    else:
        lam_params, subln_g = extras
        in_specs = [qt_spec, k_spec, vt_spec, const(lam_params), const(subln_g)]
        args = (proj_t, keys, proj_t, lam_params, subln_g)
        kd, dv, n_values = LANES, LANES, gps
    rows = dv + ONES_ROWS
    n_strips = 2 * gps * (tq // COL_TILE)
    u = STRIPS_PER_TRIP
    assert n_strips % (2 * u) == 0
    assert tq >= 2 * tk
    kern = functools.partial(_flash_pair_kernel, mode=mode, tq=tq, tk=tk, gps=gps,
                             lam_init=lam_init)
    return pl.pallas_call(
        kern,
        grid_spec=pltpu.PrefetchScalarGridSpec(
            num_scalar_prefetch=2,
            grid=(n_groups // gps, qi.shape[0]),
            in_specs=in_specs,
            out_specs=pl.BlockSpec((tq, gw), lambda g, t, qi, kj: (qi[t], g)),
            scratch_shapes=[
                pltpu.VMEM((n_strips, kd, COL_TILE), BF16),
                pltpu.VMEM((n_strips, 8, COL_TILE), F32),
                pltpu.VMEM((n_strips, rows, COL_TILE), F32),
                pltpu.VMEM((gps, tk, kd), BF16),
                pltpu.VMEM((2, n_values, rows, tk), BF16),
            ] + [pltpu.VMEM((u, tk, COL_TILE), F32)] * 2
              + [pltpu.VMEM((u, 8, COL_TILE), F32)] * 2
              + [pltpu.VMEM((u, tk, COL_TILE), BF16)] * 2
              + [pltpu.VMEM((u, 8, COL_TILE), F32)] * 2,
        ),
        out_shape=jax.ShapeDtypeStruct((s, n_groups * LANES), BF16),
        compiler_params=_params(("arbitrary", "arbitrary")),
        name=f"flash_{mode}",
    )(qi, kj, *args)


def _hyb_out_kernel(*refs, dilations):
    n_pat = len(dilations)
    pat = refs[:2 * n_pat]
    ob_ref, w_ref, h_ref, out_ref = refs[2 * n_pat:2 * n_pat + 4]
    scratch = list(refs[2 * n_pat + 4:])
    tm = h_ref.shape[0]
    vals = []
    for idx, ref in enumerate(pat):
        dil = dilations[idx // 2]
        if dil == 1:
            vals.append(ref[...])
            continue
        buf = scratch.pop(0)
        n_chunks = buf.shape[0]
        for r in range(dil):
            for c in range(n_chunks):
                c0 = (r * n_chunks + c) * LANES
                buf[c, pl.ds(r, tm // dil, stride=dil), :] = ref[:, c0:c0 + LANES]
        vals.append(jnp.concatenate([buf[c] for c in range(n_chunks)], axis=1))
    os_, ls_ = vals[0::2], vals[1::2]
    m = functools.reduce(jnp.maximum, ls_)
    es = [jnp.exp(l - m) for l in ls_]
    oa = sum(e * o for e, o in zip(es, os_)) / sum(es)
    wa = oa.shape[1]
    acc = jnp.dot(oa.astype(BF16), w_ref[0:wa, :], preferred_element_type=F32)
    acc = acc + jnp.dot(ob_ref[...], w_ref[wa:, :], preferred_element_type=F32)
    out_ref[...] = h_ref[...] + acc


def _hyb_out(pattern_outs, dilations, ob, w, h, *, tm):
    s, d = h.shape
    wa = pattern_outs[0][0].shape[1] // dilations[0]
    in_specs, args, scratch = [], [], []
    for (o, lse), dil in zip(pattern_outs, dilations):
        for arr in (o, lse):
            in_specs.append(pl.BlockSpec((tm // dil, dil * wa), lambda i: (i, 0)))
            args.append(arr)
            if dil > 1:
                scratch.append(pltpu.VMEM((wa // LANES, tm, LANES), F32))
    return pl.pallas_call(
        functools.partial(_hyb_out_kernel, dilations=tuple(dilations)),
        grid=(s // tm,),
        in_specs=in_specs + [
            pl.BlockSpec((tm, ob.shape[1]), lambda i: (i, 0)),
            pl.BlockSpec(w.shape, lambda i: (0, 0)),
            pl.BlockSpec((tm, d), lambda i: (i, 0)),
        ],
        out_specs=pl.BlockSpec((tm, d), lambda i: (i, 0)),
        out_shape=jax.ShapeDtypeStruct((s, d), F32),
        scratch_shapes=scratch,
        compiler_params=_params(("parallel",)),
        name="hyb_out",
    )(*args, ob, w, h)


def _proj_res_kernel(a_ref, w_ref, h_ref, out_ref):
    out_ref[...] = h_ref[...] + jnp.dot(a_ref[...], w_ref[...], preferred_element_type=F32)


def _proj_res(a, w, h, *, tm):
    s, d = h.shape
    return pl.pallas_call(
        _proj_res_kernel,
        grid=(s // tm,),
        in_specs=[
            pl.BlockSpec((tm, a.shape[1]), lambda i: (i, 0)),
            pl.BlockSpec(w.shape, lambda i: (0, 0)),
            pl.BlockSpec((tm, d), lambda i: (i, 0)),
        ],
        out_specs=pl.BlockSpec((tm, d), lambda i: (i, 0)),
        out_shape=jax.ShapeDtypeStruct((s, d), F32),
        compiler_params=_params(("parallel",)),
        name="proj_res",
    )(a, w, h)


HALO = BF16_SUBLANES


FFN_SUB = 256


def _ffn_kernel(h_ref, halo_ref, g_ref, wg_ref, wu_ref, cw_ref, cb_ref, wd_ref, *rest,
                tm, final):
    if final:
        fg_ref, out_ref, n_scr, gate_scr, act_scr = rest
    else:
        out_ref, n_scr, gate_scr, act_scr = rest
    i = pl.program_id(0)
    j = pl.program_id(1)
    n_chunks, _, chunk = act_scr.shape

    @pl.when(j == 0)
    def _():
        g = g_ref[...]
        prev = jnp.where(i > 0, halo_ref[...], 0.0)
        n_scr[0:HALO] = _rms(prev, g, NORM_EPS).astype(BF16)
        n_scr[HALO:] = _rms(h_ref[...], g, NORM_EPS).astype(BF16)

    for t, c0 in enumerate(range(0, chunk, FFN_SUB)):
        w = min(FFN_SUB, chunk - c0)
        cols = slice(c0, c0 + w)
        gate_scr[t, :, 0:w] = jnp.dot(n_scr[...], wg_ref[:, cols], preferred_element_type=F32)
        up = jnp.dot(n_scr[HALO:], wu_ref[:, cols], preferred_element_type=F32)
        conv = cb_ref[:, cols]
        for k in range(CONV_WIDTH):
            start = HALO - (CONV_WIDTH - 1) + k
            conv = conv + gate_scr[t, start:start + tm, 0:w] * cw_ref[k:k + 1, cols]
        act_scr[j, :, cols] = (conv * (1.0 / (1.0 + jnp.exp(-conv))) * up).astype(BF16)

    @pl.when(j == n_chunks - 1)
    def _():
        act = jnp.concatenate([act_scr[t] for t in range(n_chunks)], axis=1)
        y = h_ref[...] + jnp.dot(act, wd_ref[...], preferred_element_type=F32)
        if final:
            y = _rms(y, fg_ref[...], NORM_EPS)
        out_ref[...] = y


def _ffn(h, g, w_up, conv_w, conv_b, w_down, final_g, *, tm, n_chunks):
    s, d = h.shape
    d_ff = w_down.shape[0]
    chunk = d_ff // n_chunks
    assert chunk * n_chunks == d_ff and chunk % LANES == 0
    n_sub = -(-chunk // FFN_SUB)
    final = final_g is not None
    once = pl.Buffered(1)
    up_mode = once if n_chunks == 1 else None
    in_specs = [
        pl.BlockSpec((tm, d), lambda i, j: (i, 0)),
        pl.BlockSpec((HALO, d), lambda i, j: (jnp.maximum(i * (tm // HALO) - 1, 0), 0)),
        pl.BlockSpec((1, d), lambda i, j: (0, 0)),
        pl.BlockSpec((d, chunk), lambda i, j: (0, j), pipeline_mode=up_mode),
        pl.BlockSpec((d, chunk), lambda i, j: (0, n_chunks + j), pipeline_mode=up_mode),
        pl.BlockSpec((CONV_WIDTH, chunk), lambda i, j: (0, j)),
        pl.BlockSpec((1, chunk), lambda i, j: (0, j)),
        pl.BlockSpec((d_ff, d), lambda i, j: (0, 0), pipeline_mode=once),
    ]
    args = [h, h, g, w_up, w_up, conv_w, conv_b, w_down]
    if final:
        in_specs.append(pl.BlockSpec((1, d), lambda i, j: (0, 0)))
        args.append(final_g)
    return pl.pallas_call(
        functools.partial(_ffn_kernel, tm=tm, final=final),
        grid=(s // tm, n_chunks),
        in_specs=in_specs,
        out_specs=pl.BlockSpec((tm, d), lambda i, j: (i, 0)),
        out_shape=jax.ShapeDtypeStruct((s, d), F32),
        scratch_shapes=[
            pltpu.VMEM((tm + HALO, d), BF16),
            pltpu.VMEM((n_sub, tm + HALO, FFN_SUB), F32),
            pltpu.VMEM((n_chunks, tm, chunk), BF16),
        ],
        compiler_params=_params(("parallel", "arbitrary")),
        name="ffn_final" if final else "ffn",
    )(*args)


def _rope_tables(s):
    inv = 1.0 / (ROPE_THETA ** (jnp.arange(0, HEAD_DIM, 2, dtype=F32) / HEAD_DIM))
    ang = jnp.arange(s, dtype=F32)[:, None] * inv[None, :]
    cos, sin = jnp.cos(ang), jnp.sin(ang)
    sign = jnp.where((jnp.arange(LANES) & 32) == 0, -1.0, 1.0).astype(F32)
    cos_l = jnp.tile(cos, (1, LANES // 32))
    sin_signed = jnp.tile(sin, (1, LANES // 32)) * sign[None, :]
    return cos_l, sin_signed, cos.T, sin.T


def kernel(x, attn_norm, ffn_norm, final_norm, hyb_w_in, hyb_b_f, hyb_w_out, diff_w_qkv,
           diff_lambda, diff_subln, diff_w_out, ffn_w_up, ffn_conv_w, ffn_conv_b, ffn_w_down):
    b, s, d = x.shape
    assert b == 1
    depth = attn_norm.shape[0]
    width = hyb_w_out.shape[1] // 2
    n_pairs = width // LANES
    n_heads_b = width // HEAD_DIM
    n_diff_heads = diff_w_out.shape[1] // LANES
    dq = n_diff_heads * LANES
    tm = min(1024, s)
    ts = min(512, s)
    tq_flash, tk_flash = min(1024, s), min(512, s)
    gps = 4
    q_scale = HEAD_DIM ** -0.5
    cos, sin_signed, cos_t, sin_t = _rope_tables(s)
    h = x[0]

    for l in range(depth):
        g_attn = attn_norm[l][None, :]
        if l % 2 == 0:
            e = l // 2
            w_in = hyb_w_in[e]
            qa_ka_va, qb, kb, vb, wf = (w_in[:, :3 * width], w_in[:, 3 * width:4 * width],
                                        w_in[:, 4 * width:5 * width], w_in[:, 5 * width:6 * width],
                                        w_in[:, 6 * width:])
            dils = tuple(dil for _, dil in DILATED_PATTERNS)
            proj, *views = _norm_proj(
                h, g_attn, jnp.concatenate([qa_ka_va, kb], axis=1).astype(BF16), cos, sin_signed,
                rope_tiles=(0, 1), scale_tiles=(0,), scale=q_scale, tm=tm, tn=PROJ_TILE,
                strided_tiles=3, dilations=dils[1:])
            proj_t = _norm_proj_t(h, g_attn, jnp.concatenate([qb, vb], axis=1).T.astype(BF16),
                                  cos_t, sin_t, rope_tiles=(), scale_tiles=(0,),
                                  scale=q_scale * LOG2E, tm=tm, tn=PROJ_TILE)
            wf_pad = jnp.pad(wf, ((0, 0), (0, LANES - n_heads_b))).astype(BF16)
            bf_pad = jnp.pad(hyb_b_f[e], (0, LANES - n_heads_b))[None, :]
            cfeat = _fox_gate(h, g_attn, wf_pad, bf_pad, tc=ts, n_heads=n_heads_b)
            pats = [_dilated_pattern(proj, 1, width=width, per_res=4, tq=min(1024, s))]
            pats += [_dilated_pattern(v, dil, width=width, per_res=3, tq=min(1024, s // dil))
                     for v, dil in zip(views, dils[1:])]
            ob = _flash_pair(proj_t, proj, (cfeat,), mode="fox", n_groups=n_pairs, gps=gps,
                             q_row=0, v_row=n_pairs // gps, k_col=3 * n_pairs // gps,
                             tq=tq_flash, tk=tk_flash)
            h = _hyb_out(pats, dils, ob, hyb_w_out[e].astype(BF16), h, tm=ts)
        else:
            o = l // 2
            w = diff_w_qkv[o]
            wq, wk, wv = w[:, :dq], w[:, dq:2 * dq], w[:, 2 * dq:]
            k_tiles = dq // PROJ_TILE
            keys = _norm_proj(h, g_attn, wk.astype(BF16), cos, sin_signed,
                              rope_tiles=tuple(range(k_tiles)), scale_tiles=(),
                              scale=1.0, tm=tm, tn=PROJ_TILE)
            proj_t = _norm_proj_t(h, g_attn, jnp.concatenate([wq, wv], axis=1).T.astype(BF16),
                                  cos_t, sin_t, rope_tiles=tuple(range(k_tiles)),
                                  scale_tiles=tuple(range(k_tiles)), scale=q_scale * LOG2E,
                                  tm=tm, tn=PROJ_TILE)
            lam_init = 0.8 - 0.6 * math.exp(-0.3 * l)
            att = _flash_pair(proj_t, keys, (diff_lambda[o], diff_subln[o][:, None]),
                              mode="diff", n_groups=n_diff_heads, gps=n_diff_heads, q_row=0,
                              v_row=1, k_col=0, tq=tq_flash, tk=tk_flash,
                              lam_init=lam_init)
            h = _proj_res(att, diff_w_out[o].astype(BF16), h, tm=tm)
        h = _ffn(h, ffn_norm[l][None, :], ffn_w_up[l].astype(BF16), ffn_conv_w[l],
                 ffn_conv_b[l][None, :], ffn_w_down[l].astype(BF16),
                 final_norm[None, :] if l == depth - 1 else None, tm=ts, n_chunks=1)
    return h[None]
```

```python
import functools
import math

import jax
import jax.numpy as jnp
from jax import lax
from jax.experimental import pallas as pl
from jax.experimental.pallas import tpu as pltpu

F32 = jnp.float32
BF16 = jnp.bfloat16

HEAD_DIM = 64
LANES = 128
BF16_SUBLANES = 16
ROPE_THETA = 10000.0
DILATED_PATTERNS = ((128, 1), (512, 4), (2048, 16))
BAND = 128
NORM_EPS = 1e-6
SUBLN_EPS = 1e-5
CONV_WIDTH = 3
NEG = -1e30
VMEM_LIMIT = 48 * 1024 * 1024
PROJ_TILE = 512

NT_DIMS = (((1,), (1,)), ((), ()))


def _params(sem):
    return pltpu.CompilerParams(dimension_semantics=sem, vmem_limit_bytes=VMEM_LIMIT)


def _rms(x, g, eps):
    return x * lax.rsqrt(jnp.mean(x * x, axis=-1, keepdims=True) + eps) * g


def _lane_lo(rows):
    return lax.broadcasted_iota(jnp.int32, (rows, LANES), 1) < HEAD_DIM


def _rope_tile(x, cos, sin_signed):
    rows = x.shape[0]
    first_half = (lax.broadcasted_iota(jnp.int32, (rows, LANES), 1) & 32) == 0
    out = []
    for c in range(x.shape[1] // LANES):
        xc = x[:, c * LANES:(c + 1) * LANES]
        ahead = pltpu.roll(xc, LANES - 32, 1)
        behind = pltpu.roll(xc, 32, 1)
        rot = jnp.where(first_half, ahead, behind)
        out.append(xc * cos + rot * sin_signed)
    return jnp.concatenate(out, axis=1)


def _norm_proj_kernel(h_ref, g_ref, w_ref, cos_ref, sin_ref, o_ref, *rest,
                      rope_tiles, scale_tiles, scale, tn, strided_tiles, dilations):
    n = _rms(h_ref[...], g_ref[...], NORM_EPS).astype(BF16)
    tm = h_ref.shape[0]
    for j in range(w_ref.shape[1] // tn):
        cols = slice(j * tn, (j + 1) * tn)
        acc = jnp.dot(n, w_ref[:, cols], preferred_element_type=F32)
        if j in rope_tiles:
            acc = _rope_tile(acc, cos_ref[...], sin_ref[...])
        if j in scale_tiles:
            acc = acc * scale
        o_ref[:, cols] = acc.astype(BF16)
        if j < strided_tiles:
            stage = rest[-1]
            for c in range(tn // LANES):
                stage[c] = acc[:, c * LANES:(c + 1) * LANES]
            for d_ref, dil in zip(rest[:-1], dilations):
                for r in range(dil):
                    for c in range(tn // LANES):
                        c0 = (r * strided_tiles + j) * tn + c * LANES
                        rows = stage[c, pl.ds(r, tm // dil, stride=dil), :]
                        d_ref[:, c0:c0 + LANES] = rows.astype(BF16)


def _norm_proj(h, g, w, cos, sin_signed, *, rope_tiles, scale_tiles, scale, tm, tn,
               strided_tiles=0, dilations=()):
    s, d = h.shape
    n = w.shape[1]
    kern = functools.partial(_norm_proj_kernel, rope_tiles=rope_tiles, scale_tiles=scale_tiles,
                             scale=scale, tn=tn, strided_tiles=strided_tiles,
                             dilations=dilations)
    out_specs = [pl.BlockSpec((tm, n), lambda i: (i, 0))]
    out_shape = [jax.ShapeDtypeStruct((s, n), BF16)]
    for dil in dilations:
        wide = dil * strided_tiles * tn
        out_specs.append(pl.BlockSpec((tm // dil, wide), lambda i: (i, 0)))
        out_shape.append(jax.ShapeDtypeStruct((s // dil, wide), BF16))
    outs = pl.pallas_call(
        kern,
        grid=(s // tm,),
        in_specs=[
            pl.BlockSpec((tm, d), lambda i: (i, 0)),
            pl.BlockSpec((1, d), lambda i: (0, 0)),
            pl.BlockSpec((d, n), lambda i: (0, 0)),
            pl.BlockSpec((tm, LANES), lambda i: (i, 0)),
            pl.BlockSpec((tm, LANES), lambda i: (i, 0)),
        ],
        out_specs=out_specs,
        out_shape=out_shape,
        scratch_shapes=[pltpu.VMEM((tn // LANES, tm, LANES), F32)] if dilations else [],
        compiler_params=_params(("parallel",)),
        name="norm_proj",
    )(h, g, w, cos, sin_signed)
    return outs if dilations else outs[0]


def _norm_proj_t_kernel(h_ref, g_ref, wt_ref, cos_ref, sin_ref, o_ref, *,
                        rope_tiles, scale_tiles, scale, tn):
    n = _rms(h_ref[...], g_ref[...], NORM_EPS).astype(BF16)
    half = HEAD_DIM // 2
    for j in range(wt_ref.shape[0] // tn):
        acc = lax.dot_general(wt_ref[j * tn:(j + 1) * tn, :], n, NT_DIMS,
                              preferred_element_type=F32)
        sc = scale if j in scale_tiles else 1.0
        if j in rope_tiles:
            cos, sin = cos_ref[...], sin_ref[...]
            for hd in range(tn // HEAD_DIM):
                r0 = j * tn + hd * HEAD_DIM
                x1 = acc[hd * HEAD_DIM:hd * HEAD_DIM + half]
                x2 = acc[hd * HEAD_DIM + half:(hd + 1) * HEAD_DIM]
                o_ref[r0:r0 + half] = ((x1 * cos - x2 * sin) * sc).astype(BF16)
                o_ref[r0 + half:r0 + HEAD_DIM] = ((x2 * cos + x1 * sin) * sc).astype(BF16)
        else:
            o_ref[j * tn:(j + 1) * tn] = (acc * sc).astype(BF16)


def _norm_proj_t(h, g, wt, cos_t, sin_t, *, rope_tiles, scale_tiles, scale, tm, tn):
    s, d = h.shape
    n = wt.shape[0]
    kern = functools.partial(_norm_proj_t_kernel, rope_tiles=rope_tiles,
                             scale_tiles=scale_tiles, scale=scale, tn=tn)
    return pl.pallas_call(
        kern,
        grid=(s // tm,),
        in_specs=[
            pl.BlockSpec((tm, d), lambda i: (i, 0)),
            pl.BlockSpec((1, d), lambda i: (0, 0)),
            pl.BlockSpec((n, d), lambda i: (0, 0)),
            pl.BlockSpec((HEAD_DIM // 2, tm), lambda i: (0, i)),
            pl.BlockSpec((HEAD_DIM // 2, tm), lambda i: (0, i)),
        ],
        out_specs=pl.BlockSpec((n, tm), lambda i: (0, i)),
        out_shape=jax.ShapeDtypeStruct((n, s), BF16),
        compiler_params=_params(("parallel",)),
        name="norm_proj_t",
    )(h, g, wt, cos_t, sin_t)


GATE_TERMS = 3
GATE_STRIDE = 8
LOG2E = math.log2(math.e)


def _fox_gate_kernel(h_ref, g_ref, wf_ref, bf_ref, cf_ref, carry, *, tc, n_heads):
    i = pl.program_id(0)

    @pl.when(i == 0)
    def _():
        carry[...] = jnp.zeros_like(carry)

    n = _rms(h_ref[...], g_ref[...], NORM_EPS).astype(BF16)
    z = jnp.dot(n, wf_ref[...], preferred_element_type=F32) + bf_ref[...]
    logf = jnp.minimum(z, 0.0) - jnp.log(1.0 + jnp.exp(-jnp.abs(z)))
    dst = lax.broadcasted_iota(jnp.int32, (tc, tc), 0)
    src = lax.broadcasted_iota(jnp.int32, (tc, tc), 1)
    prefix = jnp.where(src <= dst, 1.0, 0.0).astype(BF16)
    terms, rem = [], logf
    for _ in range(GATE_TERMS):
        part = rem.astype(BF16)
        terms.append(part)
        rem = rem - part.astype(F32)
    sums = jnp.dot(prefix, jnp.concatenate(terms, axis=1), preferred_element_type=F32)
    cs = sum(sums[:, t * LANES:(t + 1) * LANES] for t in range(GATE_TERMS)) + carry[0:1, :]
    carry[...] = jnp.broadcast_to(cs[tc - 1:tc, :], carry.shape)
    lane = lax.broadcasted_iota(jnp.int32, (tc, LANES), 1)
    rem = jnp.where(lane < n_heads, cs * LOG2E, 0.0)
    feat = jnp.zeros_like(rem)
    for term in range(GATE_TERMS):
        part = rem.astype(BF16).astype(F32)
        rem = rem - part
        feat = feat + (pltpu.roll(part, term * GATE_STRIDE, 1) if term else part)
    cf_ref[...] = feat.astype(BF16)


def _fox_gate(h, g, wf, b_f, *, tc, n_heads):
    s, d = h.shape
    return pl.pallas_call(
        functools.partial(_fox_gate_kernel, tc=tc, n_heads=n_heads),
        grid=(s // tc,),
        in_specs=[
            pl.BlockSpec((tc, d), lambda i: (i, 0)),
            pl.BlockSpec((1, d), lambda i: (0, 0)),
            pl.BlockSpec((d, LANES), lambda i: (0, 0)),
            pl.BlockSpec((1, LANES), lambda i: (0, 0)),
        ],
        out_specs=pl.BlockSpec((tc, LANES), lambda i: (i, 0)),
        out_shape=jax.ShapeDtypeStruct((s, LANES), BF16),
        scratch_shapes=[pltpu.VMEM((8, LANES), F32)],
        compiler_params=_params(("arbitrary",)),
        name="fox_gate",
    )(h, g, wf, b_f)


def _dilated_kernel(q_ref, kc_ref, kh_ref, vc_ref, vh_ref, o_ref, l_ref, kbuf, vbuf, *, tq):
    i = pl.program_id(1)
    kbuf[0:BAND] = kh_ref[...]
    kbuf[BAND:] = kc_ref[...]
    vbuf[0:BAND] = vh_ref[...]
    vbuf[BAND:] = vc_ref[...]
    row = lax.broadcasted_iota(jnp.int32, (BAND, 2 * BAND), 0)
    col = lax.broadcasted_iota(jnp.int32, (BAND, 2 * BAND), 1)
    delta = row - col + BAND
    band = (delta >= 0) & (delta <= BAND)
    lo = _lane_lo(BAND)
    for a in range(tq // BAND):
        first_key = i * tq + (a - 1) * BAND
        valid = band & (col + first_key >= 0)
        rows = slice(a * BAND, (a + 1) * BAND)
        for hp in range(q_ref.shape[1] // LANES):
            lanes = slice(hp * LANES, (hp + 1) * LANES)
            q = q_ref[rows, lanes]
            kk = kbuf[a * BAND:(a + 2) * BAND, lanes]
            vv = jnp.concatenate([vbuf[a * BAND:(a + 2) * BAND, lanes],
                                  jnp.ones((2 * BAND, LANES), BF16)], axis=1)
            outs, lses = [], []
            for qh in (jnp.where(lo, q, jnp.zeros_like(q)), jnp.where(lo, jnp.zeros_like(q), q)):
                s = lax.dot_general(qh, kk, NT_DIMS, preferred_element_type=F32)
                s = jnp.where(valid, s, NEG)
                m = jnp.max(s, axis=1, keepdims=True)
                p = jnp.exp2(s - m)
                pv = jnp.dot(p.astype(BF16), vv, preferred_element_type=F32)
                den = pv[:, LANES:]
                outs.append(pv[:, :LANES] / den)
                lses.append(m + jnp.log2(den))
            o_ref[rows, lanes] = jnp.where(lo, outs[0], outs[1])
            l_ref[rows, lanes] = jnp.where(lo, lses[0], lses[1])


def _dilated_pattern(view, dil, *, width, per_res, tq):
    L = view.shape[0]
    halo_per_tile = tq // BAND
    cur = lambda part: pl.BlockSpec((tq, width), lambda r, i: (i, r * per_res + part))
    halo = lambda part: pl.BlockSpec(
        (BAND, width), lambda r, i: (jnp.maximum(i * halo_per_tile - 1, 0), r * per_res + part))
    out_spec = pl.BlockSpec((tq, width), lambda r, i: (i, r))
    return pl.pallas_call(
        functools.partial(_dilated_kernel, tq=tq),
        grid=(dil, L // tq),
        in_specs=[cur(0), cur(1), halo(1), cur(2), halo(2)],
        out_specs=[out_spec, out_spec],
        out_shape=[jax.ShapeDtypeStruct((L, dil * width), F32)] * 2,
        scratch_shapes=[pltpu.VMEM((tq + BAND, width), BF16)] * 2,
        compiler_params=_params(("parallel", "arbitrary")),
        name=f"dilated_d{dil}",
    )(view, view, view, view, view)


ONES_ROWS = BF16_SUBLANES


COL_TILE = 256
STRIPS_PER_TRIP = 4


def _flash_pair_kernel(qi_ref, kj_ref, qt_ref, k_ref, *rest, mode, tq, tk, gps, lam_init):
    if mode == "fox":
        cf_ref, vt_ref, o_ref, wq, m_scr, acc, kx_scr, vx_scr, *bufs = rest
    else:
        vt_ref, lam_ref, g_ref, o_ref, wq, m_scr, acc, kx_scr, vx_scr, *bufs = rest
    s_buf, mx_buf, p_buf, al_buf = bufs[0:2], bufs[2:4], bufs[4:6], bufs[6:8]
    U = STRIPS_PER_TRIP
    step = pl.program_id(1)
    qi = qi_ref[step]
    kj = kj_ref[step]
    last_kj = (qi * tq + tq - 1) // tk
    rows = acc.shape[1]
    dv = rows - ONES_ROWS
    n_ct = tq // COL_TILE
    n_strips = 2 * gps * n_ct
    n_trips = n_strips // U
    par = kj % 2

    @pl.when(kj == 0)
    def _():
        r = lax.broadcasted_iota(jnp.int32, (LANES, COL_TILE), 0)
        first = r < HEAD_DIM
        for gi in range(gps):
            for a in range(2):
                keep_q = first if a == 0 else jnp.logical_not(first)
                if mode == "fox":
                    head = 2 * (pl.program_id(0) * gps + gi) + a
                    pick = (r - head) == 0
                    for term in range(1, GATE_TERMS):
                        pick = pick | ((r - head) == term * GATE_STRIDE)
                    gate_rows = jnp.where(pick, -1.0, 0.0).astype(BF16)
                for ct in range(n_ct):
                    t = (2 * gi + a) * n_ct + ct
                    qt = qt_ref[gi * LANES:(gi + 1) * LANES, ct * COL_TILE:(ct + 1) * COL_TILE]
                    wq[t, 0:LANES] = jnp.where(keep_q, qt, jnp.zeros_like(qt))
                    if mode == "fox":
                        wq[t, LANES:2 * LANES] = gate_rows
        m_scr[...] = jnp.full_like(m_scr, NEG)
        acc[...] = jnp.zeros_like(acc)

    maps_per_v = 1 if mode == "fox" else 2
    ones = jnp.ones((ONES_ROWS, tk), BF16)
    for gi in range(gps):
        kx_scr[gi, :, 0:LANES] = k_ref[:, gi * LANES:(gi + 1) * LANES]
        if mode == "fox":
            kx_scr[gi, :, LANES:2 * LANES] = cf_ref[...]
            for a in range(2):
                v0 = gi * LANES + a * HEAD_DIM
                vx_scr[par, 2 * gi + a, 0:dv] = vt_ref[v0:v0 + HEAD_DIM, :]
                vx_scr[par, 2 * gi + a, dv:rows] = ones
        else:
            vx_scr[par, gi, 0:dv] = vt_ref[gi * LANES:(gi + 1) * LANES, :]
            vx_scr[par, gi, dv:rows] = ones

    def stage_a(j, slot, u, masked):
        s = jnp.dot(kx_scr[j // (2 * n_ct)], wq[j], preferred_element_type=F32)
        if masked:
            key = kj * tk + lax.broadcasted_iota(jnp.int32, (tk, COL_TILE), 0)
            qry = (qi * tq + (j % n_ct) * COL_TILE
                   + lax.broadcasted_iota(jnp.int32, (tk, COL_TILE), 1))
            s = jnp.where(key <= qry, s, NEG)
        s_buf[slot][u] = s
        mx_buf[slot][u] = jnp.broadcast_to(jnp.max(s, axis=0, keepdims=True), (8, COL_TILE))


    def stage_b(j0, slot, us):
        m_prevs = {u: m_scr[j0 + u] for u in us}
        for u, m_prev in list(m_prevs.items()):
            m_next = jnp.maximum(m_prev, mx_buf[slot][u])
            al_buf[slot][u] = jnp.exp2(m_prev - m_next)
            s3 = s_buf[slot][u].reshape(tk // 8, 8, COL_TILE)
            p_buf[slot][u] = jnp.exp2(s3 - m_next[None]).reshape(tk, COL_TILE).astype(BF16)
            m_prevs[u] = m_next
        for u, m_next in m_prevs.items():
            m_scr[j0 + u] = m_next

    def stage_c(j0, slot, vpar, us):
        pvs = [jnp.dot(vx_scr[vpar, (j0 + u) // (n_ct * maps_per_v)], p_buf[slot][u],
                       preferred_element_type=F32) for u in us]
        olds = [acc[j0 + u] for u in us]
        for u, old, pv in zip(us, olds, pvs):
            acc[j0 + u] = ((old.reshape(rows // 8, 8, COL_TILE) * al_buf[slot][u][None])
                           .reshape(rows, COL_TILE) + pv)

    def sweep(masked, first_step):
        def trip(k, slot, run_b=True, run_c=True):
            kc = (k + n_trips - 2) % n_trips
            kb = (k + n_trips - 1) % n_trips
            for u in range(U):
                sees_all = tq == 2 * tk and (u % n_ct) * COL_TILE >= tk - 1
                stage_a(k * U + u, slot, u, masked and not sees_all)
            if run_b:
                stage_b(kb * U, 1 - slot, range(U))
            if run_c:
                stage_c(kc * U, slot, jnp.where(k >= 2, par, 1 - par), range(U))

        def trip_pair(half, carry):
            trip(2 * half, 0)
            trip(2 * half + 1, 1)
            return carry

        if first_step:
            trip(0, 0, run_b=False, run_c=False)
            trip(1, 1, run_c=False)
            lax.fori_loop(1, n_trips // 2, trip_pair, 0)
        else:
            lax.fori_loop(0, n_trips // 2, trip_pair, 0)

    assert U % n_ct == 0
    live = tuple(u for u in range(U) if (u % n_ct) * COL_TILE >= tq - tk)

    def last_sweep():
        def trip(k, slot, b_from_prev_step, c_from_prev_step):
            kc = (k + n_trips - 2) % n_trips
            kb = (k + n_trips - 1) % n_trips
            for u in live:
                stage_a(k * U + u, slot, u, True)
            stage_b(kb * U, 1 - slot, range(U) if b_from_prev_step else live)
            stage_c(kc * U, slot, 1 - par if c_from_prev_step else par,
                    range(U) if c_from_prev_step else live)

        def trip_pair(half, carry):
            trip(2 * half, 0, False, False)
            trip(2 * half + 1, 1, False, False)
            return carry

        trip(0, 0, True, True)
        trip(1, 1, False, True)
        lax.fori_loop(1, n_trips // 2, trip_pair, 0)

    needs_mask = kj * tk + tk - 1 > qi * tq
    is_last = kj == last_kj
    is_first = kj == 0
    for masked in (True, False):
        on_mask = jnp.logical_and(needs_mask, jnp.logical_not(is_last)) if masked \
            else jnp.logical_not(needs_mask)
        for first_step in (True, False):
            on_step = is_first if first_step else jnp.logical_not(is_first)
            pl.when(jnp.logical_and(on_mask, on_step))(
                functools.partial(sweep, masked, first_step))
    pl.when(is_last)(last_sweep)

    @pl.when(is_last)
    def _():
        last = n_trips - 1
        stage_c((last - 1) * U, (last - 1) % 2, par, live)
        stage_b(last * U, last % 2, live)
        stage_c(last * U, last % 2, par, live)

        def map_out(mi):
            parts = [acc[mi * n_ct + ct] for ct in range(n_ct)]
            full = jnp.concatenate(parts, axis=1)
            return full[0:dv] / full[dv:dv + 1]

        for gi in range(gps):
            o0, o1 = map_out(2 * gi), map_out(2 * gi + 1)
            if mode == "fox":
                o = jnp.concatenate([o0, o1], axis=0)
            else:
                lp = lam_ref[...]
                t1 = jnp.sum(lp[0:1] * lp[1:2], axis=1, keepdims=True)
                t2 = jnp.sum(lp[2:3] * lp[3:4], axis=1, keepdims=True)
                lam = jnp.exp(t1) - jnp.exp(t2) + lam_init
                o = o0 - lam * o1
                ms = jnp.mean(o * o, axis=0, keepdims=True)
                o = o * lax.rsqrt(ms + SUBLN_EPS) * g_ref[...] * (1.0 - lam_init)
            o_ref[:, gi * LANES:(gi + 1) * LANES] = o.T.astype(BF16)


def _causal_steps(s, tq, tk):
    qi, kj = [], []
    for i in range(s // tq):
        for j in range((i * tq + tq - 1) // tk + 1):
            qi.append(i)
            kj.append(j)
    return jnp.asarray(qi, jnp.int32), jnp.asarray(kj, jnp.int32)


def _flash_pair(proj_t, keys, extras, *, mode, n_groups, gps, q_row, v_row, k_col, tq, tk,
                lam_init=0.0):
    s = keys.shape[0]
    qi, kj = _causal_steps(s, tq, tk)
    gw = gps * LANES
    qt_spec = pl.BlockSpec((gw, tq), lambda g, t, qi, kj: (q_row + g, qi[t]))
    k_spec = pl.BlockSpec((tk, gw), lambda g, t, qi, kj: (kj[t], k_col + g))
    vt_spec = pl.BlockSpec((gw, tk), lambda g, t, qi, kj: (v_row + g, kj[t]))
    const = lambda x: pl.BlockSpec(x.shape, lambda g, t, qi, kj: (0, 0))
    if mode == "fox":
        (cfeat,) = extras
        in_specs = [qt_spec, k_spec,
                    pl.BlockSpec((tk, LANES), lambda g, t, qi, kj: (kj[t], 0)), vt_spec]
        args = (proj_t, keys, cfeat, proj_t)
        kd, dv, n_values = 2 * LANES, HEAD_DIM, 2 * gps
    else:
        lam_params, subln_g = extras
        in_specs = [qt_spec, k_spec, vt_spec, const(lam_params), const(subln_g)]
        args = (proj_t, keys, proj_t, lam_params, subln_g)
        kd, dv, n_values = LANES, LANES, gps
    rows = dv + ONES_ROWS
    n_strips = 2 * gps * (tq // COL_TILE)
    u = STRIPS_PER_TRIP
    assert n_strips % (2 * u) == 0
    assert tq >= 2 * tk
    kern = functools.partial(_flash_pair_kernel, mode=mode, tq=tq, tk=tk, gps=gps,
                             lam_init=lam_init)
    return pl.pallas_call(
        kern,
        grid_spec=pltpu.PrefetchScalarGridSpec(
            num_scalar_prefetch=2,
            grid=(n_groups // gps, qi.shape[0]),
            in_specs=in_specs,
            out_specs=pl.BlockSpec((tq, gw), lambda g, t, qi, kj: (qi[t], g)),
            scratch_shapes=[
                pltpu.VMEM((n_strips, kd, COL_TILE), BF16),
                pltpu.VMEM((n_strips, 8, COL_TILE), F32),
                pltpu.VMEM((n_strips, rows, COL_TILE), F32),
                pltpu.VMEM((gps, tk, kd), BF16),
                pltpu.VMEM((2, n_values, rows, tk), BF16),
            ] + [pltpu.VMEM((u, tk, COL_TILE), F32)] * 2
              + [pltpu.VMEM((u, 8, COL_TILE), F32)] * 2
              + [pltpu.VMEM((u, tk, COL_TILE), BF16)] * 2
              + [pltpu.VMEM((u, 8, COL_TILE), F32)] * 2,
        ),
        out_shape=jax.ShapeDtypeStruct((s, n_groups * LANES), BF16),
        compiler_params=_params(("arbitrary", "arbitrary")),
        name=f"flash_{mode}",
    )(qi, kj, *args)


def _hyb_out_kernel(*refs, dilations):
    n_pat = len(dilations)
    pat = refs[:2 * n_pat]
    ob_ref, w_ref, h_ref, out_ref = refs[2 * n_pat:2 * n_pat + 4]
    scratch = list(refs[2 * n_pat + 4:])
    tm = h_ref.shape[0]
    vals = []
    for idx, ref in enumerate(pat):
        dil = dilations[idx // 2]
        if dil == 1:
            vals.append(ref[...])
            continue
        buf = scratch.pop(0)
        n_chunks = buf.shape[0]
        for r in range(dil):
            for c in range(n_chunks):
                c0 = (r * n_chunks + c) * LANES
                buf[c, pl.ds(r, tm // dil, stride=dil), :] = ref[:, c0:c0 + LANES]
        vals.append(jnp.concatenate([buf[c] for c in range(n_chunks)], axis=1))
    os_, ls_ = vals[0::2], vals[1::2]
    m = functools.reduce(jnp.maximum, ls_)
    es = [jnp.exp2(l - m) for l in ls_]
    oa = sum(e * o for e, o in zip(es, os_)) / sum(es)
    wa = oa.shape[1]
    acc = jnp.dot(oa.astype(BF16), w_ref[0:wa, :], preferred_element_type=F32)
    acc = acc + jnp.dot(ob_ref[...], w_ref[wa:, :], preferred_element_type=F32)
    out_ref[...] = h_ref[...] + acc


def _hyb_out(pattern_outs, dilations, ob, w, h, *, tm):
    s, d = h.shape
    wa = pattern_outs[0][0].shape[1] // dilations[0]
    in_specs, args, scratch = [], [], []
    for (o, lse), dil in zip(pattern_outs, dilations):
        for arr in (o, lse):
            in_specs.append(pl.BlockSpec((tm // dil, dil * wa), lambda i: (i, 0)))
            args.append(arr)
            if dil > 1:
                scratch.append(pltpu.VMEM((wa // LANES, tm, LANES), F32))
    return pl.pallas_call(
        functools.partial(_hyb_out_kernel, dilations=tuple(dilations)),
        grid=(s // tm,),
        in_specs=in_specs + [
            pl.BlockSpec((tm, ob.shape[1]), lambda i: (i, 0)),
            pl.BlockSpec(w.shape, lambda i: (0, 0)),
            pl.BlockSpec((tm, d), lambda i: (i, 0)),
        ],
        out_specs=pl.BlockSpec((tm, d), lambda i: (i, 0)),
        out_shape=jax.ShapeDtypeStruct((s, d), F32),
        scratch_shapes=scratch,
        compiler_params=_params(("parallel",)),
        name="hyb_out",
    )(*args, ob, w, h)


def _proj_res_kernel(a_ref, w_ref, h_ref, out_ref):
    out_ref[...] = h_ref[...] + jnp.dot(a_ref[...], w_ref[...], preferred_element_type=F32)


def _proj_res(a, w, h, *, tm):
    s, d = h.shape
    return pl.pallas_call(
        _proj_res_kernel,
        grid=(s // tm,),
        in_specs=[
            pl.BlockSpec((tm, a.shape[1]), lambda i: (i, 0)),
            pl.BlockSpec(w.shape, lambda i: (0, 0)),
            pl.BlockSpec((tm, d), lambda i: (i, 0)),
        ],
        out_specs=pl.BlockSpec((tm, d), lambda i: (i, 0)),
        out_shape=jax.ShapeDtypeStruct((s, d), F32),
        compiler_params=_params(("parallel",)),
        name="proj_res",
    )(a, w, h)


HALO = BF16_SUBLANES


FFN_SUB = 256


def _ffn_kernel(h_ref, halo_ref, g_ref, wg_ref, wu_ref, cw_ref, cb_ref, wd_ref, *rest,
                tm, final):
    if final:
        fg_ref, out_ref, n_scr, gate_scr, act_scr = rest
    else:
        out_ref, n_scr, gate_scr, act_scr = rest
    i = pl.program_id(0)
    j = pl.program_id(1)
    n_chunks, _, chunk = act_scr.shape

    @pl.when(j == 0)
    def _():
        g = g_ref[...]
        prev = jnp.where(i > 0, halo_ref[...], 0.0)
        n_scr[0:HALO] = _rms(prev, g, NORM_EPS).astype(BF16)
        n_scr[HALO:] = _rms(h_ref[...], g, NORM_EPS).astype(BF16)

    for t, c0 in enumerate(range(0, chunk, FFN_SUB)):
        w = min(FFN_SUB, chunk - c0)
        cols = slice(c0, c0 + w)
        gate_scr[t, :, 0:w] = jnp.dot(n_scr[...], wg_ref[:, cols], preferred_element_type=F32)
        up = jnp.dot(n_scr[HALO:], wu_ref[:, cols], preferred_element_type=F32)
        conv = cb_ref[:, cols]
        for k in range(CONV_WIDTH):
            start = HALO - (CONV_WIDTH - 1) + k
            conv = conv + gate_scr[t, start:start + tm, 0:w] * cw_ref[k:k + 1, cols]
        act_scr[j, :, cols] = (conv * (1.0 / (1.0 + jnp.exp(-conv))) * up).astype(BF16)

    @pl.when(j == n_chunks - 1)
    def _():
        act = jnp.concatenate([act_scr[t] for t in range(n_chunks)], axis=1)
        y = h_ref[...] + jnp.dot(act, wd_ref[...], preferred_element_type=F32)
        if final:
            y = _rms(y, fg_ref[...], NORM_EPS)
        out_ref[...] = y


def _ffn(h, g, w_up, conv_w, conv_b, w_down, final_g, *, tm, n_chunks):
    s, d = h.shape
    d_ff = w_down.shape[0]
    chunk = d_ff // n_chunks
    assert chunk * n_chunks == d_ff and chunk % LANES == 0
    n_sub = -(-chunk // FFN_SUB)
    final = final_g is not None
    once = pl.Buffered(1)
    up_mode = once if n_chunks == 1 else None
    in_specs = [
        pl.BlockSpec((tm, d), lambda i, j: (i, 0)),
        pl.BlockSpec((HALO, d), lambda i, j: (jnp.maximum(i * (tm // HALO) - 1, 0), 0)),
        pl.BlockSpec((1, d), lambda i, j: (0, 0)),
        pl.BlockSpec((d, chunk), lambda i, j: (0, j), pipeline_mode=up_mode),
        pl.BlockSpec((d, chunk), lambda i, j: (0, n_chunks + j), pipeline_mode=up_mode),
        pl.BlockSpec((CONV_WIDTH, chunk), lambda i, j: (0, j)),
        pl.BlockSpec((1, chunk), lambda i, j: (0, j)),
        pl.BlockSpec((d_ff, d), lambda i, j: (0, 0), pipeline_mode=once),
    ]
    args = [h, h, g, w_up, w_up, conv_w, conv_b, w_down]
    if final:
        in_specs.append(pl.BlockSpec((1, d), lambda i, j: (0, 0)))
        args.append(final_g)
    return pl.pallas_call(
        functools.partial(_ffn_kernel, tm=tm, final=final),
        grid=(s // tm, n_chunks),
        in_specs=in_specs,
        out_specs=pl.BlockSpec((tm, d), lambda i, j: (i, 0)),
        out_shape=jax.ShapeDtypeStruct((s, d), F32),
        scratch_shapes=[
            pltpu.VMEM((tm + HALO, d), BF16),
            pltpu.VMEM((n_sub, tm + HALO, FFN_SUB), F32),
            pltpu.VMEM((n_chunks, tm, chunk), BF16),
        ],
        compiler_params=_params(("parallel", "arbitrary")),
        name="ffn_final" if final else "ffn",
    )(*args)


def _rope_tables(s):
    inv = 1.0 / (ROPE_THETA ** (jnp.arange(0, HEAD_DIM, 2, dtype=F32) / HEAD_DIM))
    ang = jnp.arange(s, dtype=F32)[:, None] * inv[None, :]
    cos, sin = jnp.cos(ang), jnp.sin(ang)
    sign = jnp.where((jnp.arange(LANES) & 32) == 0, -1.0, 1.0).astype(F32)
    cos_l = jnp.tile(cos, (1, LANES // 32))
    sin_signed = jnp.tile(sin, (1, LANES // 32)) * sign[None, :]
    return cos_l, sin_signed, cos.T, sin.T


def kernel(x, attn_norm, ffn_norm, final_norm, hyb_w_in, hyb_b_f, hyb_w_out, diff_w_qkv,
           diff_lambda, diff_subln, diff_w_out, ffn_w_up, ffn_conv_w, ffn_conv_b, ffn_w_down):
    b, s, d = x.shape
    assert b == 1
    depth = attn_norm.shape[0]
    width = hyb_w_out.shape[1] // 2
    n_pairs = width // LANES
    n_heads_b = width // HEAD_DIM
    n_diff_heads = diff_w_out.shape[1] // LANES
    dq = n_diff_heads * LANES
    tm = min(1024, s)
    ts = min(512, s)
    tq_flash, tk_flash = min(1024, s), min(512, s)
    gps = 4
    q_scale = HEAD_DIM ** -0.5
    cos, sin_signed, cos_t, sin_t = _rope_tables(s)
    h = x[0]

    for l in range(depth):
        g_attn = attn_norm[l][None, :]
        if l % 2 == 0:
            e = l // 2
            w_in = hyb_w_in[e]
            qa_ka_va, qb, kb, vb, wf = (w_in[:, :3 * width], w_in[:, 3 * width:4 * width],
                                        w_in[:, 4 * width:5 * width], w_in[:, 5 * width:6 * width],
                                        w_in[:, 6 * width:])
            dils = tuple(dil for _, dil in DILATED_PATTERNS)
            proj, *views = _norm_proj(
                h, g_attn, jnp.concatenate([qa_ka_va, kb], axis=1).astype(BF16), cos, sin_signed,
                rope_tiles=(0, 1), scale_tiles=(0,), scale=q_scale * LOG2E, tm=tm, tn=PROJ_TILE,
                strided_tiles=3, dilations=dils[1:])
            proj_t = _norm_proj_t(h, g_attn, jnp.concatenate([qb, vb], axis=1).T.astype(BF16),
                                  cos_t, sin_t, rope_tiles=(), scale_tiles=(0,),
                                  scale=q_scale * LOG2E, tm=tm, tn=PROJ_TILE)
            wf_pad = jnp.pad(wf, ((0, 0), (0, LANES - n_heads_b))).astype(BF16)
            bf_pad = jnp.pad(hyb_b_f[e], (0, LANES - n_heads_b))[None, :]
            cfeat = _fox_gate(h, g_attn, wf_pad, bf_pad, tc=ts, n_heads=n_heads_b)
            pats = [_dilated_pattern(proj, 1, width=width, per_res=4, tq=min(1024, s))]
            pats += [_dilated_pattern(v, dil, width=width, per_res=3, tq=min(1024, s // dil))
                     for v, dil in zip(views, dils[1:])]
            ob = _flash_pair(proj_t, proj, (cfeat,), mode="fox", n_groups=n_pairs, gps=gps,
                             q_row=0, v_row=n_pairs // gps, k_col=3 * n_pairs // gps,
                             tq=tq_flash, tk=tk_flash)
            h = _hyb_out(pats, dils, ob, hyb_w_out[e].astype(BF16), h, tm=ts)
        else:
            o = l // 2
            w = diff_w_qkv[o]
            wq, wk, wv = w[:, :dq], w[:, dq:2 * dq], w[:, 2 * dq:]
            k_tiles = dq // PROJ_TILE
            keys = _norm_proj(h, g_attn, wk.astype(BF16), cos, sin_signed,
                              rope_tiles=tuple(range(k_tiles)), scale_tiles=(),
                              scale=1.0, tm=tm, tn=PROJ_TILE)
            proj_t = _norm_proj_t(h, g_attn, jnp.concatenate([wq, wv], axis=1).T.astype(BF16),
                                  cos_t, sin_t, rope_tiles=tuple(range(k_tiles)),
                                  scale_tiles=tuple(range(k_tiles)), scale=q_scale * LOG2E,
                                  tm=tm, tn=PROJ_TILE)
            lam_init = 0.8 - 0.6 * math.exp(-0.3 * l)
            att = _flash_pair(proj_t, keys, (diff_lambda[o], diff_subln[o][:, None]),
                              mode="diff", n_groups=n_diff_heads, gps=n_diff_heads, q_row=0,
                              v_row=1, k_col=0, tq=tq_flash, tk=tk_flash,
                              lam_init=lam_init)
            h = _proj_res(att, diff_w_out[o].astype(BF16), h, tm=tm)
        h = _ffn(h, ffn_norm[l][None, :], ffn_w_up[l].astype(BF16), ffn_conv_w[l],
                 ffn_conv_b[l][None, :], ffn_w_down[l].astype(BF16),
                 final_norm[None, :] if l == depth - 1 else None, tm=ts, n_chunks=1)
    return h[None]
```

```python
import functools
import math

import jax
import jax.numpy as jnp
from jax import lax
from jax.experimental import pallas as pl
from jax.experimental.pallas import tpu as pltpu

F32 = jnp.float32
BF16 = jnp.bfloat16

HEAD_DIM = 64
LANES = 128
BF16_SUBLANES = 16
ROPE_THETA = 10000.0
DILATED_PATTERNS = ((128, 1), (512, 4), (2048, 16))
BAND = 128
NORM_EPS = 1e-6
SUBLN_EPS = 1e-5
CONV_WIDTH = 3
NEG = -1e30
VMEM_LIMIT = 48 * 1024 * 1024
PROJ_TILE = 512

NT_DIMS = (((1,), (1,)), ((), ()))


def _params(sem):
    return pltpu.CompilerParams(dimension_semantics=sem, vmem_limit_bytes=VMEM_LIMIT)


def _rms(x, g, eps):
    return x * lax.rsqrt(jnp.mean(x * x, axis=-1, keepdims=True) + eps) * g


def _lane_lo(rows):
    return lax.broadcasted_iota(jnp.int32, (rows, LANES), 1) < HEAD_DIM


def _rope_tile(x, cos, sin_signed):
    rows = x.shape[0]
    first_half = (lax.broadcasted_iota(jnp.int32, (rows, LANES), 1) & 32) == 0
    out = []
    for c in range(x.shape[1] // LANES):
        xc = x[:, c * LANES:(c + 1) * LANES]
        ahead = pltpu.roll(xc, LANES - 32, 1)
        behind = pltpu.roll(xc, 32, 1)
        rot = jnp.where(first_half, ahead, behind)
        out.append(xc * cos + rot * sin_signed)
    return jnp.concatenate(out, axis=1)


def _norm_proj_kernel(h_ref, g_ref, w_ref, cos_ref, sin_ref, o_ref, *rest,
                      rope_tiles, scale_tiles, scale, tn, strided_tiles, dilations):
    n = _rms(h_ref[...], g_ref[...], NORM_EPS).astype(BF16)
    tm = h_ref.shape[0]
    for j in range(w_ref.shape[1] // tn):
        cols = slice(j * tn, (j + 1) * tn)
        acc = jnp.dot(n, w_ref[:, cols], preferred_element_type=F32)
        if j in rope_tiles:
            acc = _rope_tile(acc, cos_ref[...], sin_ref[...])
        if j in scale_tiles:
            acc = acc * scale
        o_ref[:, cols] = acc.astype(BF16)
        if j < strided_tiles:
            stage = rest[-1]
            for c in range(tn // LANES):
                stage[c] = acc[:, c * LANES:(c + 1) * LANES]
            for d_ref, dil in zip(rest[:-1], dilations):
                for r in range(dil):
                    for c in range(tn // LANES):
                        c0 = (r * strided_tiles + j) * tn + c * LANES
                        rows = stage[c, pl.ds(r, tm // dil, stride=dil), :]
                        d_ref[:, c0:c0 + LANES] = rows.astype(BF16)


def _norm_proj(h, g, w, cos, sin_signed, *, rope_tiles, scale_tiles, scale, tm, tn,
               strided_tiles=0, dilations=()):
    s, d = h.shape
    n = w.shape[1]
    kern = functools.partial(_norm_proj_kernel, rope_tiles=rope_tiles, scale_tiles=scale_tiles,
                             scale=scale, tn=tn, strided_tiles=strided_tiles,
                             dilations=dilations)
    out_specs = [pl.BlockSpec((tm, n), lambda i: (i, 0))]
    out_shape = [jax.ShapeDtypeStruct((s, n), BF16)]
    for dil in dilations:
        wide = dil * strided_tiles * tn
        out_specs.append(pl.BlockSpec((tm // dil, wide), lambda i: (i, 0)))
        out_shape.append(jax.ShapeDtypeStruct((s // dil, wide), BF16))
    outs = pl.pallas_call(
        kern,
        grid=(s // tm,),
        in_specs=[
            pl.BlockSpec((tm, d), lambda i: (i, 0)),
            pl.BlockSpec((1, d), lambda i: (0, 0)),
            pl.BlockSpec((d, n), lambda i: (0, 0)),
            pl.BlockSpec((tm, LANES), lambda i: (i, 0)),
            pl.BlockSpec((tm, LANES), lambda i: (i, 0)),
        ],
        out_specs=out_specs,
        out_shape=out_shape,
        scratch_shapes=[pltpu.VMEM((tn // LANES, tm, LANES), F32)] if dilations else [],
        compiler_params=_params(("parallel",)),
        name="norm_proj",
    )(h, g, w, cos, sin_signed)
    return outs if dilations else outs[0]


def _norm_proj_t_kernel(h_ref, g_ref, wt_ref, cos_ref, sin_ref, o_ref, *,
                        rope_tiles, scale_tiles, scale, tn):
    n = _rms(h_ref[...], g_ref[...], NORM_EPS).astype(BF16)
    half = HEAD_DIM // 2
    for j in range(wt_ref.shape[0] // tn):
        acc = lax.dot_general(wt_ref[j * tn:(j + 1) * tn, :], n, NT_DIMS,
                              preferred_element_type=F32)
        sc = scale if j in scale_tiles else 1.0
        if j in rope_tiles:
            cos, sin = cos_ref[...], sin_ref[...]
            for hd in range(tn // HEAD_DIM):
                r0 = j * tn + hd * HEAD_DIM
                x1 = acc[hd * HEAD_DIM:hd * HEAD_DIM + half]
                x2 = acc[hd * HEAD_DIM + half:(hd + 1) * HEAD_DIM]
                o_ref[r0:r0 + half] = ((x1 * cos - x2 * sin) * sc).astype(BF16)
                o_ref[r0 + half:r0 + HEAD_DIM] = ((x2 * cos + x1 * sin) * sc).astype(BF16)
        else:
            o_ref[j * tn:(j + 1) * tn] = (acc * sc).astype(BF16)


def _norm_proj_t(h, g, wt, cos_t, sin_t, *, rope_tiles, scale_tiles, scale, tm, tn):
    s, d = h.shape
    n = wt.shape[0]
    kern = functools.partial(_norm_proj_t_kernel, rope_tiles=rope_tiles,
                             scale_tiles=scale_tiles, scale=scale, tn=tn)
    return pl.pallas_call(
        kern,
        grid=(s // tm,),
        in_specs=[
            pl.BlockSpec((tm, d), lambda i: (i, 0)),
            pl.BlockSpec((1, d), lambda i: (0, 0)),
            pl.BlockSpec((n, d), lambda i: (0, 0)),
            pl.BlockSpec((HEAD_DIM // 2, tm), lambda i: (0, i)),
            pl.BlockSpec((HEAD_DIM // 2, tm), lambda i: (0, i)),
        ],
        out_specs=pl.BlockSpec((n, tm), lambda i: (0, i)),
        out_shape=jax.ShapeDtypeStruct((n, s), BF16),
        compiler_params=_params(("parallel",)),
        name="norm_proj_t",
    )(h, g, wt, cos_t, sin_t)


GATE_TERMS = 3
GATE_STRIDE = 8
LOG2E = math.log2(math.e)


def _fox_gate_kernel(h_ref, g_ref, wf_ref, bf_ref, cf_ref, carry, *, tc, n_heads):
    i = pl.program_id(0)

    @pl.when(i == 0)
    def _():
        carry[...] = jnp.zeros_like(carry)

    n = _rms(h_ref[...], g_ref[...], NORM_EPS).astype(BF16)
    z = jnp.dot(n, wf_ref[...], preferred_element_type=F32) + bf_ref[...]
    logf = jnp.minimum(z, 0.0) - jnp.log(1.0 + jnp.exp(-jnp.abs(z)))
    dst = lax.broadcasted_iota(jnp.int32, (tc, tc), 0)
    src = lax.broadcasted_iota(jnp.int32, (tc, tc), 1)
    prefix = jnp.where(src <= dst, 1.0, 0.0).astype(BF16)
    terms, rem = [], logf
    for _ in range(GATE_TERMS):
        part = rem.astype(BF16)
        terms.append(part)
        rem = rem - part.astype(F32)
    sums = jnp.dot(prefix, jnp.concatenate(terms, axis=1), preferred_element_type=F32)
    cs = sum(sums[:, t * LANES:(t + 1) * LANES] for t in range(GATE_TERMS)) + carry[0:1, :]
    carry[...] = jnp.broadcast_to(cs[tc - 1:tc, :], carry.shape)
    lane = lax.broadcasted_iota(jnp.int32, (tc, LANES), 1)
    rem = jnp.where(lane < n_heads, cs * LOG2E, 0.0)
    feat = jnp.zeros_like(rem)
    for term in range(GATE_TERMS):
        part = rem.astype(BF16).astype(F32)
        rem = rem - part
        feat = feat + (pltpu.roll(part, term * GATE_STRIDE, 1) if term else part)
    cf_ref[...] = feat.astype(BF16)


def _fox_gate(h, g, wf, b_f, *, tc, n_heads):
    s, d = h.shape
    return pl.pallas_call(
        functools.partial(_fox_gate_kernel, tc=tc, n_heads=n_heads),
        grid=(s // tc,),
        in_specs=[
            pl.BlockSpec((tc, d), lambda i: (i, 0)),
            pl.BlockSpec((1, d), lambda i: (0, 0)),
            pl.BlockSpec((d, LANES), lambda i: (0, 0)),
            pl.BlockSpec((1, LANES), lambda i: (0, 0)),
        ],
        out_specs=pl.BlockSpec((tc, LANES), lambda i: (i, 0)),
        out_shape=jax.ShapeDtypeStruct((s, LANES), BF16),
        scratch_shapes=[pltpu.VMEM((8, LANES), F32)],
        compiler_params=_params(("arbitrary",)),
        name="fox_gate",
    )(h, g, wf, b_f)


def _dilated_kernel(q_ref, kc_ref, kh_ref, vc_ref, vh_ref, o_ref, l_ref, kbuf, vbuf, *, tq):
    i = pl.program_id(1)
    kbuf[0:BAND] = kh_ref[...]
    kbuf[BAND:] = kc_ref[...]
    vbuf[0:BAND] = vh_ref[...]
    vbuf[BAND:] = vc_ref[...]
    row = lax.broadcasted_iota(jnp.int32, (BAND, 2 * BAND), 0)
    col = lax.broadcasted_iota(jnp.int32, (BAND, 2 * BAND), 1)
    delta = row - col + BAND
    band = (delta >= 0) & (delta <= BAND)
    lo = _lane_lo(BAND)
    for a in range(tq // BAND):
        first_key = i * tq + (a - 1) * BAND
        valid = band & (col + first_key >= 0)
        rows = slice(a * BAND, (a + 1) * BAND)
        for hp in range(q_ref.shape[1] // LANES):
            lanes = slice(hp * LANES, (hp + 1) * LANES)
            q = q_ref[rows, lanes]
            kk = kbuf[a * BAND:(a + 2) * BAND, lanes]
            vv = jnp.concatenate([vbuf[a * BAND:(a + 2) * BAND, lanes],
                                  jnp.ones((2 * BAND, LANES), BF16)], axis=1)
            outs, lses = [], []
            for qh in (jnp.where(lo, q, jnp.zeros_like(q)), jnp.where(lo, jnp.zeros_like(q), q)):
                s = lax.dot_general(qh, kk, NT_DIMS, preferred_element_type=F32)
                s = jnp.where(valid, s, NEG)
                m = jnp.max(s, axis=1, keepdims=True)
                p = jnp.exp2(s - m)
                pv = jnp.dot(p.astype(BF16), vv, preferred_element_type=F32)
                den = pv[:, LANES:]
                outs.append(pv[:, :LANES] / den)
                lses.append(m + jnp.log2(den))
            o_ref[rows, lanes] = jnp.where(lo, outs[0], outs[1])
            l_ref[rows, lanes] = jnp.where(lo, lses[0], lses[1])


def _dilated_pattern(view, dil, *, width, per_res, tq):
    L = view.shape[0]
    halo_per_tile = tq // BAND
    cur = lambda part: pl.BlockSpec((tq, width), lambda r, i: (i, r * per_res + part))
    halo = lambda part: pl.BlockSpec(
        (BAND, width), lambda r, i: (jnp.maximum(i * halo_per_tile - 1, 0), r * per_res + part))
    out_spec = pl.BlockSpec((tq, width), lambda r, i: (i, r))
    return pl.pallas_call(
        functools.partial(_dilated_kernel, tq=tq),
        grid=(dil, L // tq),
        in_specs=[cur(0), cur(1), halo(1), cur(2), halo(2)],
        out_specs=[out_spec, out_spec],
        out_shape=[jax.ShapeDtypeStruct((L, dil * width), F32)] * 2,
        scratch_shapes=[pltpu.VMEM((tq + BAND, width), BF16)] * 2,
        compiler_params=_params(("parallel", "arbitrary")),
        name=f"dilated_d{dil}",
    )(view, view, view, view, view)


ONES_ROWS = BF16_SUBLANES


COL_TILE = 256
STRIPS_PER_TRIP = 4


def _flash_pair_kernel(qi_ref, kj_ref, qt_ref, k_ref, *rest, mode, tq, tk, gps, lam_init):
    if mode == "fox":
        cf_ref, vt_ref, o_ref, wq, m_scr, acc, kx_scr, vx_scr, *bufs = rest
    else:
        vt_ref, lam_ref, g_ref, o_ref, wq, m_scr, acc, kx_scr, vx_scr, *bufs = rest
    s_buf, mx_buf, p_buf, al_buf = bufs[0:2], bufs[2:4], bufs[4:6], bufs[6:8]
    U = STRIPS_PER_TRIP
    step = pl.program_id(1)
    qi = qi_ref[step]
    kj = kj_ref[step]
    last_kj = (qi * tq + tq - 1) // tk
    rows = acc.shape[1]
    dv = rows - ONES_ROWS
    n_ct = tq // COL_TILE
    n_strips = 2 * gps * n_ct
    n_trips = n_strips // U
    par = kj % 2

    @pl.when(kj == 0)
    def _():
        r = lax.broadcasted_iota(jnp.int32, (LANES, COL_TILE), 0)
        first = r < HEAD_DIM
        for gi in range(gps):
            for a in range(2):
                keep_q = first if a == 0 else jnp.logical_not(first)
                if mode == "fox":
                    head = 2 * (pl.program_id(0) * gps + gi) + a
                    pick = (r - head) == 0
                    for term in range(1, GATE_TERMS):
                        pick = pick | ((r - head) == term * GATE_STRIDE)
                    gate_rows = jnp.where(pick, -1.0, 0.0).astype(BF16)
                for ct in range(n_ct):
                    t = (2 * gi + a) * n_ct + ct
                    qt = qt_ref[gi * LANES:(gi + 1) * LANES, ct * COL_TILE:(ct + 1) * COL_TILE]
                    wq[t, 0:LANES] = jnp.where(keep_q, qt, jnp.zeros_like(qt))
                    if mode == "fox":
                        wq[t, LANES:2 * LANES] = gate_rows
        m_scr[...] = jnp.full_like(m_scr, NEG)
        acc[...] = jnp.zeros_like(acc)

    maps_per_v = 1 if mode == "fox" else 2
    ones = jnp.ones((ONES_ROWS, tk), BF16)
    for gi in range(gps):
        kx_scr[gi, :, 0:LANES] = k_ref[:, gi * LANES:(gi + 1) * LANES]
        if mode == "fox":
            kx_scr[gi, :, LANES:2 * LANES] = cf_ref[...]
            for a in range(2):
                v0 = gi * LANES + a * HEAD_DIM
                vx_scr[par, 2 * gi + a, 0:dv] = vt_ref[v0:v0 + HEAD_DIM, :]
                vx_scr[par, 2 * gi + a, dv:rows] = ones
        else:
            vx_scr[par, gi, 0:dv] = vt_ref[gi * LANES:(gi + 1) * LANES, :]
            vx_scr[par, gi, dv:rows] = ones

    def stage_a(j, slot, u, masked):
        s = jnp.dot(kx_scr[j // (2 * n_ct)], wq[j], preferred_element_type=F32)
        if masked:
            key = kj * tk + lax.broadcasted_iota(jnp.int32, (tk, COL_TILE), 0)
            qry = (qi * tq + (j % n_ct) * COL_TILE
                   + lax.broadcasted_iota(jnp.int32, (tk, COL_TILE), 1))
            s = jnp.where(key <= qry, s, NEG)
        s_buf[slot][u] = s
        mx_buf[slot][u] = jnp.broadcast_to(jnp.max(s, axis=0, keepdims=True), (8, COL_TILE))


    def stage_b(j0, slot, us):
        m_prevs = {u: m_scr[j0 + u] for u in us}
        for u, m_prev in list(m_prevs.items()):
            m_next = jnp.maximum(m_prev, mx_buf[slot][u])
            al_buf[slot][u] = jnp.exp2(m_prev - m_next)
            s3 = s_buf[slot][u].reshape(tk // 8, 8, COL_TILE)
            p_buf[slot][u] = jnp.exp2(s3 - m_next[None]).reshape(tk, COL_TILE).astype(BF16)
            m_prevs[u] = m_next
        for u, m_next in m_prevs.items():
            m_scr[j0 + u] = m_next

    def stage_c(j0, slot, vpar, us):
        pvs = [jnp.dot(vx_scr[vpar, (j0 + u) // (n_ct * maps_per_v)], p_buf[slot][u],
                       preferred_element_type=F32) for u in us]
        olds = [acc[j0 + u] for u in us]
        for u, old, pv in zip(us, olds, pvs):
            acc[j0 + u] = ((old.reshape(rows // 8, 8, COL_TILE) * al_buf[slot][u][None])
                           .reshape(rows, COL_TILE) + pv)

    def sweep(masked, first_step):
        def trip(k, slot, run_b=True, run_c=True):
            kc = (k + n_trips - 2) % n_trips
            kb = (k + n_trips - 1) % n_trips
            for u in range(U):
                sees_all = tq == 2 * tk and (u % n_ct) * COL_TILE >= tk - 1
                stage_a(k * U + u, slot, u, masked and not sees_all)
            if run_b:
                stage_b(kb * U, 1 - slot, range(U))
            if run_c:
                stage_c(kc * U, slot, jnp.where(k >= 2, par, 1 - par), range(U))

        def trip_pair(half, carry):
            trip(2 * half, 0)
            trip(2 * half + 1, 1)
            return carry

        if first_step:
            trip(0, 0, run_b=False, run_c=False)
            trip(1, 1, run_c=False)
            lax.fori_loop(1, n_trips // 2, trip_pair, 0)
        else:
            lax.fori_loop(0, n_trips // 2, trip_pair, 0)

    assert U % n_ct == 0
    live = tuple(u for u in range(U) if (u % n_ct) * COL_TILE >= tq - tk)

    def last_sweep():
        def trip(k, slot, b_from_prev_step, c_from_prev_step):
            kc = (k + n_trips - 2) % n_trips
            kb = (k + n_trips - 1) % n_trips
            for u in live:
                stage_a(k * U + u, slot, u, True)
            stage_b(kb * U, 1 - slot, range(U) if b_from_prev_step else live)
            stage_c(kc * U, slot, 1 - par if c_from_prev_step else par,
                    range(U) if c_from_prev_step else live)

        def trip_pair(half, carry):
            trip(2 * half, 0, False, False)
            trip(2 * half + 1, 1, False, False)
            return carry

        trip(0, 0, True, True)
        trip(1, 1, False, True)
        lax.fori_loop(1, n_trips // 2, trip_pair, 0)

    needs_mask = kj * tk + tk - 1 > qi * tq
    is_last = kj == last_kj
    is_first = kj == 0
    for masked in (True, False):
        on_mask = jnp.logical_and(needs_mask, jnp.logical_not(is_last)) if masked \
            else jnp.logical_not(needs_mask)
        for first_step in (True, False):
            on_step = is_first if first_step else jnp.logical_not(is_first)
            pl.when(jnp.logical_and(on_mask, on_step))(
                functools.partial(sweep, masked, first_step))
    pl.when(is_last)(last_sweep)

    @pl.when(is_last)
    def _():
        last = n_trips - 1
        stage_c((last - 1) * U, (last - 1) % 2, par, live)
        stage_b(last * U, last % 2, live)
        stage_c(last * U, last % 2, par, live)

        def map_out(mi):
            parts = [acc[mi * n_ct + ct] for ct in range(n_ct)]
            full = jnp.concatenate(parts, axis=1)
            return full[0:dv] / full[dv:dv + 1]

        for gi in range(gps):
            o0, o1 = map_out(2 * gi), map_out(2 * gi + 1)
            if mode == "fox":
                o = jnp.concatenate([o0, o1], axis=0)
            else:
                lp = lam_ref[...]
                t1 = jnp.sum(lp[0:1] * lp[1:2], axis=1, keepdims=True)
                t2 = jnp.sum(lp[2:3] * lp[3:4], axis=1, keepdims=True)
                lam = jnp.exp(t1) - jnp.exp(t2) + lam_init
                o = o0 - lam * o1
                ms = jnp.mean(o * o, axis=0, keepdims=True)
                o = o * lax.rsqrt(ms + SUBLN_EPS) * g_ref[...] * (1.0 - lam_init)
            o_ref[:, gi * LANES:(gi + 1) * LANES] = o.T.astype(BF16)


def _causal_steps(s, tq, tk):
    qi, kj = [], []
    for i in range(s // tq):
        for j in range((i * tq + tq - 1) // tk + 1):
            qi.append(i)
            kj.append(j)
    return jnp.asarray(qi, jnp.int32), jnp.asarray(kj, jnp.int32)


def _flash_pair(proj_t, keys, extras, *, mode, n_groups, gps, q_row, v_row, k_col, tq, tk,
                lam_init=0.0):
    s = keys.shape[0]
    qi, kj = _causal_steps(s, tq, tk)
    gw = gps * LANES
    qt_spec = pl.BlockSpec((gw, tq), lambda g, t, qi, kj: (q_row + g, qi[t]))
    k_spec = pl.BlockSpec((tk, gw), lambda g, t, qi, kj: (kj[t], k_col + g))
    vt_spec = pl.BlockSpec((gw, tk), lambda g, t, qi, kj: (v_row + g, kj[t]))
    const = lambda x: pl.BlockSpec(x.shape, lambda g, t, qi, kj: (0, 0))
    if mode == "fox":
        (cfeat,) = extras
        in_specs = [qt_spec, k_spec,
                    pl.BlockSpec((tk, LANES), lambda g, t, qi, kj: (kj[t], 0)), vt_spec]
        args = (proj_t, keys, cfeat, proj_t)
        kd, dv, n_values = 2 * LANES, HEAD_DIM, 2 * gps
    else:
        lam_params, subln_g = extras
        in_specs = [qt_spec, k_spec, vt_spec, const(lam_params), const(subln_g)]
        args = (proj_t, keys, proj_t, lam_params, subln_g)
        kd, dv, n_values = LANES, LANES, gps
    rows = dv + ONES_ROWS
    n_strips = 2 * gps * (tq // COL_TILE)
    u = STRIPS_PER_TRIP
    assert n_strips % (2 * u) == 0
    assert tq >= 2 * tk
    kern = functools.partial(_flash_pair_kernel, mode=mode, tq=tq, tk=tk, gps=gps,
                             lam_init=lam_init)
    return pl.pallas_call(
        kern,
        grid_spec=pltpu.PrefetchScalarGridSpec(
            num_scalar_prefetch=2,
            grid=(n_groups // gps, qi.shape[0]),
            in_specs=in_specs,
            out_specs=pl.BlockSpec((tq, gw), lambda g, t, qi, kj: (qi[t], g)),
            scratch_shapes=[
                pltpu.VMEM((n_strips, kd, COL_TILE), BF16),
                pltpu.VMEM((n_strips, 8, COL_TILE), F32),
                pltpu.VMEM((n_strips, rows, COL_TILE), F32),
                pltpu.VMEM((gps, tk, kd), BF16),
                pltpu.VMEM((2, n_values, rows, tk), BF16),
            ] + [pltpu.VMEM((u, tk, COL_TILE), F32)] * 2
              + [pltpu.VMEM((u, 8, COL_TILE), F32)] * 2
              + [pltpu.VMEM((u, tk, COL_TILE), BF16)] * 2
              + [pltpu.VMEM((u, 8, COL_TILE), F32)] * 2,
        ),
        out_shape=jax.ShapeDtypeStruct((s, n_groups * LANES), BF16),
        compiler_params=_params(("arbitrary", "arbitrary")),
        name=f"flash_{mode}",
    )(qi, kj, *args)


def _hyb_out_kernel(*refs, dilations):
    n_pat = len(dilations)
    pat = refs[:2 * n_pat]
    ob_ref, w_ref, h_ref, out_ref = refs[2 * n_pat:2 * n_pat + 4]
    scratch = list(refs[2 * n_pat + 4:])
    tm = h_ref.shape[0]
    vals = []
    for idx, ref in enumerate(pat):
        dil = dilations[idx // 2]
        if dil == 1:
            vals.append(ref[...])
            continue
        buf = scratch.pop(0)
        n_chunks = buf.shape[0]
        for r in range(dil):
            for c in range(n_chunks):
                c0 = (r * n_chunks + c) * LANES
                buf[c, pl.ds(r, tm // dil, stride=dil), :] = ref[:, c0:c0 + LANES]
        vals.append(jnp.concatenate([buf[c] for c in range(n_chunks)], axis=1))
    os_, ls_ = vals[0::2], vals[1::2]
    m = functools.reduce(jnp.maximum, ls_)
    es = [jnp.exp2(l - m) for l in ls_]
    oa = sum(e * o for e, o in zip(es, os_)) / sum(es)
    mixed = jnp.concatenate([oa.astype(BF16), ob_ref[...]], axis=1)
    out_ref[...] = h_ref[...] + jnp.dot(mixed, w_ref[...], preferred_element_type=F32)


def _hyb_out(pattern_outs, dilations, ob, w, h, *, tm):
    s, d = h.shape
    wa = pattern_outs[0][0].shape[1] // dilations[0]
    in_specs, args, scratch = [], [], []
    for (o, lse), dil in zip(pattern_outs, dilations):
        for arr in (o, lse):
            in_specs.append(pl.BlockSpec((tm // dil, dil * wa), lambda i: (i, 0)))
            args.append(arr)
            if dil > 1:
                scratch.append(pltpu.VMEM((wa // LANES, tm, LANES), F32))
    return pl.pallas_call(
        functools.partial(_hyb_out_kernel, dilations=tuple(dilations)),
        grid=(s // tm,),
        in_specs=in_specs + [
            pl.BlockSpec((tm, ob.shape[1]), lambda i: (i, 0)),
            pl.BlockSpec(w.shape, lambda i: (0, 0)),
            pl.BlockSpec((tm, d), lambda i: (i, 0)),
        ],
        out_specs=pl.BlockSpec((tm, d), lambda i: (i, 0)),
        out_shape=jax.ShapeDtypeStruct((s, d), F32),
        scratch_shapes=scratch,
        compiler_params=_params(("parallel",)),
        name="hyb_out",
    )(*args, ob, w, h)


def _proj_res_kernel(a_ref, w_ref, h_ref, out_ref):
    out_ref[...] = h_ref[...] + jnp.dot(a_ref[...], w_ref[...], preferred_element_type=F32)


def _proj_res(a, w, h, *, tm):
    s, d = h.shape
    return pl.pallas_call(
        _proj_res_kernel,
        grid=(s // tm,),
        in_specs=[
            pl.BlockSpec((tm, a.shape[1]), lambda i: (i, 0)),
            pl.BlockSpec(w.shape, lambda i: (0, 0)),
            pl.BlockSpec((tm, d), lambda i: (i, 0)),
        ],
        out_specs=pl.BlockSpec((tm, d), lambda i: (i, 0)),
        out_shape=jax.ShapeDtypeStruct((s, d), F32),
        compiler_params=_params(("parallel",)),
        name="proj_res",
    )(a, w, h)


HALO = BF16_SUBLANES


FFN_SUB = 256


def _ffn_kernel(h_ref, halo_ref, g_ref, wg_ref, wu_ref, cw_ref, cb_ref, wd_ref, *rest,
                tm, final):
    if final:
        fg_ref, out_ref, n_scr, gate_scr, act_scr = rest
    else:
        out_ref, n_scr, gate_scr, act_scr = rest
    i = pl.program_id(0)
    j = pl.program_id(1)
    n_chunks, _, chunk = act_scr.shape

    @pl.when(j == 0)
    def _():
        g = g_ref[...]
        prev = jnp.where(i > 0, halo_ref[...], 0.0)
        n_scr[0:HALO] = _rms(prev, g, NORM_EPS).astype(BF16)
        n_scr[HALO:] = _rms(h_ref[...], g, NORM_EPS).astype(BF16)

    for t, c0 in enumerate(range(0, chunk, FFN_SUB)):
        w = min(FFN_SUB, chunk - c0)
        cols = slice(c0, c0 + w)
        gate_scr[t, :, 0:w] = jnp.dot(n_scr[...], wg_ref[:, cols], preferred_element_type=F32)
        up = jnp.dot(n_scr[HALO:], wu_ref[:, cols], preferred_element_type=F32)
        conv = cb_ref[:, cols]
        for k in range(CONV_WIDTH):
            start = HALO - (CONV_WIDTH - 1) + k
            conv = conv + gate_scr[t, start:start + tm, 0:w] * cw_ref[k:k + 1, cols]
        act_scr[j, :, cols] = (conv * (1.0 / (1.0 + jnp.exp(-conv))) * up).astype(BF16)

    @pl.when(j == n_chunks - 1)
    def _():
        act = jnp.concatenate([act_scr[t] for t in range(n_chunks)], axis=1)
        y = h_ref[...] + jnp.dot(act, wd_ref[...], preferred_element_type=F32)
        if final:
            y = _rms(y, fg_ref[...], NORM_EPS)
        out_ref[...] = y


def _ffn(h, g, w_up, conv_w, conv_b, w_down, final_g, *, tm, n_chunks):
    s, d = h.shape
    d_ff = w_down.shape[0]
    chunk = d_ff // n_chunks
    assert chunk * n_chunks == d_ff and chunk % LANES == 0
    n_sub = -(-chunk // FFN_SUB)
    final = final_g is not None
    once = pl.Buffered(1)
    up_mode = once if n_chunks == 1 else None
    in_specs = [
        pl.BlockSpec((tm, d), lambda i, j: (i, 0)),
        pl.BlockSpec((HALO, d), lambda i, j: (jnp.maximum(i * (tm // HALO) - 1, 0), 0)),
        pl.BlockSpec((1, d), lambda i, j: (0, 0)),
        pl.BlockSpec((d, chunk), lambda i, j: (0, j), pipeline_mode=up_mode),
        pl.BlockSpec((d, chunk), lambda i, j: (0, n_chunks + j), pipeline_mode=up_mode),
        pl.BlockSpec((CONV_WIDTH, chunk), lambda i, j: (0, j)),
        pl.BlockSpec((1, chunk), lambda i, j: (0, j)),
        pl.BlockSpec((d_ff, d), lambda i, j: (0, 0), pipeline_mode=once),
    ]
    args = [h, h, g, w_up, w_up, conv_w, conv_b, w_down]
    if final:
        in_specs.append(pl.BlockSpec((1, d), lambda i, j: (0, 0)))
        args.append(final_g)
    return pl.pallas_call(
        functools.partial(_ffn_kernel, tm=tm, final=final),
        grid=(s // tm, n_chunks),
        in_specs=in_specs,
        out_specs=pl.BlockSpec((tm, d), lambda i, j: (i, 0)),
        out_shape=jax.ShapeDtypeStruct((s, d), F32),
        scratch_shapes=[
            pltpu.VMEM((tm + HALO, d), BF16),
            pltpu.VMEM((n_sub, tm + HALO, FFN_SUB), F32),
            pltpu.VMEM((n_chunks, tm, chunk), BF16),
        ],
        compiler_params=_params(("parallel", "arbitrary")),
        name="ffn_final" if final else "ffn",
    )(*args)


def _rope_tables(s):
    inv = 1.0 / (ROPE_THETA ** (jnp.arange(0, HEAD_DIM, 2, dtype=F32) / HEAD_DIM))
    ang = jnp.arange(s, dtype=F32)[:, None] * inv[None, :]
    cos, sin = jnp.cos(ang), jnp.sin(ang)
    sign = jnp.where((jnp.arange(LANES) & 32) == 0, -1.0, 1.0).astype(F32)
    cos_l = jnp.tile(cos, (1, LANES // 32))
    sin_signed = jnp.tile(sin, (1, LANES // 32)) * sign[None, :]
    return cos_l, sin_signed, cos.T, sin.T


def kernel(x, attn_norm, ffn_norm, final_norm, hyb_w_in, hyb_b_f, hyb_w_out, diff_w_qkv,
           diff_lambda, diff_subln, diff_w_out, ffn_w_up, ffn_conv_w, ffn_conv_b, ffn_w_down):
    b, s, d = x.shape
    assert b == 1
    depth = attn_norm.shape[0]
    width = hyb_w_out.shape[1] // 2
    n_pairs = width // LANES
    n_heads_b = width // HEAD_DIM
    n_diff_heads = diff_w_out.shape[1] // LANES
    dq = n_diff_heads * LANES
    tm = min(1024, s)
    ts = min(512, s)
    tq_flash, tk_flash = min(1024, s), min(512, s)
    gps = 4
    q_scale = HEAD_DIM ** -0.5
    cos, sin_signed, cos_t, sin_t = _rope_tables(s)
    h = x[0]

    for l in range(depth):
        g_attn = attn_norm[l][None, :]
        if l % 2 == 0:
            e = l // 2
            w_in = hyb_w_in[e]
            qa_ka_va, qb, kb, vb, wf = (w_in[:, :3 * width], w_in[:, 3 * width:4 * width],
                                        w_in[:, 4 * width:5 * width], w_in[:, 5 * width:6 * width],
                                        w_in[:, 6 * width:])
            dils = tuple(dil for _, dil in DILATED_PATTERNS)
            proj, *views = _norm_proj(
                h, g_attn, jnp.concatenate([qa_ka_va, kb], axis=1).astype(BF16), cos, sin_signed,
                rope_tiles=(0, 1), scale_tiles=(0,), scale=q_scale * LOG2E, tm=tm, tn=PROJ_TILE,
                strided_tiles=3, dilations=dils[1:])
            proj_t = _norm_proj_t(h, g_attn, jnp.concatenate([qb, vb], axis=1).T.astype(BF16),
                                  cos_t, sin_t, rope_tiles=(), scale_tiles=(0,),
                                  scale=q_scale * LOG2E, tm=tm, tn=PROJ_TILE)
            wf_pad = jnp.pad(wf, ((0, 0), (0, LANES - n_heads_b))).astype(BF16)
            bf_pad = jnp.pad(hyb_b_f[e], (0, LANES - n_heads_b))[None, :]
            cfeat = _fox_gate(h, g_attn, wf_pad, bf_pad, tc=ts, n_heads=n_heads_b)
            pats = [_dilated_pattern(proj, 1, width=width, per_res=4, tq=min(1024, s))]
            pats += [_dilated_pattern(v, dil, width=width, per_res=3, tq=min(1024, s // dil))
                     for v, dil in zip(views, dils[1:])]
            ob = _flash_pair(proj_t, proj, (cfeat,), mode="fox", n_groups=n_pairs, gps=gps,
                             q_row=0, v_row=n_pairs // gps, k_col=3 * n_pairs // gps,
                             tq=tq_flash, tk=tk_flash)
            h = _hyb_out(pats, dils, ob, hyb_w_out[e].astype(BF16), h, tm=ts)
        else:
            o = l // 2
            w = diff_w_qkv[o]
            wq, wk, wv = w[:, :dq], w[:, dq:2 * dq], w[:, 2 * dq:]
            k_tiles = dq // PROJ_TILE
            keys = _norm_proj(h, g_attn, wk.astype(BF16), cos, sin_signed,
                              rope_tiles=tuple(range(k_tiles)), scale_tiles=(),
                              scale=1.0, tm=tm, tn=PROJ_TILE)
            proj_t = _norm_proj_t(h, g_attn, jnp.concatenate([wq, wv], axis=1).T.astype(BF16),
                                  cos_t, sin_t, rope_tiles=tuple(range(k_tiles)),
                                  scale_tiles=tuple(range(k_tiles)), scale=q_scale * LOG2E,
                                  tm=tm, tn=PROJ_TILE)
            lam_init = 0.8 - 0.6 * math.exp(-0.3 * l)
            att = _flash_pair(proj_t, keys, (diff_lambda[o], diff_subln[o][:, None]),
                              mode="diff", n_groups=n_diff_heads, gps=n_diff_heads, q_row=0,
                              v_row=1, k_col=0, tq=tq_flash, tk=tk_flash,
                              lam_init=lam_init)
            h = _proj_res(att, diff_w_out[o].astype(BF16), h, tm=tm)
        h = _ffn(h, ffn_norm[l][None, :], ffn_w_up[l].astype(BF16), ffn_conv_w[l],
                 ffn_conv_b[l][None, :], ffn_w_down[l].astype(BF16),
                 final_norm[None, :] if l == depth - 1 else None, tm=ts, n_chunks=1)
    return h[None]
```
